```python
import jax, jax.numpy as jnp
from jax import lax
import numpy as np

D_MODEL = 1024
BATCH = 8
SEQ = 2048
DEPTH = 2

GRID_W = 64
CTX_LEN = 256
N_MIXERS = 2
EPS = 1e-6
SSM_EXPAND = 2
D_INNER = SSM_EXPAND * D_MODEL
SSM_HEAD_DIM = 64
SSM_HEADS = D_INNER // SSM_HEAD_DIM
SSM_GROUPS = 8
HEADS_PER_GROUP = SSM_HEADS // SSM_GROUPS
SSM_STATE = 128
SSM_CONV = 3
SSM_CHUNK = 128
XBC_DIM = D_INNER + 2 * SSM_GROUPS * SSM_STATE
SSM_IN_DIM = D_INNER + XBC_DIM + 2 * SSM_HEADS
SC_WIDTH = D_MODEL
SC_CONV = 3
D_FF = 4 * D_MODEL
N_SSD_LAYERS = (DEPTH + 1) // 2
N_SC_LAYERS = DEPTH // 2

kernel_name = "hybrid_ssd_shortconv_prefix_dit"

F32 = jnp.float32


def rms_norm(x, g):
    x32 = x.astype(F32)
    y = x32 * lax.rsqrt(jnp.mean(x32 * x32, axis=-1, keepdims=True) + EPS)
    return y.astype(x.dtype) * g


def modulate(h, shift, scale):
    return h * (1 + scale) + shift


def centred_dwconv(u, w):
    K = w.shape[0]
    p = K // 2
    L = u.shape[1]
    up = jnp.pad(u, ((0, 0), (p, p), (0, 0)))
    return sum(up[:, k:k + L] * w[k] for k in range(K))


def latent_rowconv(u, w):
    b, L, ch = u.shape
    rows = L // GRID_W
    y = centred_dwconv(u.reshape(b * rows, GRID_W, ch), w)
    return y.reshape(b, L, ch)


def ssd_chunked(xh, dt, la, Bm, Cm, h0, want_y):
    b, L = xh.shape[:2]
    Q = SSM_CHUNK
    nc = L // Q
    chunk = lambda t: t.reshape((b, nc, Q) + t.shape[2:])
    xc, dtc, lac, Bc, Cc = chunk(xh), chunk(dt), chunk(la), chunk(Bm), chunk(Cm)
    cs = jnp.cumsum(lac, axis=2)
    total = cs[:, :, -1]
    w_end = jnp.exp(total[:, :, None] - cs) * dtc
    states = jnp.einsum('bckgn,bckgr,bckgrp->bcgrpn', Bc, w_end, xc)

    def step(h, inp):
        dec, s = inp
        return jnp.exp(dec)[..., None, None] * h + s, h

    h_last, h_in = lax.scan(step, h0, (jnp.moveaxis(total, 1, 0), jnp.moveaxis(states, 1, 0)))
    if not want_y:
        return None, h_last
    h_in = jnp.moveaxis(h_in, 0, 1)
    mask = jnp.tril(jnp.ones((Q, Q), dtype=bool))[:, :, None, None]
    seg = cs[:, :, :, None] - cs[:, :, None, :]
    decay = jnp.where(mask, jnp.exp(jnp.where(mask, seg, 0.0)), 0.0)
    cb = jnp.einsum('bcqgn,bckgn->bcqkg', Cc, Bc)
    y_diag = jnp.einsum('bcqkg,bcqkgr,bckgr,bckgrp->bcqgrp', cb, decay, dtc, xc)
    y_off = jnp.einsum('bcqgn,bcqgr,bcgrpn->bcqgrp', Cc, jnp.exp(cs), h_in)
    return (y_diag + y_off).reshape(xh.shape), h_last


def ssd_inputs(h, w_in, conv_w, conv_b, conv_fn):
    b, L, _ = h.shape
    proj = h @ w_in
    z, xbc, dt_raw = jnp.split(proj, [D_INNER, D_INNER + XBC_DIM], axis=-1)
    xbc = jax.nn.silu(conv_fn(xbc, conv_w) + conv_b)
    xs, Bm, Cm = jnp.split(xbc, [D_INNER, D_INNER + SSM_GROUPS * SSM_STATE], axis=-1)
    xh = xs.reshape(b, L, SSM_GROUPS, HEADS_PER_GROUP, SSM_HEAD_DIM)
    Bm = Bm.reshape(b, L, SSM_GROUPS, SSM_STATE)
    Cm = Cm.reshape(b, L, SSM_GROUPS, SSM_STATE)
    dt_raw = dt_raw.reshape(b, L, 2, SSM_GROUPS, HEADS_PER_GROUP)
    return z, xh, Bm, Cm, dt_raw


def ssd_direction(inputs, d, dt_bias, A, h0, want_y):
    _, xh, Bm, Cm, dt_raw = inputs
    dt = jax.nn.softplus(dt_raw[:, :, d].astype(F32) +
                         dt_bias[d].astype(F32).reshape(SSM_GROUPS, HEADS_PER_GROUP))
    la = dt * A[d].reshape(SSM_GROUPS, HEADS_PER_GROUP)
    f = (lambda t: jnp.flip(t, axis=1)) if d == 1 else (lambda t: t)
    y, h_last = ssd_chunked(f(xh.astype(F32)), f(dt), f(la), f(Bm.astype(F32)),
                            f(Cm.astype(F32)), h0, want_y)
    return (f(y) if want_y else None), h_last


def ssd_finish(inputs, y, d_skip, norm_g, w_out):
    z, xh = inputs[0], inputs[1]
    y = y + d_skip.astype(F32).reshape(SSM_GROUPS, HEADS_PER_GROUP, 1) * xh.astype(F32)
    y = y.reshape(z.shape) * jax.nn.silu(z.astype(F32))
    yg = y.reshape(z.shape[:-1] + (SSM_GROUPS, D_INNER // SSM_GROUPS))
    yg = yg * lax.rsqrt(jnp.mean(yg * yg, axis=-1, keepdims=True) + EPS)
    y = yg.reshape(z.shape).astype(z.dtype) * norm_g
    return y @ w_out


def ssd_mixer(h, hc, w_in, conv_w, conv_b, dt_bias, a_log, d_skip, norm_g, w_out, want_ctx_y):
    lat = ssd_inputs(h, w_in, conv_w, conv_b, latent_rowconv)
    con = ssd_inputs(hc, w_in, conv_w, conv_b, centred_dwconv)
    A = -jnp.exp(a_log.astype(F32))
    h0 = jnp.zeros((h.shape[0], SSM_GROUPS, HEADS_PER_GROUP, SSM_HEAD_DIM, SSM_STATE), F32)
    y_lat = 0.0
    y_ctx = 0.0
    for d in range(2):
        yc_d, hc_d = ssd_direction(con, d, dt_bias, A, h0, want_ctx_y)
        yl_d, _ = ssd_direction(lat, d, dt_bias, A, hc_d, True)
        y_lat = y_lat + yl_d
        if want_ctx_y:
            y_ctx = y_ctx + yc_d
    out_lat = ssd_finish(lat, y_lat, d_skip, norm_g, w_out)
    out_ctx = ssd_finish(con, y_ctx, d_skip, norm_g, w_out) if want_ctx_y else None
    return out_lat, out_ctx


def shortconv_mixer(h, w_in, conv_w, w_out, conv_fn):
    Bg, Cg, xv = jnp.split(h @ w_in, 3, axis=-1)
    return (Bg * conv_fn(Cg * xv, conv_w)) @ w_out


def sqrelu_mlp(h, w1, w2):
    return jnp.square(jax.nn.relu(h @ w1)) @ w2


def setup_inputs(seed: int = 0) -> dict:
    key = jax.random.key(seed)
    ks = jax.random.split(key, 24)
    nrm = lambda k, shape, s: jax.random.normal(k, shape, F32) * s
    H = SSM_HEADS
    dt0 = jnp.exp(jax.random.uniform(ks[10], (N_SSD_LAYERS, 2, H), F32,
                                     float(np.log(1e-3)), float(np.log(1e-1))))
    return {
        "x": nrm(ks[0], (BATCH, SEQ, D_MODEL), 1.0),
        "c": nrm(ks[1], (BATCH, D_MODEL), 1.0),
        "ctx": nrm(ks[2], (BATCH, CTX_LEN, D_MODEL), 1.0),
        "c_ctx": nrm(ks[3], (D_MODEL,), 1.0),
        "ada_w": nrm(ks[4], (DEPTH, D_MODEL, 6 * D_MODEL), 0.3 * D_MODEL ** -0.5),
        "ada_b": nrm(ks[5], (DEPTH, 6 * D_MODEL), 0.02),
        "norm_mix_g": 1.0 + nrm(ks[6], (DEPTH, D_MODEL), 0.02),
        "norm_mlp_g": 1.0 + nrm(ks[7], (DEPTH, D_MODEL), 0.02),
        "ssd_w_in": nrm(ks[8], (N_SSD_LAYERS, D_MODEL, SSM_IN_DIM), D_MODEL ** -0.5),
        "ssd_conv_w": nrm(ks[9], (N_SSD_LAYERS, SSM_CONV, XBC_DIM), SSM_CONV ** -0.5),
        "ssd_conv_b": nrm(ks[11], (N_SSD_LAYERS, XBC_DIM), 0.02),
        "ssd_dt_bias": dt0 + jnp.log(-jnp.expm1(-dt0)),
        "ssd_a_log": jnp.log(jax.random.uniform(ks[12], (N_SSD_LAYERS, 2, H), F32, 1.0, 16.0)),
        "ssd_d": 1.0 + nrm(ks[13], (N_SSD_LAYERS, H), 0.02),
        "ssd_norm_g": 1.0 + nrm(ks[14], (N_SSD_LAYERS, D_INNER), 0.02),
        "ssd_w_out": nrm(ks[15], (N_SSD_LAYERS, D_INNER, D_MODEL), D_INNER ** -0.5),
        "sc_w_in": nrm(ks[16], (N_SC_LAYERS, D_MODEL, 3 * SC_WIDTH), D_MODEL ** -0.5),
        "sc_conv_w": nrm(ks[17], (N_SC_LAYERS, SC_CONV, SC_WIDTH), SC_CONV ** -0.5),
        "sc_w_out": nrm(ks[18], (N_SC_LAYERS, SC_WIDTH, D_MODEL), SC_WIDTH ** -0.5),
        "mlp_w1": nrm(ks[19], (DEPTH, D_MODEL, D_FF), D_MODEL ** -0.5),
        "mlp_w2": nrm(ks[20], (DEPTH, D_FF, D_MODEL), D_FF ** -0.5),
        "final_norm_g": 1.0 + nrm(ks[21], (D_MODEL,), 0.02),
    }


def reference(x, c, ctx, c_ctx, ada_w, ada_b, norm_mix_g, norm_mlp_g, ssd_w_in, ssd_conv_w,
              ssd_conv_b, ssd_dt_bias, ssd_a_log, ssd_d, ssd_norm_g, ssd_w_out, sc_w_in,
              sc_conv_w, sc_w_out, mlp_w1, mlp_w2, final_norm_g):
    for i in range(DEPTH):
        kind = i % N_MIXERS
        k = i // N_MIXERS
        ctx_after = any(j % N_MIXERS == 0 for j in range(i + 1, DEPTH))
        need_ctx_here = (kind == 0) or ctx_after

        mod = (jax.nn.silu(c) @ ada_w[i] + ada_b[i])[:, None, :]
        sh_m, sc_m, g_m, sh_f, sc_f, g_f = jnp.split(mod, 6, axis=-1)
        h = modulate(rms_norm(x, norm_mix_g[i]), sh_m, sc_m)
        if need_ctx_here:
            mod_c = jax.nn.silu(c_ctx) @ ada_w[i] + ada_b[i]
            csh_m, csc_m, cg_m, csh_f, csc_f, cg_f = jnp.split(mod_c, 6)
            hc = modulate(rms_norm(ctx, norm_mix_g[i]), csh_m, csc_m)

        if kind == 0:
            y, yc = ssd_mixer(h, hc, ssd_w_in[k], ssd_conv_w[k], ssd_conv_b[k], ssd_dt_bias[k],
                              ssd_a_log[k], ssd_d[k], ssd_norm_g[k], ssd_w_out[k], ctx_after)
        else:
            y = shortconv_mixer(h, sc_w_in[k], sc_conv_w[k], sc_w_out[k], latent_rowconv)
            yc = (shortconv_mixer(hc, sc_w_in[k], sc_conv_w[k], sc_w_out[k], centred_dwconv)
                  if ctx_after else None)

        x = x + g_m * y
        x = x + g_f * sqrelu_mlp(modulate(rms_norm(x, norm_mlp_g[i]), sh_f, sc_f),
                                 mlp_w1[i], mlp_w2[i])
        if ctx_after:
            ctx = ctx + cg_m * yc
            ctx = ctx + cg_f * sqrelu_mlp(modulate(rms_norm(ctx, norm_mlp_g[i]), csh_f, csc_f),
                                          mlp_w1[i], mlp_w2[i])
    return rms_norm(x, final_norm_g)
```

```python
import functools

import jax
import jax.numpy as jnp
from jax import lax
from jax.experimental import pallas as pl
from jax.experimental.pallas import tpu as pltpu

F32 = jnp.float32
BF16 = jnp.bfloat16

EPS = 1e-6
GRID_W = 64
SSM_HEAD_DIM = 64
SSM_GROUPS = 8
HEADS_PER_GROUP = 4
SSM_STATE = 128
SSM_CHUNK = 128
GROUP_WIDTH = HEADS_PER_GROUP * SSM_HEAD_DIM
DT_ROWS_PER_GROUP = 16
NEG_BIG = -1e30

VMEM_LIMIT_BYTES = 56 * 1024 * 1024


def _cparams(n_axes):
    return pltpu.CompilerParams(
        dimension_semantics=("arbitrary",) * n_axes,
        vmem_limit_bytes=VMEM_LIMIT_BYTES,
    )


def _const_spec(shape):
    nd = len(shape)
    return pl.BlockSpec(shape, lambda *_: (0,) * nd, pipeline_mode=pl.Buffered(1))


def _silu(u):
    return u * (1.0 / (1.0 + jnp.exp(-u)))


def _norm_mod(x, g, shift, scale):
    ms = jnp.mean(x * x, axis=-1, keepdims=True)
    y = x * lax.rsqrt(ms + EPS) * g
    return y * (1.0 + scale) + shift


def _row_conv3(u, w_ref, col0, ncols, period):
    rows = u.shape[0]
    rid = lax.broadcasted_iota(jnp.int32, (rows, 1), 0)
    pos = jnp.bitwise_and(rid, period - 1)
    prev = jnp.where(pos != 0, pltpu.roll(u, 1, 0), 0.0)
    nxt = jnp.where(pos != period - 1, pltpu.roll(u, rows - 1, 0), 0.0)
    w0 = w_ref[0:1, col0:col0 + ncols]
    w1 = w_ref[1:2, col0:col0 + ncols]
    w2 = w_ref[2:3, col0:col0 + ncols]
    return prev * w0 + u * w1 + nxt * w2


def _mod_kernel(c_ref, w_ref, b_ref, o_ref):
    s = _silu(c_ref[...])
    o_ref[...] = jnp.dot(s, w_ref[...], preferred_element_type=F32,
                         precision=lax.Precision.HIGHEST) + b_ref[...]


def _modulation(cvec, ada_w, ada_b):
    depth, d, n = ada_w.shape
    rows = cvec.shape[0]
    tn = 1536
    return pl.pallas_call(
        _mod_kernel,
        grid=(depth, n // tn),
        in_specs=[
            pl.BlockSpec((rows, d), lambda i, j: (0, 0)),
            pl.BlockSpec((None, d, tn), lambda i, j: (i, 0, j)),
            pl.BlockSpec((None, 1, tn), lambda i, j: (i, 0, j)),
        ],
        out_specs=pl.BlockSpec((None, rows, tn), lambda i, j: (i, 0, j)),
        out_shape=jax.ShapeDtypeStruct((depth, rows, n), F32),
        compiler_params=_cparams(2),
        name="adaln_mod",
    )(cvec, ada_w, ada_b.reshape(depth, 1, n))


def _ssd_in_kernel(x_ref, sh_ref, sc_ref, g_ref, wz_ref, wx_ref, wdt_ref, cw_ref, cb_ref,
                   dtb_ref, alog_ref, z_ref, xbc_ref, dt_ref, *, period, ncol):
    h = _norm_mod(x_ref[...], g_ref[...], sh_ref[...], sc_ref[...]).astype(BF16)
    z_ref[...] = jnp.dot(h, wz_ref[...], preferred_element_type=F32).astype(BF16)
    n_xbc = xbc_ref.shape[1]
    for j in range(n_xbc // ncol):
        c0 = j * ncol
        u = jnp.dot(h, wx_ref[:, c0:c0 + ncol], preferred_element_type=F32)
        u = _row_conv3(u, cw_ref, c0, ncol, period) + cb_ref[:, c0:c0 + ncol]
        xbc_ref[:, c0:c0 + ncol] = _silu(u).astype(BF16)
    raw = lax.dot_general(wdt_ref[...], h, (((1,), (1,)), ((), ())), preferred_element_type=F32)
    v = raw + dtb_ref[...]
    sp = jnp.maximum(v, 0.0) + jnp.log1p(jnp.exp(-jnp.abs(v)))
    row = lax.broadcasted_iota(jnp.int32, (dt_ref.shape[0], 1), 0)
    is_la = jnp.bitwise_and(row, DT_ROWS_PER_GROUP - 1) >= DT_ROWS_PER_GROUP // 2
    dt_ref[...] = jnp.where(is_la, sp * (-jnp.exp(alog_ref[...])), sp)


def _ssd_in_proj(x2d, mod3, mod_row_of_tile, norm_g, wz, wx, wdt_t, conv_w, conv_b, dtb_col, alog_col,
                 *, tm, period):
    rows, d = x2d.shape
    dz = wz.shape[1]
    dxbc = wx.shape[1]
    ndt = wdt_t.shape[0]
    kern = functools.partial(_ssd_in_kernel, period=period, ncol=512)
    return pl.pallas_call(
        kern,
        grid=(rows // tm,),
        in_specs=[
            pl.BlockSpec((tm, d), lambda i: (i, 0)),
            pl.BlockSpec((None, 1, d), lambda i: (mod_row_of_tile(i), 0, 0)),
            pl.BlockSpec((None, 1, d), lambda i: (mod_row_of_tile(i), 0, 1)),
            _const_spec((1, d)),
            _const_spec((d, dz)),
            _const_spec((d, dxbc)),
            _const_spec((ndt, d)),
            _const_spec((3, dxbc)),
            _const_spec((1, dxbc)),
            _const_spec((ndt, 1)),
            _const_spec((ndt, 1)),
        ],
        out_specs=[
            pl.BlockSpec((tm, dz), lambda i: (i, 0)),
            pl.BlockSpec((tm, dxbc), lambda i: (i, 0)),
            pl.BlockSpec((ndt, tm), lambda i: (0, i)),
        ],
        out_shape=[
            jax.ShapeDtypeStruct((rows, dz), BF16),
            jax.ShapeDtypeStruct((rows, dxbc), BF16),
            jax.ShapeDtypeStruct((ndt, rows), F32),
        ],
        compiler_params=_cparams(1),
        name="ssd_in_proj",
    )(x2d, mod3, mod3, norm_g, wz, wx, wdt_t, conv_w, conv_b, dtb_col, alog_col)


def _head_expand(cols):
    q = cols[0].shape[0]
    lane_head = lax.broadcasted_iota(jnp.int32, (q, GROUP_WIDTH), 1) // SSM_HEAD_DIM
    e = jnp.broadcast_to(cols[HEADS_PER_GROUP - 1], (q, GROUP_WIDTH))
    for hd in range(HEADS_PER_GROUP - 2, -1, -1):
        e = jnp.where(lane_head == hd, cols[hd], e)
    return e


def _ssd_chunk(x, bm, cm, dt16, state, *, backward, want_y):
    q = x.shape[0]
    d = 1 if backward else 0
    nh = HEADS_PER_GROUP
    dt8 = dt16[0:2 * nh, :]
    la8 = dt16[2 * nh:4 * nh, :]
    ki = lax.broadcasted_iota(jnp.int32, (q, q), 0)
    kj = lax.broadcasted_iota(jnp.int32, (q, q), 1)
    tri = (ki >= kj) if backward else (ki <= kj)
    cs8 = jnp.dot(la8, tri.astype(F32), preferred_element_type=F32,
                  precision=lax.Precision.HIGHEST)
    total8 = cs8[:, 0:1] if backward else cs8[:, q - 1:q]
    ecs8 = jnp.exp(cs8)
    w8 = jnp.exp(total8 - cs8) * dt8
    stacked = jnp.concatenate(
        [cs8, ecs8, w8, jnp.zeros((q - 6 * nh, q), F32)], axis=0)
    colmat = stacked.T
    r0 = d * nh
    cs_row = cs8[r0:r0 + nh, :]
    dt_row = dt8[r0:r0 + nh, :]
    cs_col = [colmat[:, r0 + hd:r0 + hd + 1] for hd in range(nh)]
    ecs_col = [colmat[:, 2 * nh + r0 + hd:2 * nh + r0 + hd + 1] for hd in range(nh)]
    w_col = [colmat[:, 4 * nh + r0 + hd:4 * nh + r0 + hd + 1] for hd in range(nh)]

    e_ecs = _head_expand(ecs_col)
    y = None
    if want_y:
        cb = lax.dot_general(cm, bm, (((1,), (1,)), ((), ())), preferred_element_type=F32)
        valid = (ki <= kj) if backward else (ki >= kj)
        lane_head = lax.broadcasted_iota(jnp.int32, (q, GROUP_WIDTH), 1) // SSM_HEAD_DIM
        m_parts = []
        x_parts = []
        for hd in range(nh):
            seg = cs_col[hd] - cs_row[hd:hd + 1, :]
            m = jnp.exp(jnp.where(valid, seg, NEG_BIG)) * cb * dt_row[hd:hd + 1, :]
            m_parts.append(m.astype(BF16))
            x_parts.append(jnp.where(lane_head == hd, x, jnp.zeros_like(x)))
        m_all = jnp.concatenate(m_parts, axis=1)
        x_bd = jnp.concatenate(x_parts, axis=0)
        y_diag = jnp.dot(m_all, x_bd, preferred_element_type=F32)
        y_off = jnp.dot(cm, state.astype(BF16), preferred_element_type=F32)
        y = y_diag + e_ecs * y_off

    xw = (x.astype(F32) * _head_expand(w_col)).astype(BF16)
    s_new = lax.dot_general(bm, xw, (((0,), (0,)), ((), ())), preferred_element_type=F32)
    e_tot = e_ecs[0:1, :] if backward else e_ecs[q - 1:q, :]
    return y, state * e_tot + s_new


def _ssd_scan_kernel(xl_ref, bl_ref, cl_ref, zl_ref, dtl_ref, xc_ref, bc_ref, dtc_ref,
                     dsk_ref, ng_ref, o_ref, yacc_ref):
    q = SSM_CHUNK
    n_lat = xl_ref.shape[0] // q
    n_ctx = xc_ref.shape[0] // q
    zero_state = jnp.zeros((SSM_STATE, GROUP_WIDTH), F32)

    def ctx_chunk(c, state, backward):
        r0 = c * q
        _, state = _ssd_chunk(xc_ref[r0:r0 + q, :], bc_ref[r0:r0 + q, :], None,
                              dtc_ref[:, r0:r0 + q], state, backward=backward, want_y=False)
        return state

    def lat_chunk(c, state, backward):
        r0 = pl.multiple_of(c * q, q)
        return _ssd_chunk(xl_ref[pl.ds(r0, q), :], bl_ref[pl.ds(r0, q), :], cl_ref[pl.ds(r0, q), :],
                          dtl_ref[:, pl.ds(r0, q)], state, backward=backward, want_y=True)

    state = zero_state
    for c in range(n_ctx):
        state = ctx_chunk(c, state, False)

    def fwd_body(c, state):
        y, state = lat_chunk(c, state, False)
        r0 = pl.multiple_of(c * q, q)
        yacc_ref[pl.ds(r0, q), :] = y
        return state

    lax.fori_loop(0, n_lat, fwd_body, state)

    state = zero_state
    for c in range(n_ctx - 1, -1, -1):
        state = ctx_chunk(c, state, True)

    def bwd_body(i, state):
        c = n_lat - 1 - i
        y, state = lat_chunk(c, state, True)
        r0 = pl.multiple_of(c * q, q)
        y = y + yacc_ref[pl.ds(r0, q), :]
        y = y + dsk_ref[...] * xl_ref[pl.ds(r0, q), :].astype(F32)
        y = y * _silu(zl_ref[pl.ds(r0, q), :].astype(F32))
        y = y * lax.rsqrt(jnp.mean(y * y, axis=-1, keepdims=True) + EPS)
        o_ref[pl.ds(r0, q), :] = (y * ng_ref[...]).astype(BF16)
        return state

    lax.fori_loop(0, n_lat, bwd_body, state)


def _ssd_scan(z_l, xbc_l, dt_l, xbc_c, dt_c, dskip_row, ng_row, *, batch, seq, ctx_len):
    g = SSM_GROUPS
    gw = GROUP_WIDTH
    n = SSM_STATE
    b_off = (g * gw) // n
    c_off = b_off + g
    return pl.pallas_call(
        _ssd_scan_kernel,
        grid=(batch, g),
        in_specs=[
            pl.BlockSpec((seq, gw), lambda b, k: (b, k)),
            pl.BlockSpec((seq, n), lambda b, k: (b, b_off + k)),
            pl.BlockSpec((seq, n), lambda b, k: (b, c_off + k)),
            pl.BlockSpec((seq, gw), lambda b, k: (b, k)),
            pl.BlockSpec((DT_ROWS_PER_GROUP, seq), lambda b, k: (k, b)),
            pl.BlockSpec((ctx_len, gw), lambda b, k: (b, k)),
            pl.BlockSpec((ctx_len, n), lambda b, k: (b, b_off + k)),
            pl.BlockSpec((DT_ROWS_PER_GROUP, ctx_len), lambda b, k: (k, b)),
            pl.BlockSpec((1, gw), lambda b, k: (0, k)),
            pl.BlockSpec((1, gw), lambda b, k: (0, k)),
        ],
        out_specs=pl.BlockSpec((seq, gw), lambda b, k: (b, k)),
        out_shape=jax.ShapeDtypeStruct((batch * seq, g * gw), BF16),
        scratch_shapes=[pltpu.VMEM((seq, gw), F32)],
        compiler_params=_cparams(2),
        name="ssd_scan",
    )(xbc_l, xbc_l, xbc_l, z_l, dt_l, xbc_c, xbc_c, dt_c, dskip_row, ng_row)


def _mlp_tail(x1, g_ref, sh_ref, sc_ref, gate_ref, w1_ref, w2_ref, nff):
    h2 = _norm_mod(x1, g_ref[...], sh_ref[...], sc_ref[...]).astype(BF16)
    dff = w1_ref.shape[1]
    acc = None
    for j in range(dff // nff):
        c0 = j * nff
        a = jnp.dot(h2, w1_ref[:, c0:c0 + nff], preferred_element_type=F32)
        a = jnp.square(jnp.maximum(a, 0.0)).astype(BF16)
        p = jnp.dot(a, w2_ref[c0:c0 + nff, :], preferred_element_type=F32)
        acc = p if acc is None else acc + p
    return x1 + gate_ref[...] * acc


def _ssd_out_kernel(y_ref, x_ref, gm_ref, shf_ref, scf_ref, gf_ref, ng_ref,
                    wo_ref, w1_ref, w2_ref, o_ref):
    y = jnp.dot(y_ref[...], wo_ref[...], preferred_element_type=F32)
    x1 = x_ref[...] + gm_ref[...] * y
    o_ref[...] = _mlp_tail(x1, ng_ref, shf_ref, scf_ref, gf_ref, w1_ref, w2_ref, 1024)


def _mod_spec(d, row_of_tile, k):
    return pl.BlockSpec((None, 1, d), lambda i: (row_of_tile(i), 0, k))


def _ssd_out_mlp(y2d, x2d, mod3, row_of_tile, norm_g, wo, w1, w2, *, tm):
    rows, d = x2d.shape
    di = y2d.shape[1]
    dff = w1.shape[1]
    return pl.pallas_call(
        _ssd_out_kernel,
        grid=(rows // tm,),
        in_specs=[
            pl.BlockSpec((tm, di), lambda i: (i, 0)),
            pl.BlockSpec((tm, d), lambda i: (i, 0)),
            _mod_spec(d, row_of_tile, 2),
            _mod_spec(d, row_of_tile, 3),
            _mod_spec(d, row_of_tile, 4),
            _mod_spec(d, row_of_tile, 5),
            _const_spec((1, d)),
            _const_spec((di, d)),
            _const_spec((d, dff)),
            _const_spec((dff, d)),
        ],
        out_specs=pl.BlockSpec((tm, d), lambda i: (i, 0)),
        out_shape=jax.ShapeDtypeStruct((rows, d), F32),
        compiler_params=_cparams(1),
        name="ssd_out_mlp",
    )(y2d, x2d, mod3, mod3, mod3, mod3, norm_g, wo, w1, w2)


def _sc_layer_kernel(x_ref, shm_ref, scm_ref, gm_ref, shf_ref, scf_ref, gf_ref, ngm_ref, ngf_ref,
                     fg_ref, wi_ref, cw_ref, wo_ref, w1_ref, w2_ref, o_ref, *, period):
    x = x_ref[...]
    h = _norm_mod(x, ngm_ref[...], shm_ref[...], scm_ref[...]).astype(BF16)
    w = wo_ref.shape[0]
    bg = jnp.dot(h, wi_ref[:, 0:w], preferred_element_type=F32)
    cg = jnp.dot(h, wi_ref[:, w:2 * w], preferred_element_type=F32)
    xv = jnp.dot(h, wi_ref[:, 2 * w:3 * w], preferred_element_type=F32)
    u = (bg * _row_conv3(cg * xv, cw_ref, 0, w, period)).astype(BF16)
    y = jnp.dot(u, wo_ref[...], preferred_element_type=F32)
    x1 = x + gm_ref[...] * y
    x2 = _mlp_tail(x1, ngf_ref, shf_ref, scf_ref, gf_ref, w1_ref, w2_ref, 1024)
    ms = jnp.mean(x2 * x2, axis=-1, keepdims=True)
    o_ref[...] = x2 * lax.rsqrt(ms + EPS) * fg_ref[...]


def _sc_layer(x2d, mod3, row_of_tile, ng_mix, ng_mlp, final_g, wi, conv_w, wo, w1, w2, *, tm, period):
    rows, d = x2d.shape
    dff = w1.shape[1]
    kern = functools.partial(_sc_layer_kernel, period=period)
    return pl.pallas_call(
        kern,
        grid=(rows // tm,),
        in_specs=[
            pl.BlockSpec((tm, d), lambda i: (i, 0)),
            _mod_spec(d, row_of_tile, 0),
            _mod_spec(d, row_of_tile, 1),
            _mod_spec(d, row_of_tile, 2),
            _mod_spec(d, row_of_tile, 3),
            _mod_spec(d, row_of_tile, 4),
            _mod_spec(d, row_of_tile, 5),
            _const_spec((1, d)),
            _const_spec((1, d)),
            _const_spec((1, d)),
            _const_spec(wi.shape),
            _const_spec(conv_w.shape),
            _const_spec(wo.shape),
            _const_spec((d, dff)),
            _const_spec((dff, d)),
        ],
        out_specs=pl.BlockSpec((tm, d), lambda i: (i, 0)),
        out_shape=jax.ShapeDtypeStruct((rows, d), F32),
        compiler_params=_cparams(1),
        name="shortconv_layer",
    )(x2d, mod3, mod3, mod3, mod3, mod3, mod3, ng_mix, ng_mlp, final_g, wi, conv_w, wo, w1, w2)


def kernel(x, c, ctx, c_ctx, ada_w, ada_b, norm_mix_g, norm_mlp_g, ssd_w_in, ssd_conv_w, ssd_conv_b,
           ssd_dt_bias, ssd_a_log, ssd_d, ssd_norm_g, ssd_w_out, sc_w_in, sc_conv_w, sc_w_out,
           mlp_w1, mlp_w2, final_norm_g):
    batch, seq, d = x.shape
    ctx_len = ctx.shape[1]
    depth = ada_w.shape[0]
    assert depth == 2 and ssd_w_in.shape[0] == 1 and sc_w_in.shape[0] == 1
    d_inner = ssd_w_out.shape[1]
    n_heads = ssd_d.shape[1]
    xbc_dim = ssd_conv_w.shape[2]
    assert n_heads == SSM_GROUPS * HEADS_PER_GROUP and d_inner == SSM_GROUPS * GROUP_WIDTH

    mod_rows = 16
    cvec = jnp.zeros((mod_rows, d), F32).at[:batch].set(c).at[batch].set(c_ctx)
    mod = _modulation(cvec, ada_w, ada_b)
    mod0 = mod[0].reshape(mod_rows, 1, 6 * d)
    mod1 = mod[1].reshape(mod_rows, 1, 6 * d)

    w_in = ssd_w_in[0]
    wz = w_in[:, :d_inner].astype(BF16)
    wx = w_in[:, d_inner:d_inner + xbc_dim].astype(BF16)
    gi = jnp.arange(SSM_GROUPS)[:, None, None]
    di_ = jnp.arange(2)[None, :, None]
    ri = jnp.arange(HEADS_PER_GROUP)[None, None, :]
    flat = (di_ * n_heads + gi * HEADS_PER_GROUP + ri).reshape(SSM_GROUPS, 2 * HEADS_PER_GROUP)
    dt_idx = jnp.concatenate([flat, flat], axis=1).reshape(-1)
    wdt_t = w_in[:, d_inner + xbc_dim:][:, dt_idx].T.astype(BF16)
    dtb_col = ssd_dt_bias[0].reshape(-1)[dt_idx].reshape(-1, 1).astype(F32)
    alog_col = ssd_a_log[0].reshape(-1)[dt_idx].reshape(-1, 1).astype(F32)
    conv_w = ssd_conv_w[0]
    conv_b = ssd_conv_b[0].reshape(1, xbc_dim)
    ng_mix0 = norm_mix_g[0].reshape(1, d)

    tm = 512
    x2d = x.reshape(batch * seq, d)
    ctx2d = ctx.reshape(batch * ctx_len, d)
    lat_row = lambda i: (i * tm) // seq
    z_l, xbc_l, dt_l = _ssd_in_proj(x2d, mod0, lat_row, ng_mix0, wz, wx, wdt_t, conv_w, conv_b,
                                    dtb_col, alog_col, tm=tm, period=GRID_W)
    _, xbc_c, dt_c = _ssd_in_proj(ctx2d, mod0, lambda i: batch, ng_mix0, wz, wx, wdt_t, conv_w, conv_b,
                                  dtb_col, alog_col, tm=ctx_len, period=ctx_len)

    dskip_row = jnp.repeat(ssd_d[0].astype(F32), SSM_HEAD_DIM).reshape(1, d_inner)
    ng_row = ssd_norm_g[0].reshape(1, d_inner)
    y_fin = _ssd_scan(z_l, xbc_l, dt_l, xbc_c, dt_c, dskip_row, ng_row,
                      batch=batch, seq=seq, ctx_len=ctx_len)

    x1 = _ssd_out_mlp(y_fin, x2d, mod0, lat_row, norm_mlp_g[0].reshape(1, d),
                      ssd_w_out[0].astype(BF16), mlp_w1[0].astype(BF16), mlp_w2[0].astype(BF16), tm=tm)

    out = _sc_layer(x1, mod1, lat_row, norm_mix_g[1].reshape(1, d), norm_mlp_g[1].reshape(1, d),
                    final_norm_g.reshape(1, d), sc_w_in[0].astype(BF16), sc_conv_w[0],
                    sc_w_out[0].astype(BF16), mlp_w1[1].astype(BF16), mlp_w2[1].astype(BF16),
                    tm=tm, period=GRID_W)
    return out.reshape(batch, seq, d)
```

```python
import functools

import numpy as np

import jax
import jax.numpy as jnp
from jax import lax
from jax.experimental import pallas as pl
from jax.experimental.pallas import tpu as pltpu

F32 = jnp.float32
BF16 = jnp.bfloat16

EPS = 1e-6
GRID_W = 64
SSM_HEAD_DIM = 64
SSM_GROUPS = 8
HEADS_PER_GROUP = 4
SSM_STATE = 128
SSM_CHUNK = 128
GROUP_WIDTH = HEADS_PER_GROUP * SSM_HEAD_DIM
DT_ROWS_PER_GROUP = 16
NEG_BIG = -1e30

VMEM_LIMIT_BYTES = 56 * 1024 * 1024


def _cparams(n_axes):
    return pltpu.CompilerParams(
        dimension_semantics=("arbitrary",) * n_axes,
        vmem_limit_bytes=VMEM_LIMIT_BYTES,
    )


def _const_spec(shape):
    nd = len(shape)
    return pl.BlockSpec(shape, lambda *_: (0,) * nd, pipeline_mode=pl.Buffered(1))


def _silu(u):
    return u * (1.0 / (1.0 + jnp.exp(-u)))


def _norm_mod(x, g, shift, scale):
    ms = jnp.mean(x * x, axis=-1, keepdims=True)
    y = x * lax.rsqrt(ms + EPS) * g
    return y * (1.0 + scale) + shift


def _row_conv3(u, w_ref, col0, ncols, period):
    rows = u.shape[0]
    rid = lax.broadcasted_iota(jnp.int32, (rows, 1), 0)
    pos = jnp.bitwise_and(rid, period - 1)
    prev = jnp.where(pos != 0, pltpu.roll(u, 1, 0), 0.0)
    nxt = jnp.where(pos != period - 1, pltpu.roll(u, rows - 1, 0), 0.0)
    w0 = w_ref[0:1, col0:col0 + ncols]
    w1 = w_ref[1:2, col0:col0 + ncols]
    w2 = w_ref[2:3, col0:col0 + ncols]
    return prev * w0 + u * w1 + nxt * w2


def _mod_kernel(c_ref, w_ref, b_ref, o_ref):
    s = _silu(c_ref[...])
    o_ref[...] = jnp.dot(s, w_ref[...], preferred_element_type=F32,
                         precision=lax.Precision.HIGHEST) + b_ref[...]


def _modulation(cvec, ada_w, ada_b):
    depth, d, n = ada_w.shape
    rows = cvec.shape[0]
    tn = 1536
    return pl.pallas_call(
        _mod_kernel,
        grid=(depth, n // tn),
        in_specs=[
            pl.BlockSpec((rows, d), lambda i, j: (0, 0)),
            pl.BlockSpec((None, d, tn), lambda i, j: (i, 0, j)),
            pl.BlockSpec((None, 1, tn), lambda i, j: (i, 0, j)),
        ],
        out_specs=pl.BlockSpec((None, rows, tn), lambda i, j: (i, 0, j)),
        out_shape=jax.ShapeDtypeStruct((depth, rows, n), F32),
        compiler_params=_cparams(2),
        name="adaln_mod",
    )(cvec, ada_w, ada_b.reshape(depth, 1, n))


_NT_DIMS = (((1,), (1,)), ((), ()))


def _ssd_in_kernel(x_ref, sh_ref, sc_ref, g_ref, wz_ref, wxc_ref, wbt_ref, wdt_ref, cw_ref, cb_ref,
                   cwb_ref, dtb_ref, alog_ref, *out_refs, period, ncol, nrow, want_z):
    if want_z:
        z_ref, xc_ref, bt_ref, dt_ref = out_refs
    else:
        xc_ref, bt_ref, dt_ref = out_refs
    h = _norm_mod(x_ref[...], g_ref[...], sh_ref[...], sc_ref[...]).astype(BF16)
    if want_z:
        z_ref[...] = jnp.dot(h, wz_ref[...], preferred_element_type=F32).astype(BF16)
    for j in range(xc_ref.shape[1] // ncol):
        c0 = j * ncol
        u = jnp.dot(h, wxc_ref[:, c0:c0 + ncol], preferred_element_type=F32)
        u = _row_conv3(u, cw_ref, c0, ncol, period) + cb_ref[:, c0:c0 + ncol]
        xc_ref[:, c0:c0 + ncol] = _silu(u).astype(BF16)
    tm = x_ref.shape[0]
    pos = jnp.bitwise_and(lax.broadcasted_iota(jnp.int32, (1, tm), 1), period - 1)
    for j in range(bt_ref.shape[0] // nrow):
        r0 = j * nrow
        u = lax.dot_general(wbt_ref[r0:r0 + nrow, :], h, _NT_DIMS, preferred_element_type=F32)
        prev = jnp.where(pos != 0, pltpu.roll(u, 1, 1), 0.0)
        nxt = jnp.where(pos != period - 1, pltpu.roll(u, tm - 1, 1), 0.0)
        cwb = cwb_ref[r0:r0 + nrow, :]
        u = prev * cwb[:, 0:1] + u * cwb[:, 1:2] + nxt * cwb[:, 2:3] + cwb[:, 3:4]
        bt_ref[r0:r0 + nrow, :] = _silu(u).astype(BF16)
    raw = lax.dot_general(wdt_ref[...], h, _NT_DIMS, preferred_element_type=F32)
    v = raw + dtb_ref[...]
    sp = jnp.maximum(v, 0.0) + jnp.log1p(jnp.exp(-jnp.abs(v)))
    row = lax.broadcasted_iota(jnp.int32, (dt_ref.shape[0], 1), 0)
    is_la = jnp.bitwise_and(row, DT_ROWS_PER_GROUP - 1) >= DT_ROWS_PER_GROUP // 2
    dt_ref[...] = jnp.where(is_la, sp * (-jnp.exp(alog_ref[...])), sp)


def _ssd_in_proj(x2d, mod3, mod_row_of_tile, norm_g, wz, wxc, wbt, wdt_t, cw_xc, cb_xc, cwb, dtb_col,
                 alog_col, *, tm, period, want_z):
    rows, d = x2d.shape
    dz = wz.shape[1]
    dxc = wxc.shape[1]
    nb = wbt.shape[0]
    ndt = wdt_t.shape[0]
    kern = functools.partial(_ssd_in_kernel, period=period, ncol=512, nrow=256, want_z=want_z)
    out_specs = [
        pl.BlockSpec((tm, dxc), lambda i: (i, 0)),
        pl.BlockSpec((nb, tm), lambda i: (0, i)),
        pl.BlockSpec((ndt, tm), lambda i: (0, i)),
    ]
    out_shape = [
        jax.ShapeDtypeStruct((rows, dxc), BF16),
        jax.ShapeDtypeStruct((nb, rows), BF16),
        jax.ShapeDtypeStruct((ndt, rows), F32),
    ]
    if want_z:
        out_specs.insert(0, pl.BlockSpec((tm, dz), lambda i: (i, 0)))
        out_shape.insert(0, jax.ShapeDtypeStruct((rows, dz), BF16))
    return pl.pallas_call(
        kern,
        grid=(rows // tm,),
        in_specs=[
            pl.BlockSpec((tm, d), lambda i: (i, 0)),
            pl.BlockSpec((None, 1, d), lambda i: (mod_row_of_tile(i), 0, 0)),
            pl.BlockSpec((None, 1, d), lambda i: (mod_row_of_tile(i), 0, 1)),
            _const_spec((1, d)),
            _const_spec((d, dz)),
            _const_spec((d, dxc)),
            _const_spec((nb, d)),
            _const_spec((ndt, d)),
            _const_spec((3, dxc)),
            _const_spec((1, dxc)),
            _const_spec(cwb.shape),
            _const_spec((ndt, 1)),
            _const_spec((ndt, 1)),
        ],
        out_specs=out_specs,
        out_shape=out_shape,
        compiler_params=_cparams(1),
        name="ssd_in_proj",
    )(x2d, mod3, mod3, norm_g, wz, wxc, wbt, wdt_t, cw_xc, cb_xc, cwb, dtb_col, alog_col)


COL_CS, COL_ECS, COL_W = 0, 8, 16


def _head_expand(colmat, lane0):
    r = colmat.shape[0]
    first = lax.broadcasted_iota(jnp.int32, (r, SSM_STATE), 1) < SSM_HEAD_DIM
    cols = [colmat[:, lane0 + hd:lane0 + hd + 1] for hd in range(HEADS_PER_GROUP)]
    lo = jnp.where(first, cols[0], cols[1])
    hi = jnp.where(first, cols[2], cols[3])
    return jnp.concatenate([lo, hi], axis=1)


def _expand_select(lane0):
    r = np.arange(2 * SSM_STATE)[:, None] % SSM_STATE
    l = np.arange(2 * GROUP_WIDTH)[None, :]
    src = lane0 + (l // GROUP_WIDTH) * HEADS_PER_GROUP + (l % GROUP_WIDTH) // SSM_HEAD_DIM
    return jnp.asarray(r == src, dtype=BF16)


def _expand_both(colmat, sel_ref):
    hi = colmat.astype(BF16)
    lo = (colmat - hi.astype(F32)).astype(BF16)
    return jnp.dot(jnp.concatenate([hi, lo], axis=1), sel_ref[...], preferred_element_type=F32)


def _decay_rows(dt_ref, n_chunks, upper, lower):
    q = SSM_CHUNK
    nh2 = 2 * HEADS_PER_GROUP
    dt = jnp.concatenate([dt_ref[0:nh2, c * q:(c + 1) * q] for c in range(n_chunks)], axis=0)
    la = jnp.concatenate([dt_ref[nh2:2 * nh2, c * q:(c + 1) * q] for c in range(n_chunks)], axis=0)
    csf = jnp.dot(la, upper, preferred_element_type=F32, precision=lax.Precision.HIGHEST)
    csb = jnp.dot(la, lower, preferred_element_type=F32, precision=lax.Precision.HIGHEST)
    row = lax.broadcasted_iota(jnp.int32, (dt.shape[0], 1), 0)
    is_b = jnp.bitwise_and(row, HEADS_PER_GROUP) != 0
    cs = jnp.where(is_b, csb, csf)
    tot = jnp.where(is_b, csb[:, 0:1], csf[:, q - 1:q])
    return dt, cs, jnp.exp(cs), jnp.exp(tot - cs) * dt


def _ssd_scan_kernel(xl_ref, btl_ref, cl_ref, zl_ref, dtl_ref, xc_ref, btc_ref, dtc_ref,
                     dsk_ref, ng_ref, selw_ref, sele_ref, o_ref,
                     cols_ref, rows_ref, sloc_ref, hin_ref, etot_ref):
    q = SSM_CHUNK
    nh = HEADS_PER_GROUP
    n_lat = xl_ref.shape[0] // q
    n_ctx = xc_ref.shape[0] // q
    ki = lax.broadcasted_iota(jnp.int32, (q, q), 0)
    kj = lax.broadcasted_iota(jnp.int32, (q, q), 1)
    low_tri = ki >= kj
    up_tri = ki <= kj

    upper = up_tri.astype(F32)
    lower = low_tri.astype(F32)
    pad = jnp.zeros((q - 6 * nh, q), F32)
    for dref, n_chunks, base in ((dtc_ref, n_ctx, 0), (dtl_ref, n_lat, n_ctx)):
        dt, cs, ecs, w = _decay_rows(dref, n_chunks, upper, lower)
        for c in range(n_chunks):
            r = slice(2 * nh * c, 2 * nh * (c + 1))
            cols_ref[base + c] = jnp.concatenate([cs[r], ecs[r], w[r], pad], axis=0).T
        if base:
            rows_ref[0:2 * nh * n_lat, :] = cs
            rows_ref[2 * nh * n_lat:4 * nh * n_lat, :] = dt

    gw = GROUP_WIDTH

    def local_states(gc, x, bt):
        colmat = cols_ref[gc]
        xf = x.astype(F32)
        xw = (jnp.concatenate([xf, xf], axis=1) * _expand_both(colmat, selw_ref)).astype(BF16)
        sloc_ref[gc] = jnp.dot(bt, xw, preferred_element_type=F32)
        for d in range(2):
            p0 = 0 if d else q - 1
            etot_ref[d, gc] = _head_expand(colmat[p0:p0 + 1, :], COL_ECS + nh * d)

    for c in range(n_ctx):
        local_states(c, xc_ref[c * q:(c + 1) * q, :], btc_ref[:, c * q:(c + 1) * q])

    def a_body(c, carry):
        r0 = pl.multiple_of(c * q, q)
        local_states(n_ctx + c, xl_ref[pl.ds(r0, q), :], btl_ref[:, pl.ds(r0, q)])
        return carry

    lax.fori_loop(0, n_lat, a_body, 0, unroll=8)

    for d in range(2):
        lanes = slice(gw * d, gw * (d + 1))
        h = jnp.zeros((SSM_STATE, gw), F32)
        for c in (range(n_ctx - 1, -1, -1) if d else range(n_ctx)):
            h = h * etot_ref[d, c] + sloc_ref[c, :, lanes]

        def b_body(i, h, d=d, lanes=lanes):
            c = (n_lat - 1 - i) if d else i
            hin_ref[c, :, lanes] = h.astype(BF16)
            return h * etot_ref[d, n_ctx + c] + sloc_ref[n_ctx + c, :, lanes]

        lax.fori_loop(0, n_lat, b_body, h, unroll=4)

    lane_head = lax.broadcasted_iota(jnp.int32, (q, gw), 1) // SSM_HEAD_DIM

    def c_body(c, carry):
        r0 = pl.multiple_of(c * q, q)
        x = xl_ref[pl.ds(r0, q), :]
        cm = cl_ref[pl.ds(r0, q), :]
        colmat = cols_ref[n_ctx + c]
        r8 = pl.multiple_of(c * 2 * nh, 2 * nh)
        cs8 = rows_ref[pl.ds(r8, 2 * nh), :]
        dt8 = rows_ref[pl.ds(2 * nh * n_lat + r8, 2 * nh), :]
        cb = jnp.dot(cm, btl_ref[:, pl.ds(r0, q)], preferred_element_type=F32)
        m_parts = []
        x_parts = []
        for hd in range(nh):
            segf = colmat[:, COL_CS + hd:COL_CS + hd + 1] - cs8[hd:hd + 1, :]
            segb = colmat[:, COL_CS + nh + hd:COL_CS + nh + hd + 1] - cs8[nh + hd:nh + hd + 1, :]
            mf = jnp.exp(jnp.where(low_tri, segf, NEG_BIG)) * dt8[hd:hd + 1, :]
            mb = jnp.exp(jnp.where(up_tri, segb, NEG_BIG)) * dt8[nh + hd:nh + hd + 1, :]
            m_parts.append(((mf + mb) * cb).astype(BF16))
            x_parts.append(jnp.where(lane_head == hd, x, jnp.zeros_like(x)))
        m_all = jnp.concatenate(m_parts, axis=1)
        x_bd = jnp.concatenate(x_parts, axis=0)
        y = jnp.dot(m_all, x_bd, preferred_element_type=F32)
        y_off = _expand_both(colmat, sele_ref) * jnp.dot(cm, hin_ref[c], preferred_element_type=F32)
        y = y + y_off[:, 0:gw] + y_off[:, gw:2 * gw]
        y = y + dsk_ref[...] * x.astype(F32)
        y = y * _silu(zl_ref[pl.ds(r0, q), :].astype(F32))
        y = y * lax.rsqrt(jnp.mean(y * y, axis=-1, keepdims=True) + EPS)
        o_ref[pl.ds(r0, q), :] = (y * ng_ref[...]).astype(BF16)
        return carry

    lax.fori_loop(0, n_lat, c_body, 0, unroll=4)


def _ssd_scan(z_l, xc_l, bt_l, dt_l, xc_c, bt_c, dt_c, dskip_row, ng_row, *, batch, seq, ctx_len):
    g = SSM_GROUPS
    gw = GROUP_WIDTH
    n = SSM_STATE
    c_off = (g * gw) // n
    n_lat = seq // SSM_CHUNK
    n_all = n_lat + ctx_len // SSM_CHUNK
    sel_w = _expand_select(COL_W)
    sel_e = _expand_select(COL_ECS)
    return pl.pallas_call(
        _ssd_scan_kernel,
        grid=(batch, g),
        in_specs=[
            pl.BlockSpec((seq, gw), lambda b, k: (b, k)),
            pl.BlockSpec((n, seq), lambda b, k: (k, b)),
            pl.BlockSpec((seq, n), lambda b, k: (b, c_off + k)),
            pl.BlockSpec((seq, gw), lambda b, k: (b, k)),
            pl.BlockSpec((DT_ROWS_PER_GROUP, seq), lambda b, k: (k, b)),
            pl.BlockSpec((ctx_len, gw), lambda b, k: (b, k)),
            pl.BlockSpec((n, ctx_len), lambda b, k: (k, b)),
            pl.BlockSpec((DT_ROWS_PER_GROUP, ctx_len), lambda b, k: (k, b)),
            pl.BlockSpec((1, gw), lambda b, k: (0, k)),
            pl.BlockSpec((1, gw), lambda b, k: (0, k)),
            _const_spec(sel_w.shape),
            _const_spec(sel_e.shape),
        ],
        out_specs=pl.BlockSpec((seq, gw), lambda b, k: (b, k)),
        out_shape=jax.ShapeDtypeStruct((batch * seq, g * gw), BF16),
        scratch_shapes=[
            pltpu.VMEM((n_all, SSM_CHUNK, SSM_CHUNK), F32),
            pltpu.VMEM((4 * HEADS_PER_GROUP * n_lat, SSM_CHUNK), F32),
            pltpu.VMEM((n_all, n, 2 * gw), F32),
            pltpu.VMEM((n_lat, n, 2 * gw), BF16),
            pltpu.VMEM((2, n_all, 1, gw), F32),
        ],
        compiler_params=_cparams(2),
        name="ssd_scan",
    )(xc_l, bt_l, xc_l, z_l, dt_l, xc_c, bt_c, dt_c, dskip_row, ng_row, sel_w, sel_e)


def _mlp_tail(x1, g_ref, sh_ref, sc_ref, gate_ref, w1_ref, w2_ref, nff):
    h2 = _norm_mod(x1, g_ref[...], sh_ref[...], sc_ref[...]).astype(BF16)
    dff = w1_ref.shape[1]
    acc = None
    for j in range(dff // nff):
        c0 = j * nff
        a = jnp.dot(h2, w1_ref[:, c0:c0 + nff], preferred_element_type=F32)
        a = jnp.square(jnp.maximum(a, 0.0)).astype(BF16)
        p = jnp.dot(a, w2_ref[c0:c0 + nff, :], preferred_element_type=F32)
        acc = p if acc is None else acc + p
    return x1 + gate_ref[...] * acc


def _ssd_out_kernel(y_ref, x_ref, gm_ref, shf_ref, scf_ref, gf_ref, ng_ref,
                    wo_ref, w1_ref, w2_ref, o_ref):
    y = jnp.dot(y_ref[...], wo_ref[...], preferred_element_type=F32)
    x1 = x_ref[...] + gm_ref[...] * y
    o_ref[...] = _mlp_tail(x1, ng_ref, shf_ref, scf_ref, gf_ref, w1_ref, w2_ref, 1024)


def _mod_spec(d, row_of_tile, k):
    return pl.BlockSpec((None, 1, d), lambda i: (row_of_tile(i), 0, k))


def _ssd_out_mlp(y2d, x2d, mod3, row_of_tile, norm_g, wo, w1, w2, *, tm):
    rows, d = x2d.shape
    di = y2d.shape[1]
    dff = w1.shape[1]
    return pl.pallas_call(
        _ssd_out_kernel,
        grid=(rows // tm,),
        in_specs=[
            pl.BlockSpec((tm, di), lambda i: (i, 0)),
            pl.BlockSpec((tm, d), lambda i: (i, 0)),
            _mod_spec(d, row_of_tile, 2),
            _mod_spec(d, row_of_tile, 3),
            _mod_spec(d, row_of_tile, 4),
            _mod_spec(d, row_of_tile, 5),
            _const_spec((1, d)),
            _const_spec((di, d)),
            _const_spec((d, dff)),
            _const_spec((dff, d)),
        ],
        out_specs=pl.BlockSpec((tm, d), lambda i: (i, 0)),
        out_shape=jax.ShapeDtypeStruct((rows, d), F32),
        compiler_params=_cparams(1),
        name="ssd_out_mlp",
    )(y2d, x2d, mod3, mod3, mod3, mod3, norm_g, wo, w1, w2)


def _sc_layer_kernel(x_ref, shm_ref, scm_ref, gm_ref, shf_ref, scf_ref, gf_ref, ngm_ref, ngf_ref,
                     fg_ref, wi_ref, cw_ref, wo_ref, w1_ref, w2_ref, o_ref, *, period):
    x = x_ref[...]
    h = _norm_mod(x, ngm_ref[...], shm_ref[...], scm_ref[...]).astype(BF16)
    w = wo_ref.shape[0]
    bg = jnp.dot(h, wi_ref[:, 0:w], preferred_element_type=F32)
    cg = jnp.dot(h, wi_ref[:, w:2 * w], preferred_element_type=F32)
    xv = jnp.dot(h, wi_ref[:, 2 * w:3 * w], preferred_element_type=F32)
    u = (bg * _row_conv3(cg * xv, cw_ref, 0, w, period)).astype(BF16)
    y = jnp.dot(u, wo_ref[...], preferred_element_type=F32)
    x1 = x + gm_ref[...] * y
    x2 = _mlp_tail(x1, ngf_ref, shf_ref, scf_ref, gf_ref, w1_ref, w2_ref, 1024)
    ms = jnp.mean(x2 * x2, axis=-1, keepdims=True)
    o_ref[...] = x2 * lax.rsqrt(ms + EPS) * fg_ref[...]


def _sc_layer(x2d, mod3, row_of_tile, ng_mix, ng_mlp, final_g, wi, conv_w, wo, w1, w2, *, tm, period):
    rows, d = x2d.shape
    dff = w1.shape[1]
    kern = functools.partial(_sc_layer_kernel, period=period)
    return pl.pallas_call(
        kern,
        grid=(rows // tm,),
        in_specs=[
            pl.BlockSpec((tm, d), lambda i: (i, 0)),
            _mod_spec(d, row_of_tile, 0),
            _mod_spec(d, row_of_tile, 1),
            _mod_spec(d, row_of_tile, 2),
            _mod_spec(d, row_of_tile, 3),
            _mod_spec(d, row_of_tile, 4),
            _mod_spec(d, row_of_tile, 5),
            _const_spec((1, d)),
            _const_spec((1, d)),
            _const_spec((1, d)),
            _const_spec(wi.shape),
            _const_spec(conv_w.shape),
            _const_spec(wo.shape),
            _const_spec((d, dff)),
            _const_spec((dff, d)),
        ],
        out_specs=pl.BlockSpec((tm, d), lambda i: (i, 0)),
        out_shape=jax.ShapeDtypeStruct((rows, d), F32),
        compiler_params=_cparams(1),
        name="shortconv_layer",
    )(x2d, mod3, mod3, mod3, mod3, mod3, mod3, ng_mix, ng_mlp, final_g, wi, conv_w, wo, w1, w2)


def kernel(x, c, ctx, c_ctx, ada_w, ada_b, norm_mix_g, norm_mlp_g, ssd_w_in, ssd_conv_w, ssd_conv_b,
           ssd_dt_bias, ssd_a_log, ssd_d, ssd_norm_g, ssd_w_out, sc_w_in, sc_conv_w, sc_w_out,
           mlp_w1, mlp_w2, final_norm_g):
    batch, seq, d = x.shape
    ctx_len = ctx.shape[1]
    depth = ada_w.shape[0]
    assert depth == 2 and ssd_w_in.shape[0] == 1 and sc_w_in.shape[0] == 1
    d_inner = ssd_w_out.shape[1]
    n_heads = ssd_d.shape[1]
    xbc_dim = ssd_conv_w.shape[2]
    assert n_heads == SSM_GROUPS * HEADS_PER_GROUP and d_inner == SSM_GROUPS * GROUP_WIDTH

    mod_rows = 16
    cvec = jnp.zeros((mod_rows, d), F32).at[:batch].set(c).at[batch].set(c_ctx)
    mod = _modulation(cvec, ada_w, ada_b)
    mod0 = mod[0].reshape(mod_rows, 1, 6 * d)
    mod1 = mod[1].reshape(mod_rows, 1, 6 * d)

    w_in = ssd_w_in[0]
    wz = w_in[:, :d_inner].astype(BF16)
    nbc = SSM_GROUPS * SSM_STATE
    xs = slice(d_inner, 2 * d_inner)
    bs = slice(2 * d_inner, 2 * d_inner + nbc)
    cs_ = slice(2 * d_inner + nbc, 2 * d_inner + 2 * nbc)
    wxc = jnp.concatenate([w_in[:, xs], w_in[:, cs_]], axis=1).astype(BF16)
    wbt = w_in[:, bs].T.astype(BF16)
    cw = ssd_conv_w[0]
    cbias = ssd_conv_b[0]
    cw_xc = jnp.concatenate([cw[:, :d_inner], cw[:, d_inner + nbc:]], axis=1)
    cb_xc = jnp.concatenate([cbias[:d_inner], cbias[d_inner + nbc:]]).reshape(1, -1)
    cwb = jnp.concatenate([cw[:, d_inner:d_inner + nbc].T, cbias[d_inner:d_inner + nbc, None],
                           jnp.zeros((nbc, 4), F32)], axis=1)
    gi = jnp.arange(SSM_GROUPS)[:, None, None]
    di_ = jnp.arange(2)[None, :, None]
    ri = jnp.arange(HEADS_PER_GROUP)[None, None, :]
    flat = (di_ * n_heads + gi * HEADS_PER_GROUP + ri).reshape(SSM_GROUPS, 2 * HEADS_PER_GROUP)
    dt_idx = jnp.concatenate([flat, flat], axis=1).reshape(-1)
    wdt_t = w_in[:, d_inner + xbc_dim:][:, dt_idx].T.astype(BF16)
    dtb_col = ssd_dt_bias[0].reshape(-1)[dt_idx].reshape(-1, 1).astype(F32)
    alog_col = ssd_a_log[0].reshape(-1)[dt_idx].reshape(-1, 1).astype(F32)
    ng_mix0 = norm_mix_g[0].reshape(1, d)

    tm = 512
    x2d = x.reshape(batch * seq, d)
    ctx2d = ctx.reshape(batch * ctx_len, d)
    lat_row = lambda i: (i * tm) // seq
    z_l, xc_l, bt_l, dt_l = _ssd_in_proj(x2d, mod0, lat_row, ng_mix0, wz, wxc, wbt, wdt_t, cw_xc, cb_xc,
                                         cwb, dtb_col, alog_col, tm=tm, period=GRID_W, want_z=True)
    xc_c, bt_c, dt_c = _ssd_in_proj(ctx2d, mod0, lambda i: batch, ng_mix0, wz, wxc, wbt, wdt_t, cw_xc,
                                    cb_xc, cwb, dtb_col, alog_col, tm=ctx_len, period=ctx_len,
                                    want_z=False)

    dskip_row = jnp.repeat(ssd_d[0].astype(F32), SSM_HEAD_DIM).reshape(1, d_inner)
    ng_row = ssd_norm_g[0].reshape(1, d_inner)
    y_fin = _ssd_scan(z_l, xc_l, bt_l, dt_l, xc_c, bt_c, dt_c, dskip_row, ng_row,
                      batch=batch, seq=seq, ctx_len=ctx_len)

    x1 = _ssd_out_mlp(y_fin, x2d, mod0, lat_row, norm_mlp_g[0].reshape(1, d),
                      ssd_w_out[0].astype(BF16), mlp_w1[0].astype(BF16), mlp_w2[0].astype(BF16), tm=tm)

    out = _sc_layer(x1, mod1, lat_row, norm_mix_g[1].reshape(1, d), norm_mlp_g[1].reshape(1, d),
                    final_norm_g.reshape(1, d), sc_w_in[0].astype(BF16), sc_conv_w[0],
                    sc_w_out[0].astype(BF16), mlp_w1[1].astype(BF16), mlp_w2[1].astype(BF16),
                    tm=tm, period=GRID_W)
    return out.reshape(batch, seq, d)
```

```python
import functools

import numpy as np

import jax
import jax.numpy as jnp
from jax import lax
from jax.experimental import pallas as pl
from jax.experimental.pallas import tpu as pltpu

F32 = jnp.float32
BF16 = jnp.bfloat16

EPS = 1e-6
GRID_W = 64
SSM_HEAD_DIM = 64
SSM_GROUPS = 8
HEADS_PER_GROUP = 4
SSM_STATE = 128
SSM_CHUNK = 128
GROUP_WIDTH = HEADS_PER_GROUP * SSM_HEAD_DIM
DT_ROWS_PER_GROUP = 16
NEG_BIG = -1e30

VMEM_LIMIT_BYTES = 56 * 1024 * 1024


def _cparams(n_axes):
    return pltpu.CompilerParams(
        dimension_semantics=("arbitrary",) * n_axes,
        vmem_limit_bytes=VMEM_LIMIT_BYTES,
    )


def _const_spec(shape):
    nd = len(shape)
    return pl.BlockSpec(shape, lambda *_: (0,) * nd, pipeline_mode=pl.Buffered(1))


def _silu(u):
    return u * (1.0 / (1.0 + jnp.exp(-u)))


def _norm_mod(x, g, shift, scale):
    ms = jnp.mean(x * x, axis=-1, keepdims=True)
    y = x * lax.rsqrt(ms + EPS) * g
    return y * (1.0 + scale) + shift


def _row_conv3(u, w_ref, col0, ncols, period):
    rows = u.shape[0]
    rid = lax.broadcasted_iota(jnp.int32, (rows, 1), 0)
    pos = jnp.bitwise_and(rid, period - 1)
    prev = jnp.where(pos != 0, pltpu.roll(u, 1, 0), 0.0)
    nxt = jnp.where(pos != period - 1, pltpu.roll(u, rows - 1, 0), 0.0)
    w0 = w_ref[0:1, col0:col0 + ncols]
    w1 = w_ref[1:2, col0:col0 + ncols]
    w2 = w_ref[2:3, col0:col0 + ncols]
    return prev * w0 + u * w1 + nxt * w2


def _mod_kernel(c_ref, w_ref, b_ref, o_ref):
    s = _silu(c_ref[...])
    o_ref[...] = jnp.dot(s, w_ref[...], preferred_element_type=F32,
                         precision=lax.Precision.HIGHEST) + b_ref[...]


def _modulation(cvec, ada_w, ada_b):
    depth, d, n = ada_w.shape
    rows = cvec.shape[0]
    tn = 1536
    return pl.pallas_call(
        _mod_kernel,
        grid=(depth, n // tn),
        in_specs=[
            pl.BlockSpec((rows, d), lambda i, j: (0, 0)),
            pl.BlockSpec((None, d, tn), lambda i, j: (i, 0, j)),
            pl.BlockSpec((None, 1, tn), lambda i, j: (i, 0, j)),
        ],
        out_specs=pl.BlockSpec((None, rows, tn), lambda i, j: (i, 0, j)),
        out_shape=jax.ShapeDtypeStruct((depth, rows, n), F32),
        compiler_params=_cparams(2),
        name="adaln_mod",
    )(cvec, ada_w, ada_b.reshape(depth, 1, n))


_NT_DIMS = (((1,), (1,)), ((), ()))


def _ssd_in_kernel(x_ref, sh_ref, sc_ref, g_ref, wz_ref, wxc_ref, wbt_ref, wdt_ref, cw_ref, cb_ref,
                   cwb_ref, dtb_ref, alog_ref, *out_refs, period, ncol, nrow, want_z):
    if want_z:
        z_ref, xc_ref, bt_ref, dt_ref = out_refs
    else:
        xc_ref, bt_ref, dt_ref = out_refs
    h = _norm_mod(x_ref[...], g_ref[...], sh_ref[...], sc_ref[...]).astype(BF16)
    if want_z:
        z_ref[...] = jnp.dot(h, wz_ref[...], preferred_element_type=F32).astype(BF16)
    for j in range(xc_ref.shape[1] // ncol):
        c0 = j * ncol
        u = jnp.dot(h, wxc_ref[:, c0:c0 + ncol], preferred_element_type=F32)
        u = _row_conv3(u, cw_ref, c0, ncol, period) + cb_ref[:, c0:c0 + ncol]
        xc_ref[:, c0:c0 + ncol] = _silu(u).astype(BF16)
    tm = x_ref.shape[0]
    pos = jnp.bitwise_and(lax.broadcasted_iota(jnp.int32, (1, tm), 1), period - 1)
    for j in range(bt_ref.shape[0] // nrow):
        r0 = j * nrow
        u = lax.dot_general(wbt_ref[r0:r0 + nrow, :], h, _NT_DIMS, preferred_element_type=F32)
        prev = jnp.where(pos != 0, pltpu.roll(u, 1, 1), 0.0)
        nxt = jnp.where(pos != period - 1, pltpu.roll(u, tm - 1, 1), 0.0)
        cwb = cwb_ref[r0:r0 + nrow, :]
        u = prev * cwb[:, 0:1] + u * cwb[:, 1:2] + nxt * cwb[:, 2:3] + cwb[:, 3:4]
        bt_ref[r0:r0 + nrow, :] = _silu(u).astype(BF16)
    raw = lax.dot_general(wdt_ref[...], h, _NT_DIMS, preferred_element_type=F32)
    v = raw + dtb_ref[...]
    sp = jnp.maximum(v, 0.0) + jnp.log1p(jnp.exp(-jnp.abs(v)))
    row = lax.broadcasted_iota(jnp.int32, (dt_ref.shape[0], 1), 0)
    is_la = jnp.bitwise_and(row, DT_ROWS_PER_GROUP - 1) >= DT_ROWS_PER_GROUP // 2
    dt_ref[...] = jnp.where(is_la, sp * (-jnp.exp(alog_ref[...])), sp)


def _ssd_in_proj(x2d, mod3, mod_row_of_tile, norm_g, wz, wxc, wbt, wdt_t, cw_xc, cb_xc, cwb, dtb_col,
                 alog_col, *, tm, period, want_z):
    rows, d = x2d.shape
    dz = wz.shape[1]
    dxc = wxc.shape[1]
    nb = wbt.shape[0]
    ndt = wdt_t.shape[0]
    kern = functools.partial(_ssd_in_kernel, period=period, ncol=512, nrow=256, want_z=want_z)
    out_specs = [
        pl.BlockSpec((tm, dxc), lambda i: (i, 0)),
        pl.BlockSpec((nb, tm), lambda i: (0, i)),
        pl.BlockSpec((ndt, tm), lambda i: (0, i)),
    ]
    out_shape = [
        jax.ShapeDtypeStruct((rows, dxc), BF16),
        jax.ShapeDtypeStruct((nb, rows), BF16),
        jax.ShapeDtypeStruct((ndt, rows), F32),
    ]
    if want_z:
        out_specs.insert(0, pl.BlockSpec((tm, dz), lambda i: (i, 0)))
        out_shape.insert(0, jax.ShapeDtypeStruct((rows, dz), BF16))
    return pl.pallas_call(
        kern,
        grid=(rows // tm,),
        in_specs=[
            pl.BlockSpec((tm, d), lambda i: (i, 0)),
            pl.BlockSpec((None, 1, d), lambda i: (mod_row_of_tile(i), 0, 0)),
            pl.BlockSpec((None, 1, d), lambda i: (mod_row_of_tile(i), 0, 1)),
            _const_spec((1, d)),
            _const_spec((d, dz)),
            _const_spec((d, dxc)),
            _const_spec((nb, d)),
            _const_spec((ndt, d)),
            _const_spec((3, dxc)),
            _const_spec((1, dxc)),
            _const_spec(cwb.shape),
            _const_spec((ndt, 1)),
            _const_spec((ndt, 1)),
        ],
        out_specs=out_specs,
        out_shape=out_shape,
        compiler_params=_cparams(1),
        name="ssd_in_proj",
    )(x2d, mod3, mod3, norm_g, wz, wxc, wbt, wdt_t, cw_xc, cb_xc, cwb, dtb_col, alog_col)


COL_CS, COL_ECS, COL_W = 0, 8, 16


def _head_expand(colmat, lane0):
    r = colmat.shape[0]
    first = lax.broadcasted_iota(jnp.int32, (r, SSM_STATE), 1) < SSM_HEAD_DIM
    cols = [colmat[:, lane0 + hd:lane0 + hd + 1] for hd in range(HEADS_PER_GROUP)]
    lo = jnp.where(first, cols[0], cols[1])
    hi = jnp.where(first, cols[2], cols[3])
    return jnp.concatenate([lo, hi], axis=1)


def _expand_select(lane0):
    r = np.arange(2 * SSM_STATE)[:, None] % SSM_STATE
    l = np.arange(2 * GROUP_WIDTH)[None, :]
    src = lane0 + (l // GROUP_WIDTH) * HEADS_PER_GROUP + (l % GROUP_WIDTH) // SSM_HEAD_DIM
    return jnp.asarray(r == src, dtype=BF16)


def _broadcast_select(lane0, count):
    r = np.arange(2 * SSM_STATE)[:, None] % SSM_STATE
    l = np.arange(SSM_STATE * count)[None, :]
    return jnp.asarray(r == lane0 + l // SSM_STATE, dtype=BF16)


def _hi_lo(colmat):
    hi = colmat.astype(BF16)
    lo = (colmat - hi.astype(F32)).astype(BF16)
    return jnp.concatenate([hi, lo], axis=1)


def _decay_rows(dt_ref, n_chunks, upper, lower):
    q = SSM_CHUNK
    nh = HEADS_PER_GROUP
    dt = jnp.concatenate([dt_ref[0:2 * nh, c * q:(c + 1) * q] for c in range(n_chunks)], axis=0)
    la = jnp.concatenate([dt_ref[2 * nh:4 * nh, c * q:(c + 1) * q] for c in range(n_chunks)], axis=0)
    csf = jnp.dot(la, upper, preferred_element_type=F32, precision=lax.Precision.HIGHEST)
    csb = jnp.dot(la, lower, preferred_element_type=F32, precision=lax.Precision.HIGHEST)
    rows = dt.shape[0]
    row = lax.broadcasted_iota(jnp.int32, (rows, 1), 0)
    is_b = jnp.bitwise_and(row, nh) != 0
    cs = jnp.where(is_b, csb, csf)
    tot = jnp.where(is_b, csb[:, 0:1], csf[:, q - 1:q])
    other = jnp.where(is_b, pltpu.roll(dt, nh, 0), pltpu.roll(dt, rows - nh, 0))
    return cs, jnp.exp(cs), jnp.exp(tot - cs) * dt, cs - jnp.log(dt), jnp.log(dt + other)


def _ssd_scan_kernel(xl_ref, btl_ref, cl_ref, zl_ref, dtl_ref, xc_ref, btc_ref, dtc_ref,
                     dsk_ref, ng_ref, selw_ref, sele_ref, selc_ref, o_ref,
                     cols_ref, rows_ref, xw_ref, ee_ref, csb_ref, cb_ref, sloc_ref, hin_ref, etot_ref):
    q = SSM_CHUNK
    nh = HEADS_PER_GROUP
    gw = GROUP_WIDTH
    n_lat = xl_ref.shape[0] // q
    n_ctx = xc_ref.shape[0] // q
    ctx_rows = n_ctx * q
    ki = lax.broadcasted_iota(jnp.int32, (q, q), 0)
    kj = lax.broadcasted_iota(jnp.int32, (q, q), 1)

    upper = (ki <= kj).astype(F32)
    lower = (ki >= kj).astype(F32)
    pad = jnp.zeros((q - 6 * nh, q), F32)
    for dref, n_chunks, base in ((dtc_ref, n_ctx, 0), (dtl_ref, n_lat, n_ctx)):
        cs, ecs, w, rk, lds = _decay_rows(dref, n_chunks, upper, lower)
        for c in range(n_chunks):
            r = slice(2 * nh * c, 2 * nh * (c + 1))
            colmat = jnp.concatenate([cs[r], ecs[r], w[r], pad], axis=0).T
            cols_ref[(base + c) * q:(base + c + 1) * q, :] = colmat
            for d in range(2):
                p0 = 0 if d else q - 1
                etot_ref[d, base + c] = _head_expand(colmat[p0:p0 + 1, :], COL_ECS + nh * d)
        if base:
            rows_ref[0:2 * nh * n_lat, :] = rk
            rows_ref[2 * nh * n_lat:4 * nh * n_lat, :] = lds

    def expand(hl, sel_ref):
        return jnp.dot(hl, sel_ref[...], preferred_element_type=F32)

    xf = xc_ref[...].astype(F32)
    hl = _hi_lo(cols_ref[0:ctx_rows, :])
    xw_ref[0:ctx_rows, :] = (jnp.concatenate([xf, xf], axis=1) * expand(hl, selw_ref)).astype(BF16)
    blk = 4 * q
    for j in range(n_lat * q // blk):
        rows = slice(j * blk, (j + 1) * blk)
        xf = xl_ref[rows, :].astype(F32)
        r0 = ctx_rows + j * blk
        hl = _hi_lo(cols_ref[r0:r0 + blk, :])
        xw_ref[r0:r0 + blk, :] = (jnp.concatenate([xf, xf], axis=1) * expand(hl, selw_ref)).astype(BF16)
        ee_ref[rows, :] = expand(hl, sele_ref)
        csb_ref[rows, :] = expand(hl, selc_ref)

    for c in range(n_ctx):
        sloc_ref[c] = jnp.dot(btc_ref[:, c * q:(c + 1) * q], xw_ref[c * q:(c + 1) * q, :],
                              preferred_element_type=F32)

    def s_body(c, carry):
        r0 = pl.multiple_of(c * q, q)
        bt = btl_ref[:, pl.ds(r0, q)]
        sloc_ref[n_ctx + c] = jnp.dot(bt, xw_ref[pl.ds(ctx_rows + r0, q), :], preferred_element_type=F32)
        cb_ref[c] = jnp.dot(cl_ref[pl.ds(r0, q), :], bt, preferred_element_type=F32)
        return carry

    lax.fori_loop(0, n_lat, s_body, 0, unroll=4)

    for d in range(2):
        lanes = slice(gw * d, gw * (d + 1))
        h = jnp.zeros((SSM_STATE, gw), F32)
        for c in (range(n_ctx - 1, -1, -1) if d else range(n_ctx)):
            h = h * etot_ref[d, c] + sloc_ref[c, :, lanes]

        def b_body(i, h, d=d, lanes=lanes):
            c = (n_lat - 1 - i) if d else i
            hin_ref[c, :, lanes] = h.astype(BF16)
            return h * etot_ref[d, n_ctx + c] + sloc_ref[n_ctx + c, :, lanes]

        lax.fori_loop(0, n_lat, b_body, h, unroll=4)

    lane_head = lax.broadcasted_iota(jnp.int32, (q, gw), 1) // SSM_HEAD_DIM
    below = ki > kj
    above = ki < kj

    def c_body(c, carry):
        r0 = pl.multiple_of(c * q, q)
        x = xl_ref[pl.ds(r0, q), :]
        r8 = pl.multiple_of(c * 2 * nh, 2 * nh)
        rk8 = rows_ref[pl.ds(r8, 2 * nh), :]
        lds8 = rows_ref[pl.ds(2 * nh * n_lat + r8, 2 * nh), :]
        cb = cb_ref[c]
        m_parts = []
        x_parts = []
        for hd in range(nh):
            segf = csb_ref[pl.ds(r0, q), q * hd:q * (hd + 1)] - rk8[hd:hd + 1, :]
            segb = csb_ref[pl.ds(r0, q), q * (nh + hd):q * (nh + hd + 1)] - rk8[nh + hd:nh + hd + 1, :]
            arg = jnp.where(below, segf, jnp.where(above, segb, lds8[hd:hd + 1, :]))
            m_parts.append((jnp.exp(arg) * cb).astype(BF16))
            x_parts.append(jnp.where(lane_head == hd, x, jnp.zeros_like(x)))
        m_all = jnp.concatenate(m_parts, axis=1)
        x_bd = jnp.concatenate(x_parts, axis=0)
        y = jnp.dot(m_all, x_bd, preferred_element_type=F32)
        y_off = ee_ref[pl.ds(r0, q), :] * jnp.dot(cl_ref[pl.ds(r0, q), :], hin_ref[c],
                                                  preferred_element_type=F32)
        y = y + y_off[:, 0:gw] + y_off[:, gw:2 * gw]
        y = y + dsk_ref[...] * x.astype(F32)
        y = y * _silu(zl_ref[pl.ds(r0, q), :].astype(F32))
        y = y * lax.rsqrt(jnp.mean(y * y, axis=-1, keepdims=True) + EPS)
        o_ref[pl.ds(r0, q), :] = (y * ng_ref[...]).astype(BF16)
        return carry

    lax.fori_loop(0, n_lat, c_body, 0, unroll=2)


def _ssd_scan(z_l, xc_l, bt_l, dt_l, xc_c, bt_c, dt_c, dskip_row, ng_row, *, batch, seq, ctx_len):
    g = SSM_GROUPS
    gw = GROUP_WIDTH
    n = SSM_STATE
    c_off = (g * gw) // n
    n_lat = seq // SSM_CHUNK
    n_all = n_lat + ctx_len // SSM_CHUNK
    sel_w = _expand_select(COL_W)
    sel_e = _expand_select(COL_ECS)
    sel_c = _broadcast_select(COL_CS, 2 * HEADS_PER_GROUP)
    return pl.pallas_call(
        _ssd_scan_kernel,
        grid=(batch, g),
        in_specs=[
            pl.BlockSpec((seq, gw), lambda b, k: (b, k)),
            pl.BlockSpec((n, seq), lambda b, k: (k, b)),
            pl.BlockSpec((seq, n), lambda b, k: (b, c_off + k)),
            pl.BlockSpec((seq, gw), lambda b, k: (b, k)),
            pl.BlockSpec((DT_ROWS_PER_GROUP, seq), lambda b, k: (k, b)),
            pl.BlockSpec((ctx_len, gw), lambda b, k: (b, k)),
            pl.BlockSpec((n, ctx_len), lambda b, k: (k, b)),
            pl.BlockSpec((DT_ROWS_PER_GROUP, ctx_len), lambda b, k: (k, b)),
            pl.BlockSpec((1, gw), lambda b, k: (0, k)),
            pl.BlockSpec((1, gw), lambda b, k: (0, k)),
            _const_spec(sel_w.shape),
            _const_spec(sel_e.shape),
            _const_spec(sel_c.shape),
        ],
        out_specs=pl.BlockSpec((seq, gw), lambda b, k: (b, k)),
        out_shape=jax.ShapeDtypeStruct((batch * seq, g * gw), BF16),
        scratch_shapes=[
            pltpu.VMEM((n_all * SSM_CHUNK, SSM_CHUNK), F32),
            pltpu.VMEM((4 * HEADS_PER_GROUP * n_lat, SSM_CHUNK), F32),
            pltpu.VMEM((n_all * SSM_CHUNK, 2 * gw), BF16),
            pltpu.VMEM((seq, 2 * gw), F32),
            pltpu.VMEM((seq, 2 * HEADS_PER_GROUP * SSM_CHUNK), F32),
            pltpu.VMEM((n_lat, SSM_CHUNK, SSM_CHUNK), F32),
            pltpu.VMEM((n_all, n, 2 * gw), F32),
            pltpu.VMEM((n_lat, n, 2 * gw), BF16),
            pltpu.VMEM((2, n_all, 1, gw), F32),
        ],
        compiler_params=_cparams(2),
        name="ssd_scan",
    )(xc_l, bt_l, xc_l, z_l, dt_l, xc_c, bt_c, dt_c, dskip_row, ng_row, sel_w, sel_e, sel_c)


def _mlp_tail(x1, g_ref, sh_ref, sc_ref, gate_ref, w1_ref, w2_ref, nff):
    h2 = _norm_mod(x1, g_ref[...], sh_ref[...], sc_ref[...]).astype(BF16)
    dff = w1_ref.shape[1]
    acc = None
    for j in range(dff // nff):
        c0 = j * nff
        a = jnp.dot(h2, w1_ref[:, c0:c0 + nff], preferred_element_type=F32)
        a = jnp.square(jnp.maximum(a, 0.0)).astype(BF16)
        p = jnp.dot(a, w2_ref[c0:c0 + nff, :], preferred_element_type=F32)
        acc = p if acc is None else acc + p
    return x1 + gate_ref[...] * acc


def _ssd_out_kernel(y_ref, x_ref, gm_ref, shf_ref, scf_ref, gf_ref, ng_ref,
                    wo_ref, w1_ref, w2_ref, o_ref):
    y = jnp.dot(y_ref[...], wo_ref[...], preferred_element_type=F32)
    x1 = x_ref[...] + gm_ref[...] * y
    o_ref[...] = _mlp_tail(x1, ng_ref, shf_ref, scf_ref, gf_ref, w1_ref, w2_ref, 1024)


def _mod_spec(d, row_of_tile, k):
    return pl.BlockSpec((None, 1, d), lambda i: (row_of_tile(i), 0, k))


def _ssd_out_mlp(y2d, x2d, mod3, row_of_tile, norm_g, wo, w1, w2, *, tm):
    rows, d = x2d.shape
    di = y2d.shape[1]
    dff = w1.shape[1]
    return pl.pallas_call(
        _ssd_out_kernel,
        grid=(rows // tm,),
        in_specs=[
            pl.BlockSpec((tm, di), lambda i: (i, 0)),
            pl.BlockSpec((tm, d), lambda i: (i, 0)),
            _mod_spec(d, row_of_tile, 2),
            _mod_spec(d, row_of_tile, 3),
            _mod_spec(d, row_of_tile, 4),
            _mod_spec(d, row_of_tile, 5),
            _const_spec((1, d)),
            _const_spec((di, d)),
            _const_spec((d, dff)),
            _const_spec((dff, d)),
        ],
        out_specs=pl.BlockSpec((tm, d), lambda i: (i, 0)),
        out_shape=jax.ShapeDtypeStruct((rows, d), F32),
        compiler_params=_cparams(1),
        name="ssd_out_mlp",
    )(y2d, x2d, mod3, mod3, mod3, mod3, norm_g, wo, w1, w2)


def _sc_layer_kernel(x_ref, shm_ref, scm_ref, gm_ref, shf_ref, scf_ref, gf_ref, ngm_ref, ngf_ref,
                     fg_ref, wi_ref, cw_ref, wo_ref, w1_ref, w2_ref, o_ref, *, period):
    x = x_ref[...]
    h = _norm_mod(x, ngm_ref[...], shm_ref[...], scm_ref[...]).astype(BF16)
    w = wo_ref.shape[0]
    bg = jnp.dot(h, wi_ref[:, 0:w], preferred_element_type=F32)
    cg = jnp.dot(h, wi_ref[:, w:2 * w], preferred_element_type=F32)
    xv = jnp.dot(h, wi_ref[:, 2 * w:3 * w], preferred_element_type=F32)
    u = (bg * _row_conv3(cg * xv, cw_ref, 0, w, period)).astype(BF16)
    y = jnp.dot(u, wo_ref[...], preferred_element_type=F32)
    x1 = x + gm_ref[...] * y
    x2 = _mlp_tail(x1, ngf_ref, shf_ref, scf_ref, gf_ref, w1_ref, w2_ref, 1024)
    ms = jnp.mean(x2 * x2, axis=-1, keepdims=True)
    o_ref[...] = x2 * lax.rsqrt(ms + EPS) * fg_ref[...]


def _sc_layer(x2d, mod3, row_of_tile, ng_mix, ng_mlp, final_g, wi, conv_w, wo, w1, w2, *, tm, period):
    rows, d = x2d.shape
    dff = w1.shape[1]
    kern = functools.partial(_sc_layer_kernel, period=period)
    return pl.pallas_call(
        kern,
        grid=(rows // tm,),
        in_specs=[
            pl.BlockSpec((tm, d), lambda i: (i, 0)),
            _mod_spec(d, row_of_tile, 0),
            _mod_spec(d, row_of_tile, 1),
            _mod_spec(d, row_of_tile, 2),
            _mod_spec(d, row_of_tile, 3),
            _mod_spec(d, row_of_tile, 4),
            _mod_spec(d, row_of_tile, 5),
            _const_spec((1, d)),
            _const_spec((1, d)),
            _const_spec((1, d)),
            _const_spec(wi.shape),
            _const_spec(conv_w.shape),
            _const_spec(wo.shape),
            _const_spec((d, dff)),
            _const_spec((dff, d)),
        ],
        out_specs=pl.BlockSpec((tm, d), lambda i: (i, 0)),
        out_shape=jax.ShapeDtypeStruct((rows, d), F32),
        compiler_params=_cparams(1),
        name="shortconv_layer",
    )(x2d, mod3, mod3, mod3, mod3, mod3, mod3, ng_mix, ng_mlp, final_g, wi, conv_w, wo, w1, w2)


def kernel(x, c, ctx, c_ctx, ada_w, ada_b, norm_mix_g, norm_mlp_g, ssd_w_in, ssd_conv_w, ssd_conv_b,
           ssd_dt_bias, ssd_a_log, ssd_d, ssd_norm_g, ssd_w_out, sc_w_in, sc_conv_w, sc_w_out,
           mlp_w1, mlp_w2, final_norm_g):
    batch, seq, d = x.shape
    ctx_len = ctx.shape[1]
    depth = ada_w.shape[0]
    assert depth == 2 and ssd_w_in.shape[0] == 1 and sc_w_in.shape[0] == 1
    d_inner = ssd_w_out.shape[1]
    n_heads = ssd_d.shape[1]
    xbc_dim = ssd_conv_w.shape[2]
    assert n_heads == SSM_GROUPS * HEADS_PER_GROUP and d_inner == SSM_GROUPS * GROUP_WIDTH

    mod_rows = 16
    cvec = jnp.zeros((mod_rows, d), F32).at[:batch].set(c).at[batch].set(c_ctx)
    mod = _modulation(cvec, ada_w, ada_b)
    mod0 = mod[0].reshape(mod_rows, 1, 6 * d)
    mod1 = mod[1].reshape(mod_rows, 1, 6 * d)

    w_in = ssd_w_in[0]
    wz = w_in[:, :d_inner].astype(BF16)
    nbc = SSM_GROUPS * SSM_STATE
    xs = slice(d_inner, 2 * d_inner)
    bs = slice(2 * d_inner, 2 * d_inner + nbc)
    cs_ = slice(2 * d_inner + nbc, 2 * d_inner + 2 * nbc)
    wxc = jnp.concatenate([w_in[:, xs], w_in[:, cs_]], axis=1).astype(BF16)
    wbt = w_in[:, bs].T.astype(BF16)
    cw = ssd_conv_w[0]
    cbias = ssd_conv_b[0]
    cw_xc = jnp.concatenate([cw[:, :d_inner], cw[:, d_inner + nbc:]], axis=1)
    cb_xc = jnp.concatenate([cbias[:d_inner], cbias[d_inner + nbc:]]).reshape(1, -1)
    cwb = jnp.concatenate([cw[:, d_inner:d_inner + nbc].T, cbias[d_inner:d_inner + nbc, None],
                           jnp.zeros((nbc, 4), F32)], axis=1)
    gi = jnp.arange(SSM_GROUPS)[:, None, None]
    di_ = jnp.arange(2)[None, :, None]
    ri = jnp.arange(HEADS_PER_GROUP)[None, None, :]
    flat = (di_ * n_heads + gi * HEADS_PER_GROUP + ri).reshape(SSM_GROUPS, 2 * HEADS_PER_GROUP)
    dt_idx = jnp.concatenate([flat, flat], axis=1).reshape(-1)
    wdt_t = w_in[:, d_inner + xbc_dim:][:, dt_idx].T.astype(BF16)
    dtb_col = ssd_dt_bias[0].reshape(-1)[dt_idx].reshape(-1, 1).astype(F32)
    alog_col = ssd_a_log[0].reshape(-1)[dt_idx].reshape(-1, 1).astype(F32)
    ng_mix0 = norm_mix_g[0].reshape(1, d)

    tm = 512
    x2d = x.reshape(batch * seq, d)
    ctx2d = ctx.reshape(batch * ctx_len, d)
    lat_row = lambda i: (i * tm) // seq
    z_l, xc_l, bt_l, dt_l = _ssd_in_proj(x2d, mod0, lat_row, ng_mix0, wz, wxc, wbt, wdt_t, cw_xc, cb_xc,
                                         cwb, dtb_col, alog_col, tm=tm, period=GRID_W, want_z=True)
    xc_c, bt_c, dt_c = _ssd_in_proj(ctx2d, mod0, lambda i: batch, ng_mix0, wz, wxc, wbt, wdt_t, cw_xc,
                                    cb_xc, cwb, dtb_col, alog_col, tm=ctx_len, period=ctx_len,
                                    want_z=False)

    dskip_row = jnp.repeat(ssd_d[0].astype(F32), SSM_HEAD_DIM).reshape(1, d_inner)
    ng_row = ssd_norm_g[0].reshape(1, d_inner)
    y_fin = _ssd_scan(z_l, xc_l, bt_l, dt_l, xc_c, bt_c, dt_c, dskip_row, ng_row,
                      batch=batch, seq=seq, ctx_len=ctx_len)

    x1 = _ssd_out_mlp(y_fin, x2d, mod0, lat_row, norm_mlp_g[0].reshape(1, d),
                      ssd_w_out[0].astype(BF16), mlp_w1[0].astype(BF16), mlp_w2[0].astype(BF16), tm=tm)

    out = _sc_layer(x1, mod1, lat_row, norm_mix_g[1].reshape(1, d), norm_mlp_g[1].reshape(1, d),
                    final_norm_g.reshape(1, d), sc_w_in[0].astype(BF16), sc_conv_w[0],
                    sc_w_out[0].astype(BF16), mlp_w1[1].astype(BF16), mlp_w2[1].astype(BF16),
                    tm=tm, period=GRID_W)
    return out.reshape(batch, seq, d)
```

```python
import functools

import numpy as np

import jax
import jax.numpy as jnp
from jax import lax
from jax.experimental import pallas as pl
from jax.experimental.pallas import tpu as pltpu

F32 = jnp.float32
BF16 = jnp.bfloat16

EPS = 1e-6
GRID_W = 64
SSM_HEAD_DIM = 64
SSM_GROUPS = 8
HEADS_PER_GROUP = 4
SSM_STATE = 128
SSM_CHUNK = 128
GROUP_WIDTH = HEADS_PER_GROUP * SSM_HEAD_DIM
DT_ROWS_PER_GROUP = 16
NEG_BIG = -1e30

VMEM_LIMIT_BYTES = 56 * 1024 * 1024


def _cparams(n_axes):
    return pltpu.CompilerParams(
        dimension_semantics=("arbitrary",) * n_axes,
        vmem_limit_bytes=VMEM_LIMIT_BYTES,
    )


def _const_spec(shape):
    nd = len(shape)
    return pl.BlockSpec(shape, lambda *_: (0,) * nd, pipeline_mode=pl.Buffered(1))


def _silu(u):
    return u * (1.0 / (1.0 + jnp.exp(-u)))


def _norm_mod(x, g, shift, scale):
    ms = jnp.mean(x * x, axis=-1, keepdims=True)
    y = x * lax.rsqrt(ms + EPS) * g
    return y * (1.0 + scale) + shift


def _row_conv3(u, w_ref, col0, ncols, period):
    w0 = w_ref[0:1, col0:col0 + ncols]
    w1 = w_ref[1:2, col0:col0 + ncols]
    w2 = w_ref[2:3, col0:col0 + ncols]
    rid = lax.broadcasted_iota(jnp.int32, (period, 1), 0)
    for r0 in range(0, u.shape[0], period):
        blk = u[r0:r0 + period, :]
        prev = jnp.where(rid != 0, pltpu.roll(blk, 1, 0), 0.0)
        nxt = jnp.where(rid != period - 1, pltpu.roll(blk, period - 1, 0), 0.0)
        yield r0, prev * w0 + blk * w1 + nxt * w2


def _mod_kernel(c_ref, w_ref, b_ref, o_ref):
    s = _silu(c_ref[...])
    o_ref[...] = jnp.dot(s, w_ref[...], preferred_element_type=F32,
                         precision=lax.Precision.HIGHEST) + b_ref[...]


def _modulation(cvec, ada_w, ada_b):
    depth, d, n = ada_w.shape
    rows = cvec.shape[0]
    tn = 1536
    return pl.pallas_call(
        _mod_kernel,
        grid=(depth, n // tn),
        in_specs=[
            pl.BlockSpec((rows, d), lambda i, j: (0, 0)),
            pl.BlockSpec((None, d, tn), lambda i, j: (i, 0, j)),
            pl.BlockSpec((None, 1, tn), lambda i, j: (i, 0, j)),
        ],
        out_specs=pl.BlockSpec((None, rows, tn), lambda i, j: (i, 0, j)),
        out_shape=jax.ShapeDtypeStruct((depth, rows, n), F32),
        compiler_params=_cparams(2),
        name="adaln_mod",
    )(cvec, ada_w, ada_b.reshape(depth, 1, n))


_NT_DIMS = (((1,), (1,)), ((), ()))


def _ssd_in_kernel(x_ref, sh_ref, sc_ref, g_ref, wz_ref, wxc_ref, wbt_ref, wdt_ref, cw_ref, cb_ref,
                   cwb_ref, dtb_ref, alog_ref, *out_refs, period, ncol, nrow, want_z):
    if want_z:
        z_ref, xc_ref, bt_ref, dt_ref = out_refs
    else:
        xc_ref, bt_ref, dt_ref = out_refs
    h = _norm_mod(x_ref[...], g_ref[...], sh_ref[...], sc_ref[...]).astype(BF16)
    if want_z:
        z_ref[...] = jnp.dot(h, wz_ref[...], preferred_element_type=F32).astype(BF16)
    for j in range(xc_ref.shape[1] // ncol):
        c0 = j * ncol
        u = jnp.dot(h, wxc_ref[:, c0:c0 + ncol], preferred_element_type=F32)
        bias = cb_ref[:, c0:c0 + ncol]
        for r0, v in _row_conv3(u, cw_ref, c0, ncol, period):
            xc_ref[r0:r0 + period, c0:c0 + ncol] = _silu(v + bias).astype(BF16)
    tm = x_ref.shape[0]
    pos = jnp.bitwise_and(lax.broadcasted_iota(jnp.int32, (1, tm), 1), period - 1)
    for j in range(bt_ref.shape[0] // nrow):
        r0 = j * nrow
        u = lax.dot_general(wbt_ref[r0:r0 + nrow, :], h, _NT_DIMS, preferred_element_type=F32)
        prev = jnp.where(pos != 0, pltpu.roll(u, 1, 1), 0.0)
        nxt = jnp.where(pos != period - 1, pltpu.roll(u, tm - 1, 1), 0.0)
        cwb = cwb_ref[r0:r0 + nrow, :]
        u = prev * cwb[:, 0:1] + u * cwb[:, 1:2] + nxt * cwb[:, 2:3] + cwb[:, 3:4]
        bt_ref[r0:r0 + nrow, :] = _silu(u).astype(BF16)
    raw = lax.dot_general(wdt_ref[...], h, _NT_DIMS, preferred_element_type=F32)
    v = raw + dtb_ref[...]
    sp = jnp.maximum(v, 0.0) + jnp.log1p(jnp.exp(-jnp.abs(v)))
    row = lax.broadcasted_iota(jnp.int32, (dt_ref.shape[0], 1), 0)
    is_la = jnp.bitwise_and(row, DT_ROWS_PER_GROUP - 1) >= DT_ROWS_PER_GROUP // 2
    dt_ref[...] = jnp.where(is_la, sp * (-jnp.exp(alog_ref[...])), sp)


def _ssd_in_proj(x2d, mod3, mod_row_of_tile, norm_g, wz, wxc, wbt, wdt_t, cw_xc, cb_xc, cwb, dtb_col,
                 alog_col, *, tm, period, want_z):
    rows, d = x2d.shape
    dz = wz.shape[1]
    dxc = wxc.shape[1]
    nb = wbt.shape[0]
    ndt = wdt_t.shape[0]
    kern = functools.partial(_ssd_in_kernel, period=period, ncol=512, nrow=256, want_z=want_z)
    out_specs = [
        pl.BlockSpec((tm, dxc), lambda i: (i, 0)),
        pl.BlockSpec((nb, tm), lambda i: (0, i)),
        pl.BlockSpec((ndt, tm), lambda i: (0, i)),
    ]
    out_shape = [
        jax.ShapeDtypeStruct((rows, dxc), BF16),
        jax.ShapeDtypeStruct((nb, rows), BF16),
        jax.ShapeDtypeStruct((ndt, rows), F32),
    ]
    if want_z:
        out_specs.insert(0, pl.BlockSpec((tm, dz), lambda i: (i, 0)))
        out_shape.insert(0, jax.ShapeDtypeStruct((rows, dz), BF16))
    return pl.pallas_call(
        kern,
        grid=(rows // tm,),
        in_specs=[
            pl.BlockSpec((tm, d), lambda i: (i, 0)),
            pl.BlockSpec((None, 1, d), lambda i: (mod_row_of_tile(i), 0, 0)),
            pl.BlockSpec((None, 1, d), lambda i: (mod_row_of_tile(i), 0, 1)),
            _const_spec((1, d)),
            _const_spec((d, dz)),
            _const_spec((d, dxc)),
            _const_spec((nb, d)),
            _const_spec((ndt, d)),
            _const_spec((3, dxc)),
            _const_spec((1, dxc)),
            _const_spec(cwb.shape),
            _const_spec((ndt, 1)),
            _const_spec((ndt, 1)),
        ],
        out_specs=out_specs,
        out_shape=out_shape,
        compiler_params=_cparams(1),
        name="ssd_in_proj",
    )(x2d, mod3, mod3, norm_g, wz, wxc, wbt, wdt_t, cw_xc, cb_xc, cwb, dtb_col, alog_col)


COL_CS, COL_ECS, COL_W = 0, 8, 16


def _head_expand(colmat, lane0):
    r = colmat.shape[0]
    first = lax.broadcasted_iota(jnp.int32, (r, SSM_STATE), 1) < SSM_HEAD_DIM
    cols = [colmat[:, lane0 + hd:lane0 + hd + 1] for hd in range(HEADS_PER_GROUP)]
    lo = jnp.where(first, cols[0], cols[1])
    hi = jnp.where(first, cols[2], cols[3])
    return jnp.concatenate([lo, hi], axis=1)


def _expand_select(lane0):
    r = np.arange(2 * SSM_STATE)[:, None] % SSM_STATE
    l = np.arange(2 * GROUP_WIDTH)[None, :]
    src = lane0 + (l // GROUP_WIDTH) * HEADS_PER_GROUP + (l % GROUP_WIDTH) // SSM_HEAD_DIM
    return jnp.asarray(r == src, dtype=BF16)


def _broadcast_select(lane0, count):
    r = np.arange(2 * SSM_STATE)[:, None] % SSM_STATE
    l = np.arange(SSM_STATE * count)[None, :]
    return jnp.asarray(r == lane0 + l // SSM_STATE, dtype=BF16)


def _hi_lo(colmat):
    hi = colmat.astype(BF16)
    lo = (colmat - hi.astype(F32)).astype(BF16)
    return jnp.concatenate([hi, lo], axis=1)


def _decay_rows(dt_ref, n_chunks, upper, lower):
    q = SSM_CHUNK
    nh = HEADS_PER_GROUP
    dt = jnp.concatenate([dt_ref[0:2 * nh, c * q:(c + 1) * q] for c in range(n_chunks)], axis=0)
    la = jnp.concatenate([dt_ref[2 * nh:4 * nh, c * q:(c + 1) * q] for c in range(n_chunks)], axis=0)
    csf = jnp.dot(la, upper, preferred_element_type=F32, precision=lax.Precision.HIGHEST)
    csb = jnp.dot(la, lower, preferred_element_type=F32, precision=lax.Precision.HIGHEST)
    rows = dt.shape[0]
    row = lax.broadcasted_iota(jnp.int32, (rows, 1), 0)
    is_b = jnp.bitwise_and(row, nh) != 0
    cs = jnp.where(is_b, csb, csf)
    tot = jnp.where(is_b, csb[:, 0:1], csf[:, q - 1:q])
    other = jnp.where(is_b, pltpu.roll(dt, nh, 0), pltpu.roll(dt, rows - nh, 0))
    return cs, jnp.exp(cs), jnp.exp(tot - cs) * dt, cs - jnp.log(dt), jnp.log(dt + other)


def _ssd_scan_kernel(xl_ref, btl_ref, cl_ref, dtl_ref, xc_ref, btc_ref, dtc_ref,
                     selw_ref, sele_ref, selc_ref, o_ref,
                     cols_ref, rows_ref, xw_ref, ee_ref, csb_ref, cb_ref, sloc_ref, hin_ref, etot_ref):
    q = SSM_CHUNK
    nh = HEADS_PER_GROUP
    gw = GROUP_WIDTH
    n_lat = xl_ref.shape[0] // q
    n_ctx = xc_ref.shape[0] // q
    ctx_rows = n_ctx * q
    ki = lax.broadcasted_iota(jnp.int32, (q, q), 0)
    kj = lax.broadcasted_iota(jnp.int32, (q, q), 1)

    upper = (ki <= kj).astype(F32)
    lower = (ki >= kj).astype(F32)
    pad = jnp.zeros((q - 6 * nh, q), F32)
    for dref, n_chunks, base in ((dtc_ref, n_ctx, 0), (dtl_ref, n_lat, n_ctx)):
        cs, ecs, w, rk, lds = _decay_rows(dref, n_chunks, upper, lower)
        for c in range(n_chunks):
            r = slice(2 * nh * c, 2 * nh * (c + 1))
            colmat = jnp.concatenate([cs[r], ecs[r], w[r], pad], axis=0).T
            cols_ref[(base + c) * q:(base + c + 1) * q, :] = colmat
            for d in range(2):
                p0 = 0 if d else q - 1
                etot_ref[d, base + c] = _head_expand(colmat[p0:p0 + 1, :], COL_ECS + nh * d)
        if base:
            rows_ref[0:2 * nh * n_lat, :] = rk
            rows_ref[2 * nh * n_lat:4 * nh * n_lat, :] = lds

    def expand(hl, sel_ref):
        return jnp.dot(hl, sel_ref[...], preferred_element_type=F32)

    xf = xc_ref[...].astype(F32)
    hl = _hi_lo(cols_ref[0:ctx_rows, :])
    xw_ref[0:ctx_rows, :] = (jnp.concatenate([xf, xf], axis=1) * expand(hl, selw_ref)).astype(BF16)
    blk = 4 * q
    for j in range(n_lat * q // blk):
        rows = slice(j * blk, (j + 1) * blk)
        xf = xl_ref[rows, :].astype(F32)
        r0 = ctx_rows + j * blk
        hl = _hi_lo(cols_ref[r0:r0 + blk, :])
        xw_ref[r0:r0 + blk, :] = (jnp.concatenate([xf, xf], axis=1) * expand(hl, selw_ref)).astype(BF16)
        ee_ref[rows, :] = expand(hl, sele_ref)
        csb_ref[rows, :] = expand(hl, selc_ref)

    for c in range(n_ctx):
        sloc_ref[c] = jnp.dot(btc_ref[:, c * q:(c + 1) * q], xw_ref[c * q:(c + 1) * q, :],
                              preferred_element_type=F32)

    def s_body(c, carry):
        r0 = pl.multiple_of(c * q, q)
        bt = btl_ref[:, pl.ds(r0, q)]
        sloc_ref[n_ctx + c] = jnp.dot(bt, xw_ref[pl.ds(ctx_rows + r0, q), :], preferred_element_type=F32)
        cb_ref[c] = jnp.dot(cl_ref[pl.ds(r0, q), :], bt, preferred_element_type=F32)
        return carry

    lax.fori_loop(0, n_lat, s_body, 0, unroll=4)

    for d in range(2):
        lanes = slice(gw * d, gw * (d + 1))
        h = jnp.zeros((SSM_STATE, gw), F32)
        for c in (range(n_ctx - 1, -1, -1) if d else range(n_ctx)):
            h = h * etot_ref[d, c] + sloc_ref[c, :, lanes]

        def b_body(i, h, d=d, lanes=lanes):
            c = (n_lat - 1 - i) if d else i
            hin_ref[c, :, lanes] = h.astype(BF16)
            return h * etot_ref[d, n_ctx + c] + sloc_ref[n_ctx + c, :, lanes]

        lax.fori_loop(0, n_lat, b_body, h, unroll=4)

    lane_head = lax.broadcasted_iota(jnp.int32, (q, gw), 1) // SSM_HEAD_DIM
    below = ki > kj
    above = ki < kj

    def c_body(c, carry):
        r0 = pl.multiple_of(c * q, q)
        x = xl_ref[pl.ds(r0, q), :]
        r8 = pl.multiple_of(c * 2 * nh, 2 * nh)
        rk8 = rows_ref[pl.ds(r8, 2 * nh), :]
        lds8 = rows_ref[pl.ds(2 * nh * n_lat + r8, 2 * nh), :]
        cb = cb_ref[c]
        m_parts = []
        x_parts = []
        for hd in range(nh):
            segf = csb_ref[pl.ds(r0, q), q * hd:q * (hd + 1)] - rk8[hd:hd + 1, :]
            segb = csb_ref[pl.ds(r0, q), q * (nh + hd):q * (nh + hd + 1)] - rk8[nh + hd:nh + hd + 1, :]
            arg = jnp.where(below, segf, jnp.where(above, segb, lds8[hd:hd + 1, :]))
            m_parts.append((jnp.exp(arg) * cb).astype(BF16))
            x_parts.append(jnp.where(lane_head == hd, x, jnp.zeros_like(x)))
        m_all = jnp.concatenate(m_parts, axis=1)
        x_bd = jnp.concatenate(x_parts, axis=0)
        y = jnp.dot(m_all, x_bd, preferred_element_type=F32)
        y_off = ee_ref[pl.ds(r0, q), :] * jnp.dot(cl_ref[pl.ds(r0, q), :], hin_ref[c],
                                                  preferred_element_type=F32)
        o_ref[pl.ds(r0, q), :] = (y + y_off[:, 0:gw] + y_off[:, gw:2 * gw]).astype(BF16)
        return carry

    lax.fori_loop(0, n_lat, c_body, 0, unroll=2)


def _ssd_scan(xc_l, bt_l, dt_l, xc_c, bt_c, dt_c, *, batch, seq, ctx_len):
    g = SSM_GROUPS
    gw = GROUP_WIDTH
    n = SSM_STATE
    c_off = (g * gw) // n
    n_lat = seq // SSM_CHUNK
    n_all = n_lat + ctx_len // SSM_CHUNK
    sel_w = _expand_select(COL_W)
    sel_e = _expand_select(COL_ECS)
    sel_c = _broadcast_select(COL_CS, 2 * HEADS_PER_GROUP)
    return pl.pallas_call(
        _ssd_scan_kernel,
        grid=(batch, g),
        in_specs=[
            pl.BlockSpec((seq, gw), lambda b, k: (b, k)),
            pl.BlockSpec((n, seq), lambda b, k: (k, b)),
            pl.BlockSpec((seq, n), lambda b, k: (b, c_off + k)),
            pl.BlockSpec((DT_ROWS_PER_GROUP, seq), lambda b, k: (k, b)),
            pl.BlockSpec((ctx_len, gw), lambda b, k: (b, k)),
            pl.BlockSpec((n, ctx_len), lambda b, k: (k, b)),
            pl.BlockSpec((DT_ROWS_PER_GROUP, ctx_len), lambda b, k: (k, b)),
            _const_spec(sel_w.shape),
            _const_spec(sel_e.shape),
            _const_spec(sel_c.shape),
        ],
        out_specs=pl.BlockSpec((seq, gw), lambda b, k: (b, k)),
        out_shape=jax.ShapeDtypeStruct((batch * seq, g * gw), BF16),
        scratch_shapes=[
            pltpu.VMEM((n_all * SSM_CHUNK, SSM_CHUNK), F32),
            pltpu.VMEM((4 * HEADS_PER_GROUP * n_lat, SSM_CHUNK), F32),
            pltpu.VMEM((n_all * SSM_CHUNK, 2 * gw), BF16),
            pltpu.VMEM((seq, 2 * gw), F32),
            pltpu.VMEM((seq, 2 * HEADS_PER_GROUP * SSM_CHUNK), F32),
            pltpu.VMEM((n_lat, SSM_CHUNK, SSM_CHUNK), F32),
            pltpu.VMEM((n_all, n, 2 * gw), F32),
            pltpu.VMEM((n_lat, n, 2 * gw), BF16),
            pltpu.VMEM((2, n_all, 1, gw), F32),
        ],
        compiler_params=_cparams(2),
        name="ssd_scan",
    )(xc_l, bt_l, xc_l, dt_l, xc_c, bt_c, dt_c, sel_w, sel_e, sel_c)


def _mlp_tail(x1, g_ref, sh_ref, sc_ref, gate_ref, w1_ref, w2_ref, nff):
    h2 = _norm_mod(x1, g_ref[...], sh_ref[...], sc_ref[...]).astype(BF16)
    dff = w1_ref.shape[1]
    acc = None
    for j in range(dff // nff):
        c0 = j * nff
        a = jnp.dot(h2, w1_ref[:, c0:c0 + nff], preferred_element_type=F32)
        a = jnp.square(jnp.maximum(a, 0.0)).astype(BF16)
        p = jnp.dot(a, w2_ref[c0:c0 + nff, :], preferred_element_type=F32)
        acc = p if acc is None else acc + p
    return x1 + gate_ref[...] * acc


def _ssd_out_kernel(y_ref, xs_ref, z_ref, x_ref, dsk_ref, sng_ref, gm_ref, shf_ref, scf_ref, gf_ref,
                    ng_ref, wo_ref, w1_ref, w2_ref, o_ref):
    acc = None
    for g0 in range(0, y_ref.shape[1], GROUP_WIDTH):
        cols = slice(g0, g0 + GROUP_WIDTH)
        y = y_ref[:, cols].astype(F32) + dsk_ref[:, cols] * xs_ref[:, cols].astype(F32)
        y = y * _silu(z_ref[:, cols].astype(F32))
        y = y * lax.rsqrt(jnp.mean(y * y, axis=-1, keepdims=True) + EPS)
        yn = (y * sng_ref[:, cols]).astype(BF16)
        p = jnp.dot(yn, wo_ref[cols, :], preferred_element_type=F32)
        acc = p if acc is None else acc + p
    x1 = x_ref[...] + gm_ref[...] * acc
    o_ref[...] = _mlp_tail(x1, ng_ref, shf_ref, scf_ref, gf_ref, w1_ref, w2_ref, 1024)


def _mod_spec(d, row_of_tile, k):
    return pl.BlockSpec((None, 1, d), lambda i: (row_of_tile(i), 0, k))


def _ssd_out_mlp(y2d, xc2d, z2d, x2d, dskip_row, ssd_ng_row, mod3, row_of_tile, norm_g, wo, w1, w2, *, tm):
    rows, d = x2d.shape
    di = y2d.shape[1]
    dff = w1.shape[1]
    return pl.pallas_call(
        _ssd_out_kernel,
        grid=(rows // tm,),
        in_specs=[
            pl.BlockSpec((tm, di), lambda i: (i, 0)),
            pl.BlockSpec((tm, di), lambda i: (i, 0)),
            pl.BlockSpec((tm, di), lambda i: (i, 0)),
            pl.BlockSpec((tm, d), lambda i: (i, 0)),
            _const_spec((1, di)),
            _const_spec((1, di)),
            _mod_spec(d, row_of_tile, 2),
            _mod_spec(d, row_of_tile, 3),
            _mod_spec(d, row_of_tile, 4),
            _mod_spec(d, row_of_tile, 5),
            _const_spec((1, d)),
            _const_spec((di, d)),
            _const_spec((d, dff)),
            _const_spec((dff, d)),
        ],
        out_specs=pl.BlockSpec((tm, d), lambda i: (i, 0)),
        out_shape=jax.ShapeDtypeStruct((rows, d), F32),
        compiler_params=_cparams(1),
        name="ssd_out_mlp",
    )(y2d, xc2d, z2d, x2d, dskip_row, ssd_ng_row, mod3, mod3, mod3, mod3, norm_g, wo, w1, w2)


def _sc_layer_kernel(x_ref, shm_ref, scm_ref, gm_ref, shf_ref, scf_ref, gf_ref, ngm_ref, ngf_ref,
                     fg_ref, wi_ref, cw_ref, wo_ref, w1_ref, w2_ref, o_ref, *, period):
    x = x_ref[...]
    h = _norm_mod(x, ngm_ref[...], shm_ref[...], scm_ref[...]).astype(BF16)
    w = wo_ref.shape[0]
    bg = jnp.dot(h, wi_ref[:, 0:w], preferred_element_type=F32)
    cg = jnp.dot(h, wi_ref[:, w:2 * w], preferred_element_type=F32)
    xv = jnp.dot(h, wi_ref[:, 2 * w:3 * w], preferred_element_type=F32)
    u = jnp.concatenate(
        [(bg[r0:r0 + period, :] * v).astype(BF16) for r0, v in _row_conv3(cg * xv, cw_ref, 0, w, period)],
        axis=0)
    y = jnp.dot(u, wo_ref[...], preferred_element_type=F32)
    x1 = x + gm_ref[...] * y
    x2 = _mlp_tail(x1, ngf_ref, shf_ref, scf_ref, gf_ref, w1_ref, w2_ref, 1024)
    ms = jnp.mean(x2 * x2, axis=-1, keepdims=True)
    o_ref[...] = x2 * lax.rsqrt(ms + EPS) * fg_ref[...]


def _sc_layer(x2d, mod3, row_of_tile, ng_mix, ng_mlp, final_g, wi, conv_w, wo, w1, w2, *, tm, period):
    rows, d = x2d.shape
    dff = w1.shape[1]
    kern = functools.partial(_sc_layer_kernel, period=period)
    return pl.pallas_call(
        kern,
        grid=(rows // tm,),
        in_specs=[
            pl.BlockSpec((tm, d), lambda i: (i, 0)),
            _mod_spec(d, row_of_tile, 0),
            _mod_spec(d, row_of_tile, 1),
            _mod_spec(d, row_of_tile, 2),
            _mod_spec(d, row_of_tile, 3),
            _mod_spec(d, row_of_tile, 4),
            _mod_spec(d, row_of_tile, 5),
            _const_spec((1, d)),
            _const_spec((1, d)),
            _const_spec((1, d)),
            _const_spec(wi.shape),
            _const_spec(conv_w.shape),
            _const_spec(wo.shape),
            _const_spec((d, dff)),
            _const_spec((dff, d)),
        ],
        out_specs=pl.BlockSpec((tm, d), lambda i: (i, 0)),
        out_shape=jax.ShapeDtypeStruct((rows, d), F32),
        compiler_params=_cparams(1),
        name="shortconv_layer",
    )(x2d, mod3, mod3, mod3, mod3, mod3, mod3, ng_mix, ng_mlp, final_g, wi, conv_w, wo, w1, w2)


def kernel(x, c, ctx, c_ctx, ada_w, ada_b, norm_mix_g, norm_mlp_g, ssd_w_in, ssd_conv_w, ssd_conv_b,
           ssd_dt_bias, ssd_a_log, ssd_d, ssd_norm_g, ssd_w_out, sc_w_in, sc_conv_w, sc_w_out,
           mlp_w1, mlp_w2, final_norm_g):
    batch, seq, d = x.shape
    ctx_len = ctx.shape[1]
    depth = ada_w.shape[0]
    assert depth == 2 and ssd_w_in.shape[0] == 1 and sc_w_in.shape[0] == 1
    d_inner = ssd_w_out.shape[1]
    n_heads = ssd_d.shape[1]
    xbc_dim = ssd_conv_w.shape[2]
    assert n_heads == SSM_GROUPS * HEADS_PER_GROUP and d_inner == SSM_GROUPS * GROUP_WIDTH

    mod_rows = 16
    cvec = jnp.zeros((mod_rows, d), F32).at[:batch].set(c).at[batch].set(c_ctx)
    mod = _modulation(cvec, ada_w, ada_b)
    mod0 = mod[0].reshape(mod_rows, 1, 6 * d)
    mod1 = mod[1].reshape(mod_rows, 1, 6 * d)

    w_in = ssd_w_in[0]
    wz = w_in[:, :d_inner].astype(BF16)
    nbc = SSM_GROUPS * SSM_STATE
    xs = slice(d_inner, 2 * d_inner)
    bs = slice(2 * d_inner, 2 * d_inner + nbc)
    cs_ = slice(2 * d_inner + nbc, 2 * d_inner + 2 * nbc)
    wxc = jnp.concatenate([w_in[:, xs], w_in[:, cs_]], axis=1).astype(BF16)
    wbt = w_in[:, bs].T.astype(BF16)
    cw = ssd_conv_w[0]
    cbias = ssd_conv_b[0]
    cw_xc = jnp.concatenate([cw[:, :d_inner], cw[:, d_inner + nbc:]], axis=1)
    cb_xc = jnp.concatenate([cbias[:d_inner], cbias[d_inner + nbc:]]).reshape(1, -1)
    cwb = jnp.concatenate([cw[:, d_inner:d_inner + nbc].T, cbias[d_inner:d_inner + nbc, None],
                           jnp.zeros((nbc, 4), F32)], axis=1)
    gi = jnp.arange(SSM_GROUPS)[:, None, None]
    di_ = jnp.arange(2)[None, :, None]
    ri = jnp.arange(HEADS_PER_GROUP)[None, None, :]
    flat = (di_ * n_heads + gi * HEADS_PER_GROUP + ri).reshape(SSM_GROUPS, 2 * HEADS_PER_GROUP)
    dt_idx = jnp.concatenate([flat, flat], axis=1).reshape(-1)
    wdt_t = w_in[:, d_inner + xbc_dim:][:, dt_idx].T.astype(BF16)
    dtb_col = ssd_dt_bias[0].reshape(-1)[dt_idx].reshape(-1, 1).astype(F32)
    alog_col = ssd_a_log[0].reshape(-1)[dt_idx].reshape(-1, 1).astype(F32)
    ng_mix0 = norm_mix_g[0].reshape(1, d)

    tm = 512
    x2d = x.reshape(batch * seq, d)
    ctx2d = ctx.reshape(batch * ctx_len, d)
    lat_row = lambda i: (i * tm) // seq
    z_l, xc_l, bt_l, dt_l = _ssd_in_proj(x2d, mod0, lat_row, ng_mix0, wz, wxc, wbt, wdt_t, cw_xc, cb_xc,
                                         cwb, dtb_col, alog_col, tm=tm, period=GRID_W, want_z=True)
    xc_c, bt_c, dt_c = _ssd_in_proj(ctx2d, mod0, lambda i: batch, ng_mix0, wz, wxc, wbt, wdt_t, cw_xc,
                                    cb_xc, cwb, dtb_col, alog_col, tm=ctx_len, period=ctx_len,
                                    want_z=False)

    dskip_row = jnp.repeat(ssd_d[0].astype(F32), SSM_HEAD_DIM).reshape(1, d_inner)
    ng_row = ssd_norm_g[0].reshape(1, d_inner)
    y_ssd = _ssd_scan(xc_l, bt_l, dt_l, xc_c, bt_c, dt_c, batch=batch, seq=seq, ctx_len=ctx_len)

    x1 = _ssd_out_mlp(y_ssd, xc_l, z_l, x2d, dskip_row, ng_row, mod0, lat_row, norm_mlp_g[0].reshape(1, d),
                      ssd_w_out[0].astype(BF16), mlp_w1[0].astype(BF16), mlp_w2[0].astype(BF16), tm=tm)

    out = _sc_layer(x1, mod1, lat_row, norm_mix_g[1].reshape(1, d), norm_mlp_g[1].reshape(1, d),
                    final_norm_g.reshape(1, d), sc_w_in[0].astype(BF16), sc_conv_w[0],
                    sc_w_out[0].astype(BF16), mlp_w1[1].astype(BF16), mlp_w2[1].astype(BF16),
                    tm=tm, period=GRID_W)
    return out.reshape(batch, seq, d)
```

```python
import functools

import numpy as np

import jax
import jax.numpy as jnp
from jax import lax
from jax.experimental import pallas as pl
from jax.experimental.pallas import tpu as pltpu

F32 = jnp.float32
BF16 = jnp.bfloat16

EPS = 1e-6
GRID_W = 64
SSM_HEAD_DIM = 64
SSM_GROUPS = 8
HEADS_PER_GROUP = 4
SSM_STATE = 128
SSM_CHUNK = 128
GROUP_WIDTH = HEADS_PER_GROUP * SSM_HEAD_DIM
DT_ROWS_PER_GROUP = 16
NEG_BIG = -1e30

VMEM_LIMIT_BYTES = 56 * 1024 * 1024


def _cparams(n_axes):
    return pltpu.CompilerParams(
        dimension_semantics=("arbitrary",) * n_axes,
        vmem_limit_bytes=VMEM_LIMIT_BYTES,
    )


def _const_spec(shape):
    nd = len(shape)
    return pl.BlockSpec(shape, lambda *_: (0,) * nd, pipeline_mode=pl.Buffered(1))


def _silu(u):
    return u * (1.0 / (1.0 + jnp.exp(-u)))


def _norm_mod(x, g, shift, scale):
    ms = jnp.mean(x * x, axis=-1, keepdims=True)
    y = x * lax.rsqrt(ms + EPS) * g
    return y * (1.0 + scale) + shift


def _edge_masked_taps(conv_w, period):
    r = jnp.arange(period)[:, None]
    return jnp.concatenate([jnp.where(r != 0, conv_w[0][None, :], 0.0),
                            jnp.where(r != period - 1, conv_w[2][None, :], 0.0)], axis=0)


def _row_conv3(u, w_ref, wm_ref, cols, period):
    rows = u.shape[0]
    reps = rows // period
    w_prev = pltpu.repeat(wm_ref[0:period, cols], reps, axis=0)
    w_next = pltpu.repeat(wm_ref[period:2 * period, cols], reps, axis=0)
    return pltpu.roll(u, 1, 0) * w_prev + u * w_ref[1:2, cols] + pltpu.roll(u, rows - 1, 0) * w_next


def _mod_kernel(c_ref, w_ref, b_ref, o_ref):
    s = _silu(c_ref[...]).astype(BF16)
    o_ref[...] = jnp.dot(s, w_ref[...].astype(BF16), preferred_element_type=F32) + b_ref[...]


def _modulation(cvec, ada_w, ada_b):
    depth, d, n = ada_w.shape
    rows = cvec.shape[0]
    tn = 1536
    return pl.pallas_call(
        _mod_kernel,
        grid=(depth, n // tn),
        in_specs=[
            pl.BlockSpec((rows, d), lambda i, j: (0, 0)),
            pl.BlockSpec((None, d, tn), lambda i, j: (i, 0, j)),
            pl.BlockSpec((None, 1, tn), lambda i, j: (i, 0, j)),
        ],
        out_specs=pl.BlockSpec((None, rows, tn), lambda i, j: (i, 0, j)),
        out_shape=jax.ShapeDtypeStruct((depth, rows, n), F32),
        compiler_params=_cparams(2),
        name="adaln_mod",
    )(cvec, ada_w, ada_b.reshape(depth, 1, n))


_NT_DIMS = (((1,), (1,)), ((), ()))


def _ssd_in_kernel(x_ref, sh_ref, sc_ref, g_ref, wz_ref, wxc_ref, wbt_ref, wdt_ref, cw_ref, cwm_ref, cb_ref,
                   cwb_ref, dtb_ref, alog_ref, *out_refs, period, ncol, nrow, want_z):
    if want_z:
        z_ref, xc_ref, bt_ref, dt_ref = out_refs
    else:
        xc_ref, bt_ref, dt_ref = out_refs
    h = _norm_mod(x_ref[...], g_ref[...], sh_ref[...], sc_ref[...]).astype(BF16)
    if want_z:
        z_ref[...] = jnp.dot(h, wz_ref[...], preferred_element_type=F32).astype(BF16)
    for j in range(xc_ref.shape[1] // ncol):
        cols = slice(j * ncol, (j + 1) * ncol)
        u = jnp.dot(h, wxc_ref[:, cols], preferred_element_type=F32)
        u = _row_conv3(u, cw_ref, cwm_ref, cols, period) + cb_ref[:, cols]
        xc_ref[:, cols] = _silu(u).astype(BF16)
    tm = x_ref.shape[0]
    pos = jnp.bitwise_and(lax.broadcasted_iota(jnp.int32, (1, tm), 1), period - 1)
    for j in range(bt_ref.shape[0] // nrow):
        rows = slice(j * nrow, (j + 1) * nrow)
        u = lax.dot_general(wbt_ref[rows, :], h, _NT_DIMS, preferred_element_type=F32)
        prev = jnp.where(pos != 0, pltpu.roll(u, 1, 1), 0.0)
        nxt = jnp.where(pos != period - 1, pltpu.roll(u, tm - 1, 1), 0.0)
        cwb = cwb_ref[rows, :]
        u = prev * cwb[:, 0:1] + u * cwb[:, 1:2] + nxt * cwb[:, 2:3] + cwb[:, 3:4]
        bt_ref[rows, :] = _silu(u).astype(BF16)
    raw = lax.dot_general(wdt_ref[...], h, _NT_DIMS, preferred_element_type=F32)
    v = raw + dtb_ref[...]
    sp = jnp.maximum(v, 0.0) + jnp.log1p(jnp.exp(-jnp.abs(v)))
    row = lax.broadcasted_iota(jnp.int32, (dt_ref.shape[0], 1), 0)
    is_la = jnp.bitwise_and(row, DT_ROWS_PER_GROUP - 1) >= DT_ROWS_PER_GROUP // 2
    dt_ref[...] = jnp.where(is_la, sp * (-jnp.exp(alog_ref[...])), sp)


def _ssd_in_proj(x2d, mod3, mod_row_of_tile, norm_g, wz, wxc, wbt, wdt_t, cw_xc, cb_xc, cwb, dtb_col,
                 alog_col, *, tm, period, want_z):
    rows, d = x2d.shape
    dz = wz.shape[1]
    dxc = wxc.shape[1]
    nb = wbt.shape[0]
    ndt = wdt_t.shape[0]
    ncol, nrow = 512, 256
    kern = functools.partial(_ssd_in_kernel, period=period, ncol=ncol, nrow=nrow, want_z=want_z)
    out_specs = [
        pl.BlockSpec((tm, dxc), lambda i: (i, 0)),
        pl.BlockSpec((nb, tm), lambda i: (0, i)),
        pl.BlockSpec((ndt, tm), lambda i: (0, i)),
    ]
    out_shape = [
        jax.ShapeDtypeStruct((rows, dxc), BF16),
        jax.ShapeDtypeStruct((nb, rows), BF16),
        jax.ShapeDtypeStruct((ndt, rows), F32),
    ]
    if want_z:
        out_specs.insert(0, pl.BlockSpec((tm, dz), lambda i: (i, 0)))
        out_shape.insert(0, jax.ShapeDtypeStruct((rows, dz), BF16))
    return pl.pallas_call(
        kern,
        grid=(rows // tm,),
        in_specs=[
            pl.BlockSpec((tm, d), lambda i: (i, 0)),
            pl.BlockSpec((None, 1, d), lambda i: (mod_row_of_tile(i), 0, 0)),
            pl.BlockSpec((None, 1, d), lambda i: (mod_row_of_tile(i), 0, 1)),
            _const_spec((1, d)),
            _const_spec((d, dz)),
            _const_spec((d, dxc)),
            _const_spec((nb, d)),
            _const_spec((ndt, d)),
            _const_spec((3, dxc)),
            _const_spec((2 * period, dxc)),
            _const_spec((1, dxc)),
            _const_spec(cwb.shape),
            _const_spec((ndt, 1)),
            _const_spec((ndt, 1)),
        ],
        out_specs=out_specs,
        out_shape=out_shape,
        compiler_params=_cparams(1),
        name="ssd_in_proj",
    )(x2d, mod3, mod3, norm_g, wz, wxc, wbt, wdt_t, cw_xc, _edge_masked_taps(cw_xc, period), cb_xc, cwb,
      dtb_col, alog_col)


COL_CS, COL_ECS, COL_W = 0, 8, 16


def _head_expand(colmat, lane0):
    r = colmat.shape[0]
    first = lax.broadcasted_iota(jnp.int32, (r, SSM_STATE), 1) < SSM_HEAD_DIM
    cols = [colmat[:, lane0 + hd:lane0 + hd + 1] for hd in range(HEADS_PER_GROUP)]
    lo = jnp.where(first, cols[0], cols[1])
    hi = jnp.where(first, cols[2], cols[3])
    return jnp.concatenate([lo, hi], axis=1)


def _expand_select(lane0):
    r = np.arange(2 * SSM_STATE)[:, None] % SSM_STATE
    l = np.arange(2 * GROUP_WIDTH)[None, :]
    src = lane0 + (l // GROUP_WIDTH) * HEADS_PER_GROUP + (l % GROUP_WIDTH) // SSM_HEAD_DIM
    return jnp.asarray(r == src, dtype=BF16)


def _broadcast_select(lane0, count):
    r = np.arange(2 * SSM_STATE)[:, None] % SSM_STATE
    l = np.arange(SSM_STATE * count)[None, :]
    return jnp.asarray(r == lane0 + l // SSM_STATE, dtype=BF16)


def _hi_lo(colmat):
    hi = colmat.astype(BF16)
    lo = (colmat - hi.astype(F32)).astype(BF16)
    return jnp.concatenate([hi, lo], axis=1)


def _decay_rows(dt_ref, n_chunks, upper, lower):
    q = SSM_CHUNK
    nh = HEADS_PER_GROUP
    dt = jnp.concatenate([dt_ref[0:2 * nh, c * q:(c + 1) * q] for c in range(n_chunks)], axis=0)
    la = jnp.concatenate([dt_ref[2 * nh:4 * nh, c * q:(c + 1) * q] for c in range(n_chunks)], axis=0)
    csf = jnp.dot(la, upper, preferred_element_type=F32, precision=lax.Precision.HIGHEST)
    csb = jnp.dot(la, lower, preferred_element_type=F32, precision=lax.Precision.HIGHEST)
    rows = dt.shape[0]
    row = lax.broadcasted_iota(jnp.int32, (rows, 1), 0)
    is_b = jnp.bitwise_and(row, nh) != 0
    cs = jnp.where(is_b, csb, csf)
    tot = jnp.where(is_b, csb[:, 0:1], csf[:, q - 1:q])
    other = jnp.where(is_b, pltpu.roll(dt, nh, 0), pltpu.roll(dt, rows - nh, 0))
    return cs, jnp.exp(cs), jnp.exp(tot - cs) * dt, cs - jnp.log(dt), jnp.log(dt + other)


def _ssd_scan_kernel(xl_ref, btl_ref, cl_ref, dtl_ref, xc_ref, btc_ref, dtc_ref,
                     selw_ref, sele_ref, selc_ref, o_ref,
                     cols_ref, rows_ref, xw_ref, ee_ref, csb_ref, cb_ref, sloc_ref, hin_ref, etot_ref):
    q = SSM_CHUNK
    nh = HEADS_PER_GROUP
    gw = GROUP_WIDTH
    n_lat = xl_ref.shape[0] // q
    n_ctx = xc_ref.shape[0] // q
    ctx_rows = n_ctx * q
    ki = lax.broadcasted_iota(jnp.int32, (q, q), 0)
    kj = lax.broadcasted_iota(jnp.int32, (q, q), 1)

    upper = (ki <= kj).astype(F32)
    lower = (ki >= kj).astype(F32)
    pad = jnp.zeros((q - 6 * nh, q), F32)
    for dref, n_chunks, base in ((dtc_ref, n_ctx, 0), (dtl_ref, n_lat, n_ctx)):
        cs, ecs, w, rk, lds = _decay_rows(dref, n_chunks, upper, lower)
        for c in range(n_chunks):
            r = slice(2 * nh * c, 2 * nh * (c + 1))
            colmat = jnp.concatenate([cs[r], ecs[r], w[r], pad], axis=0).T
            cols_ref[(base + c) * q:(base + c + 1) * q, :] = colmat
            for d in range(2):
                p0 = 0 if d else q - 1
                etot_ref[d, base + c] = _head_expand(colmat[p0:p0 + 1, :], COL_ECS + nh * d)
        if base:
            rows_ref[0:2 * nh * n_lat, :] = rk
            rows_ref[2 * nh * n_lat:4 * nh * n_lat, :] = lds

    def expand(hl, sel_ref):
        return jnp.dot(hl, sel_ref[...], preferred_element_type=F32)

    xf = xc_ref[...].astype(F32)
    hl = _hi_lo(cols_ref[0:ctx_rows, :])
    xw_ref[0:ctx_rows, :] = (jnp.concatenate([xf, xf], axis=1) * expand(hl, selw_ref)).astype(BF16)
    blk = 4 * q
    for j in range(n_lat * q // blk):
        rows = slice(j * blk, (j + 1) * blk)
        xf = xl_ref[rows, :].astype(F32)
        r0 = ctx_rows + j * blk
        hl = _hi_lo(cols_ref[r0:r0 + blk, :])
        xw_ref[r0:r0 + blk, :] = (jnp.concatenate([xf, xf], axis=1) * expand(hl, selw_ref)).astype(BF16)
        ee_ref[rows, :] = expand(hl, sele_ref)
        csb_ref[rows, :] = expand(hl, selc_ref)

    for c in range(n_ctx):
        sloc_ref[c] = jnp.dot(btc_ref[:, c * q:(c + 1) * q], xw_ref[c * q:(c + 1) * q, :],
                              preferred_element_type=F32)

    def s_body(c, carry):
        r0 = pl.multiple_of(c * q, q)
        bt = btl_ref[:, pl.ds(r0, q)]
        sloc_ref[n_ctx + c] = jnp.dot(bt, xw_ref[pl.ds(ctx_rows + r0, q), :], preferred_element_type=F32)
        cb_ref[c] = jnp.dot(cl_ref[pl.ds(r0, q), :], bt, preferred_element_type=F32)
        return carry

    lax.fori_loop(0, n_lat, s_body, 0, unroll=4)

    for d in range(2):
        lanes = slice(gw * d, gw * (d + 1))
        h = jnp.zeros((SSM_STATE, gw), F32)
        for c in (range(n_ctx - 1, -1, -1) if d else range(n_ctx)):
            h = h * etot_ref[d, c] + sloc_ref[c, :, lanes]

        def b_body(i, h, d=d, lanes=lanes):
            c = (n_lat - 1 - i) if d else i
            hin_ref[c, :, lanes] = h.astype(BF16)
            return h * etot_ref[d, n_ctx + c] + sloc_ref[n_ctx + c, :, lanes]

        lax.fori_loop(0, n_lat, b_body, h, unroll=4)

    lane_head = lax.broadcasted_iota(jnp.int32, (q, gw), 1) // SSM_HEAD_DIM
    below = ki > kj
    above = ki < kj

    def c_body(c, carry):
        r0 = pl.multiple_of(c * q, q)
        x = xl_ref[pl.ds(r0, q), :]
        r8 = pl.multiple_of(c * 2 * nh, 2 * nh)
        rk8 = rows_ref[pl.ds(r8, 2 * nh), :]
        lds8 = rows_ref[pl.ds(2 * nh * n_lat + r8, 2 * nh), :]
        cb = cb_ref[c]
        m_parts = []
        x_parts = []
        for hd in range(nh):
            segf = csb_ref[pl.ds(r0, q), q * hd:q * (hd + 1)] - rk8[hd:hd + 1, :]
            segb = csb_ref[pl.ds(r0, q), q * (nh + hd):q * (nh + hd + 1)] - rk8[nh + hd:nh + hd + 1, :]
            arg = jnp.where(below, segf, jnp.where(above, segb, lds8[hd:hd + 1, :]))
            m_parts.append((jnp.exp(arg) * cb).astype(BF16))
            x_parts.append(jnp.where(lane_head == hd, x, jnp.zeros_like(x)))
        m_all = jnp.concatenate(m_parts, axis=1)
        x_bd = jnp.concatenate(x_parts, axis=0)
        y = jnp.dot(m_all, x_bd, preferred_element_type=F32)
        y_off = ee_ref[pl.ds(r0, q), :] * jnp.dot(cl_ref[pl.ds(r0, q), :], hin_ref[c],
                                                  preferred_element_type=F32)
        o_ref[pl.ds(r0, q), :] = (y + y_off[:, 0:gw] + y_off[:, gw:2 * gw]).astype(BF16)
        return carry

    lax.fori_loop(0, n_lat, c_body, 0, unroll=2)


def _ssd_scan(xc_l, bt_l, dt_l, xc_c, bt_c, dt_c, *, batch, seq, ctx_len):
    g = SSM_GROUPS
    gw = GROUP_WIDTH
    n = SSM_STATE
    c_off = (g * gw) // n
    n_lat = seq // SSM_CHUNK
    n_all = n_lat + ctx_len // SSM_CHUNK
    sel_w = _expand_select(COL_W)
    sel_e = _expand_select(COL_ECS)
    sel_c = _broadcast_select(COL_CS, 2 * HEADS_PER_GROUP)
    return pl.pallas_call(
        _ssd_scan_kernel,
        grid=(batch, g),
        in_specs=[
            pl.BlockSpec((seq, gw), lambda b, k: (b, k)),
            pl.BlockSpec((n, seq), lambda b, k: (k, b)),
            pl.BlockSpec((seq, n), lambda b, k: (b, c_off + k)),
            pl.BlockSpec((DT_ROWS_PER_GROUP, seq), lambda b, k: (k, b)),
            pl.BlockSpec((ctx_len, gw), lambda b, k: (b, k)),
            pl.BlockSpec((n, ctx_len), lambda b, k: (k, b)),
            pl.BlockSpec((DT_ROWS_PER_GROUP, ctx_len), lambda b, k: (k, b)),
            _const_spec(sel_w.shape),
            _const_spec(sel_e.shape),
            _const_spec(sel_c.shape),
        ],
        out_specs=pl.BlockSpec((seq, gw), lambda b, k: (b, k)),
        out_shape=jax.ShapeDtypeStruct((batch * seq, g * gw), BF16),
        scratch_shapes=[
            pltpu.VMEM((n_all * SSM_CHUNK, SSM_CHUNK), F32),
            pltpu.VMEM((4 * HEADS_PER_GROUP * n_lat, SSM_CHUNK), F32),
            pltpu.VMEM((n_all * SSM_CHUNK, 2 * gw), BF16),
            pltpu.VMEM((seq, 2 * gw), F32),
            pltpu.VMEM((seq, 2 * HEADS_PER_GROUP * SSM_CHUNK), F32),
            pltpu.VMEM((n_lat, SSM_CHUNK, SSM_CHUNK), F32),
            pltpu.VMEM((n_all, n, 2 * gw), F32),
            pltpu.VMEM((n_lat, n, 2 * gw), BF16),
            pltpu.VMEM((2, n_all, 1, gw), F32),
        ],
        compiler_params=_cparams(2),
        name="ssd_scan",
    )(xc_l, bt_l, xc_l, dt_l, xc_c, bt_c, dt_c, sel_w, sel_e, sel_c)


def _mlp_tail(x1, g_ref, sh_ref, sc_ref, gate_ref, w1_ref, w2_ref, nff):
    h2 = _norm_mod(x1, g_ref[...], sh_ref[...], sc_ref[...]).astype(BF16)
    dff = w1_ref.shape[1]
    acc = None
    for j in range(dff // nff):
        c0 = j * nff
        a = jnp.dot(h2, w1_ref[:, c0:c0 + nff], preferred_element_type=F32)
        a = jnp.square(jnp.maximum(a, 0.0)).astype(BF16)
        p = jnp.dot(a, w2_ref[c0:c0 + nff, :], preferred_element_type=F32)
        acc = p if acc is None else acc + p
    return x1 + gate_ref[...] * acc


def _ssd_out_kernel(y_ref, xs_ref, z_ref, x_ref, dsk_ref, sng_ref, gm_ref, shf_ref, scf_ref, gf_ref,
                    ng_ref, wo_ref, w1_ref, w2_ref, o_ref):
    acc = None
    for g0 in range(0, y_ref.shape[1], GROUP_WIDTH):
        cols = slice(g0, g0 + GROUP_WIDTH)
        y = y_ref[:, cols].astype(F32) + dsk_ref[:, cols] * xs_ref[:, cols].astype(F32)
        y = y * _silu(z_ref[:, cols].astype(F32))
        y = y * lax.rsqrt(jnp.mean(y * y, axis=-1, keepdims=True) + EPS)
        yn = (y * sng_ref[:, cols]).astype(BF16)
        p = jnp.dot(yn, wo_ref[cols, :], preferred_element_type=F32)
        acc = p if acc is None else acc + p
    x1 = x_ref[...] + gm_ref[...] * acc
    o_ref[...] = _mlp_tail(x1, ng_ref, shf_ref, scf_ref, gf_ref, w1_ref, w2_ref, 1024)


def _mod_spec(d, row_of_tile, k):
    return pl.BlockSpec((None, 1, d), lambda i: (row_of_tile(i), 0, k))


def _ssd_out_mlp(y2d, xc2d, z2d, x2d, dskip_row, ssd_ng_row, mod3, row_of_tile, norm_g, wo, w1, w2, *, tm):
    rows, d = x2d.shape
    di = y2d.shape[1]
    dff = w1.shape[1]
    return pl.pallas_call(
        _ssd_out_kernel,
        grid=(rows // tm,),
        in_specs=[
            pl.BlockSpec((tm, di), lambda i: (i, 0)),
            pl.BlockSpec((tm, di), lambda i: (i, 0)),
            pl.BlockSpec((tm, di), lambda i: (i, 0)),
            pl.BlockSpec((tm, d), lambda i: (i, 0)),
            _const_spec((1, di)),
            _const_spec((1, di)),
            _mod_spec(d, row_of_tile, 2),
            _mod_spec(d, row_of_tile, 3),
            _mod_spec(d, row_of_tile, 4),
            _mod_spec(d, row_of_tile, 5),
            _const_spec((1, d)),
            _const_spec((di, d)),
            _const_spec((d, dff)),
            _const_spec((dff, d)),
        ],
        out_specs=pl.BlockSpec((tm, d), lambda i: (i, 0)),
        out_shape=jax.ShapeDtypeStruct((rows, d), F32),
        compiler_params=_cparams(1),
        name="ssd_out_mlp",
    )(y2d, xc2d, z2d, x2d, dskip_row, ssd_ng_row, mod3, mod3, mod3, mod3, norm_g, wo, w1, w2)


def _sc_layer_kernel(x_ref, shm_ref, scm_ref, gm_ref, shf_ref, scf_ref, gf_ref, ngm_ref, ngf_ref,
                     fg_ref, wi_ref, cw_ref, cwm_ref, wo_ref, w1_ref, w2_ref, o_ref, *, period):
    x = x_ref[...]
    h = _norm_mod(x, ngm_ref[...], shm_ref[...], scm_ref[...]).astype(BF16)
    w = wo_ref.shape[0]
    bg = jnp.dot(h, wi_ref[:, 0:w], preferred_element_type=F32)
    cg = jnp.dot(h, wi_ref[:, w:2 * w], preferred_element_type=F32)
    xv = jnp.dot(h, wi_ref[:, 2 * w:3 * w], preferred_element_type=F32)
    u = (bg * _row_conv3(cg * xv, cw_ref, cwm_ref, slice(0, w), period)).astype(BF16)
    y = jnp.dot(u, wo_ref[...], preferred_element_type=F32)
    x1 = x + gm_ref[...] * y
    x2 = _mlp_tail(x1, ngf_ref, shf_ref, scf_ref, gf_ref, w1_ref, w2_ref, 1024)
    ms = jnp.mean(x2 * x2, axis=-1, keepdims=True)
    o_ref[...] = x2 * lax.rsqrt(ms + EPS) * fg_ref[...]


def _sc_layer(x2d, mod3, row_of_tile, ng_mix, ng_mlp, final_g, wi, conv_w, wo, w1, w2, *, tm, period):
    rows, d = x2d.shape
    dff = w1.shape[1]
    kern = functools.partial(_sc_layer_kernel, period=period)
    return pl.pallas_call(
        kern,
        grid=(rows // tm,),
        in_specs=[
            pl.BlockSpec((tm, d), lambda i: (i, 0)),
            _mod_spec(d, row_of_tile, 0),
            _mod_spec(d, row_of_tile, 1),
            _mod_spec(d, row_of_tile, 2),
            _mod_spec(d, row_of_tile, 3),
            _mod_spec(d, row_of_tile, 4),
            _mod_spec(d, row_of_tile, 5),
            _const_spec((1, d)),
            _const_spec((1, d)),
            _const_spec((1, d)),
            _const_spec(wi.shape),
            _const_spec(conv_w.shape),
            _const_spec((2 * period, conv_w.shape[1])),
            _const_spec(wo.shape),
            _const_spec((d, dff)),
            _const_spec((dff, d)),
        ],
        out_specs=pl.BlockSpec((tm, d), lambda i: (i, 0)),
        out_shape=jax.ShapeDtypeStruct((rows, d), F32),
        compiler_params=_cparams(1),
        name="shortconv_layer",
    )(x2d, mod3, mod3, mod3, mod3, mod3, mod3, ng_mix, ng_mlp, final_g, wi, conv_w,
      _edge_masked_taps(conv_w, period), wo, w1, w2)


def kernel(x, c, ctx, c_ctx, ada_w, ada_b, norm_mix_g, norm_mlp_g, ssd_w_in, ssd_conv_w, ssd_conv_b,
           ssd_dt_bias, ssd_a_log, ssd_d, ssd_norm_g, ssd_w_out, sc_w_in, sc_conv_w, sc_w_out,
           mlp_w1, mlp_w2, final_norm_g):
    batch, seq, d = x.shape
    ctx_len = ctx.shape[1]
    depth = ada_w.shape[0]
    assert depth == 2 and ssd_w_in.shape[0] == 1 and sc_w_in.shape[0] == 1
    d_inner = ssd_w_out.shape[1]
    n_heads = ssd_d.shape[1]
    xbc_dim = ssd_conv_w.shape[2]
    assert n_heads == SSM_GROUPS * HEADS_PER_GROUP and d_inner == SSM_GROUPS * GROUP_WIDTH

    mod_rows = 16
    cvec = jnp.zeros((mod_rows, d), F32).at[:batch].set(c).at[batch].set(c_ctx)
    mod = _modulation(cvec, ada_w, ada_b)
    mod0 = mod[0].reshape(mod_rows, 1, 6 * d)
    mod1 = mod[1].reshape(mod_rows, 1, 6 * d)

    w_in = ssd_w_in[0]
    wz = w_in[:, :d_inner].astype(BF16)
    nbc = SSM_GROUPS * SSM_STATE
    xs = slice(d_inner, 2 * d_inner)
    bs = slice(2 * d_inner, 2 * d_inner + nbc)
    cs_ = slice(2 * d_inner + nbc, 2 * d_inner + 2 * nbc)
    wxc = jnp.concatenate([w_in[:, xs], w_in[:, cs_]], axis=1).astype(BF16)
    wbt = w_in[:, bs].T.astype(BF16)
    cw = ssd_conv_w[0]
    cbias = ssd_conv_b[0]
    cw_xc = jnp.concatenate([cw[:, :d_inner], cw[:, d_inner + nbc:]], axis=1)
    cb_xc = jnp.concatenate([cbias[:d_inner], cbias[d_inner + nbc:]]).reshape(1, -1)
    cwb = jnp.concatenate([cw[:, d_inner:d_inner + nbc].T, cbias[d_inner:d_inner + nbc, None],
                           jnp.zeros((nbc, 4), F32)], axis=1)
    gi = jnp.arange(SSM_GROUPS)[:, None, None]
    di_ = jnp.arange(2)[None, :, None]
    ri = jnp.arange(HEADS_PER_GROUP)[None, None, :]
    flat = (di_ * n_heads + gi * HEADS_PER_GROUP + ri).reshape(SSM_GROUPS, 2 * HEADS_PER_GROUP)
    dt_idx = jnp.concatenate([flat, flat], axis=1).reshape(-1)
    wdt_t = w_in[:, d_inner + xbc_dim:][:, dt_idx].T.astype(BF16)
    dtb_col = ssd_dt_bias[0].reshape(-1)[dt_idx].reshape(-1, 1).astype(F32)
    alog_col = ssd_a_log[0].reshape(-1)[dt_idx].reshape(-1, 1).astype(F32)
    ng_mix0 = norm_mix_g[0].reshape(1, d)

    tm = 512
    x2d = x.reshape(batch * seq, d)
    ctx2d = ctx.reshape(batch * ctx_len, d)
    lat_row = lambda i: (i * tm) // seq
    z_l, xc_l, bt_l, dt_l = _ssd_in_proj(x2d, mod0, lat_row, ng_mix0, wz, wxc, wbt, wdt_t, cw_xc, cb_xc,
                                         cwb, dtb_col, alog_col, tm=tm, period=GRID_W, want_z=True)
    xc_c, bt_c, dt_c = _ssd_in_proj(ctx2d, mod0, lambda i: batch, ng_mix0, wz, wxc, wbt, wdt_t, cw_xc,
                                    cb_xc, cwb, dtb_col, alog_col, tm=ctx_len, period=ctx_len,
                                    want_z=False)

    dskip_row = jnp.repeat(ssd_d[0].astype(F32), SSM_HEAD_DIM).reshape(1, d_inner)
    ng_row = ssd_norm_g[0].reshape(1, d_inner)
    y_ssd = _ssd_scan(xc_l, bt_l, dt_l, xc_c, bt_c, dt_c, batch=batch, seq=seq, ctx_len=ctx_len)

    x1 = _ssd_out_mlp(y_ssd, xc_l, z_l, x2d, dskip_row, ng_row, mod0, lat_row, norm_mlp_g[0].reshape(1, d),
                      ssd_w_out[0].astype(BF16), mlp_w1[0].astype(BF16), mlp_w2[0].astype(BF16), tm=tm)

    out = _sc_layer(x1, mod1, lat_row, norm_mix_g[1].reshape(1, d), norm_mlp_g[1].reshape(1, d),
                    final_norm_g.reshape(1, d), sc_w_in[0].astype(BF16), sc_conv_w[0],
                    sc_w_out[0].astype(BF16), mlp_w1[1].astype(BF16), mlp_w2[1].astype(BF16),
                    tm=tm, period=GRID_W)
    return out.reshape(batch, seq, d)
```

```python
import functools

import numpy as np

import jax
import jax.numpy as jnp
from jax import lax
from jax.experimental import pallas as pl
from jax.experimental.pallas import tpu as pltpu

F32 = jnp.float32
BF16 = jnp.bfloat16

EPS = 1e-6
GRID_W = 64
SSM_HEAD_DIM = 64
SSM_GROUPS = 8
HEADS_PER_GROUP = 4
SSM_STATE = 128
SSM_CHUNK = 128
GROUP_WIDTH = HEADS_PER_GROUP * SSM_HEAD_DIM
DT_ROWS_PER_GROUP = 16
NEG_BIG = -1e30

VMEM_LIMIT_BYTES = 56 * 1024 * 1024


def _cparams(n_axes):
    return pltpu.CompilerParams(
        dimension_semantics=("arbitrary",) * n_axes,
        vmem_limit_bytes=VMEM_LIMIT_BYTES,
    )


def _const_spec(shape):
    nd = len(shape)
    return pl.BlockSpec(shape, lambda *_: (0,) * nd, pipeline_mode=pl.Buffered(1))


def _silu(u):
    return u * (1.0 / (1.0 + jnp.exp(-u)))


def _norm_mod(x, g, shift, scale):
    ms = jnp.mean(x * x, axis=-1, keepdims=True)
    y = x * lax.rsqrt(ms + EPS) * g
    return y * (1.0 + scale) + shift


def _edge_masked_taps(conv_w, period):
    r = jnp.arange(period)[:, None]
    return jnp.concatenate([jnp.where(r != 0, conv_w[0][None, :], 0.0),
                            jnp.where(r != period - 1, conv_w[2][None, :], 0.0)], axis=0)


def _row_conv3(u, w_ref, wm_ref, cols, period):
    rows = u.shape[0]
    reps = rows // period
    w_prev = pltpu.repeat(wm_ref[0:period, cols], reps, axis=0)
    w_next = pltpu.repeat(wm_ref[period:2 * period, cols], reps, axis=0)
    return pltpu.roll(u, 1, 0) * w_prev + u * w_ref[1:2, cols] + pltpu.roll(u, rows - 1, 0) * w_next


def _mod_kernel(c_ref, w_ref, b_ref, o_ref):
    s = _silu(c_ref[...]).astype(BF16)
    o_ref[...] = jnp.dot(s, w_ref[...].astype(BF16), preferred_element_type=F32) + b_ref[...]


def _modulation(cvec, ada_w, ada_b):
    depth, d, n = ada_w.shape
    rows = cvec.shape[0]
    tn = 1536
    return pl.pallas_call(
        _mod_kernel,
        grid=(depth, n // tn),
        in_specs=[
            pl.BlockSpec((rows, d), lambda i, j: (0, 0)),
            pl.BlockSpec((None, d, tn), lambda i, j: (i, 0, j)),
            pl.BlockSpec((None, 1, tn), lambda i, j: (i, 0, j)),
        ],
        out_specs=pl.BlockSpec((None, rows, tn), lambda i, j: (i, 0, j)),
        out_shape=jax.ShapeDtypeStruct((depth, rows, n), F32),
        compiler_params=_cparams(2),
        name="adaln_mod",
    )(cvec, ada_w, ada_b.reshape(depth, 1, n))


_NT_DIMS = (((1,), (1,)), ((), ()))


def _ssd_in_kernel(x_ref, sh_ref, sc_ref, g_ref, wz_ref, wxc_ref, wbt_ref, wdt_ref, cw_ref, cwm_ref, cb_ref,
                   cwb_ref, dtb_ref, alog_ref, *out_refs, period, ncol, nrow, want_z):
    if want_z:
        z_ref, xc_ref, bt_ref, dt_ref = out_refs
    else:
        xc_ref, bt_ref, dt_ref = out_refs
    h = _norm_mod(x_ref[...], g_ref[...], sh_ref[...], sc_ref[...]).astype(BF16)
    if want_z:
        z_ref[...] = jnp.dot(h, wz_ref[...], preferred_element_type=F32).astype(BF16)
    for j in range(xc_ref.shape[1] // ncol):
        cols = slice(j * ncol, (j + 1) * ncol)
        u = jnp.dot(h, wxc_ref[:, cols], preferred_element_type=F32)
        u = _row_conv3(u, cw_ref, cwm_ref, cols, period) + cb_ref[:, cols]
        xc_ref[:, cols] = _silu(u).astype(BF16)
    tm = x_ref.shape[0]
    pos = jnp.bitwise_and(lax.broadcasted_iota(jnp.int32, (1, tm), 1), period - 1)
    for j in range(bt_ref.shape[0] // nrow):
        rows = slice(j * nrow, (j + 1) * nrow)
        u = lax.dot_general(wbt_ref[rows, :], h, _NT_DIMS, preferred_element_type=F32)
        prev = jnp.where(pos != 0, pltpu.roll(u, 1, 1), 0.0)
        nxt = jnp.where(pos != period - 1, pltpu.roll(u, tm - 1, 1), 0.0)
        cwb = cwb_ref[rows, :]
        u = prev * cwb[:, 0:1] + u * cwb[:, 1:2] + nxt * cwb[:, 2:3] + cwb[:, 3:4]
        bt_ref[rows, :] = _silu(u).astype(BF16)
    raw = lax.dot_general(wdt_ref[...], h, _NT_DIMS, preferred_element_type=F32)
    v = raw + dtb_ref[...]
    sp = jnp.maximum(v, 0.0) + jnp.log1p(jnp.exp(-jnp.abs(v)))
    row = lax.broadcasted_iota(jnp.int32, (dt_ref.shape[0], 1), 0)
    is_la = jnp.bitwise_and(row, DT_ROWS_PER_GROUP - 1) >= DT_ROWS_PER_GROUP // 2
    dt_ref[...] = jnp.where(is_la, sp * (-jnp.exp(alog_ref[...])), sp)


def _ssd_in_proj(x2d, mod3, mod_row_of_tile, norm_g, wz, wxc, wbt, wdt_t, cw_xc, cb_xc, cwb, dtb_col,
                 alog_col, *, tm, period, want_z):
    rows, d = x2d.shape
    dz = wz.shape[1]
    dxc = wxc.shape[1]
    nb = wbt.shape[0]
    ndt = wdt_t.shape[0]
    ncol, nrow = 512, 256
    kern = functools.partial(_ssd_in_kernel, period=period, ncol=ncol, nrow=nrow, want_z=want_z)
    out_specs = [
        pl.BlockSpec((tm, dxc), lambda i: (i, 0)),
        pl.BlockSpec((nb, tm), lambda i: (0, i)),
        pl.BlockSpec((ndt, tm), lambda i: (0, i)),
    ]
    out_shape = [
        jax.ShapeDtypeStruct((rows, dxc), BF16),
        jax.ShapeDtypeStruct((nb, rows), BF16),
        jax.ShapeDtypeStruct((ndt, rows), F32),
    ]
    if want_z:
        out_specs.insert(0, pl.BlockSpec((tm, dz), lambda i: (i, 0)))
        out_shape.insert(0, jax.ShapeDtypeStruct((rows, dz), BF16))
    return pl.pallas_call(
        kern,
        grid=(rows // tm,),
        in_specs=[
            pl.BlockSpec((tm, d), lambda i: (i, 0)),
            pl.BlockSpec((None, 1, d), lambda i: (mod_row_of_tile(i), 0, 0)),
            pl.BlockSpec((None, 1, d), lambda i: (mod_row_of_tile(i), 0, 1)),
            _const_spec((1, d)),
            _const_spec((d, dz)),
            _const_spec((d, dxc)),
            _const_spec((nb, d)),
            _const_spec((ndt, d)),
            _const_spec((3, dxc)),
            _const_spec((2 * period, dxc)),
            _const_spec((1, dxc)),
            _const_spec(cwb.shape),
            _const_spec((ndt, 1)),
            _const_spec((ndt, 1)),
        ],
        out_specs=out_specs,
        out_shape=out_shape,
        compiler_params=_cparams(1),
        name="ssd_in_proj",
    )(x2d, mod3, mod3, norm_g, wz, wxc, wbt, wdt_t, cw_xc, _edge_masked_taps(cw_xc, period), cb_xc, cwb,
      dtb_col, alog_col)


COL_CS, COL_ECS, COL_W = 0, 8, 16


def _head_expand(colmat, lane0):
    r = colmat.shape[0]
    first = lax.broadcasted_iota(jnp.int32, (r, SSM_STATE), 1) < SSM_HEAD_DIM
    cols = [colmat[:, lane0 + hd:lane0 + hd + 1] for hd in range(HEADS_PER_GROUP)]
    lo = jnp.where(first, cols[0], cols[1])
    hi = jnp.where(first, cols[2], cols[3])
    return jnp.concatenate([lo, hi], axis=1)


def _expand_select(lane0):
    r = np.arange(2 * SSM_STATE)[:, None] % SSM_STATE
    l = np.arange(2 * GROUP_WIDTH)[None, :]
    src = lane0 + (l // GROUP_WIDTH) * HEADS_PER_GROUP + (l % GROUP_WIDTH) // SSM_HEAD_DIM
    return jnp.asarray(r == src, dtype=BF16)


def _broadcast_select(lane0, count):
    r = np.arange(2 * SSM_STATE)[:, None] % SSM_STATE
    l = np.arange(SSM_STATE * count)[None, :]
    return jnp.asarray(r == lane0 + l // SSM_STATE, dtype=BF16)


def _hi_lo(colmat):
    hi = colmat.astype(BF16)
    lo = (colmat - hi.astype(F32)).astype(BF16)
    return jnp.concatenate([hi, lo], axis=1)


def _decay_rows(dt_ref, n_chunks, upper, lower):
    q = SSM_CHUNK
    nh = HEADS_PER_GROUP
    dt = jnp.concatenate([dt_ref[0:2 * nh, c * q:(c + 1) * q] for c in range(n_chunks)], axis=0)
    la = jnp.concatenate([dt_ref[2 * nh:4 * nh, c * q:(c + 1) * q] for c in range(n_chunks)], axis=0)
    csf = jnp.dot(la, upper, preferred_element_type=F32, precision=lax.Precision.HIGHEST)
    csb = jnp.dot(la, lower, preferred_element_type=F32, precision=lax.Precision.HIGHEST)
    rows = dt.shape[0]
    row = lax.broadcasted_iota(jnp.int32, (rows, 1), 0)
    is_b = jnp.bitwise_and(row, nh) != 0
    cs = jnp.where(is_b, csb, csf)
    tot = jnp.where(is_b, csb[:, 0:1], csf[:, q - 1:q])
    other = jnp.where(is_b, pltpu.roll(dt, nh, 0), pltpu.roll(dt, rows - nh, 0))
    return cs, jnp.exp(cs), jnp.exp(tot - cs) * dt, cs - jnp.log(dt), jnp.log(dt + other)


def _ssd_scan_kernel(xl_ref, btl_ref, cl_ref, dtl_ref, xc_ref, btc_ref, dtc_ref,
                     selw_ref, sele_ref, selc_ref, o_ref,
                     cols_ref, rows_ref, xw_ref, ee_ref, csb_ref, cb_ref, sloc_ref, hin_ref, etot_ref):
    q = SSM_CHUNK
    nh = HEADS_PER_GROUP
    gw = GROUP_WIDTH
    n_lat = xl_ref.shape[0] // q
    n_ctx = xc_ref.shape[0] // q
    ctx_rows = n_ctx * q
    ki = lax.broadcasted_iota(jnp.int32, (q, q), 0)
    kj = lax.broadcasted_iota(jnp.int32, (q, q), 1)

    upper = (ki <= kj).astype(F32)
    lower = (ki >= kj).astype(F32)
    pad = jnp.zeros((q - 6 * nh, q), F32)
    for dref, n_chunks, base in ((dtc_ref, n_ctx, 0), (dtl_ref, n_lat, n_ctx)):
        cs, ecs, w, rk, lds = _decay_rows(dref, n_chunks, upper, lower)
        for c in range(n_chunks):
            r = slice(2 * nh * c, 2 * nh * (c + 1))
            colmat = jnp.concatenate([cs[r], ecs[r], w[r], pad], axis=0).T
            cols_ref[(base + c) * q:(base + c + 1) * q, :] = colmat
            for d in range(2):
                p0 = 0 if d else q - 1
                etot_ref[d, base + c] = _head_expand(colmat[p0:p0 + 1, :], COL_ECS + nh * d)
        if base:
            rows_ref[0:2 * nh * n_lat, :] = rk
            rows_ref[2 * nh * n_lat:4 * nh * n_lat, :] = lds

    def expand(hl, sel_ref):
        return jnp.dot(hl, sel_ref[...], preferred_element_type=F32)

    xf = xc_ref[...].astype(F32)
    hl = _hi_lo(cols_ref[0:ctx_rows, :])
    xw_ref[0:ctx_rows, :] = (jnp.concatenate([xf, xf], axis=1) * expand(hl, selw_ref)).astype(BF16)
    blk = 4 * q
    for j in range(n_lat * q // blk):
        rows = slice(j * blk, (j + 1) * blk)
        xf = xl_ref[rows, :].astype(F32)
        r0 = ctx_rows + j * blk
        hl = _hi_lo(cols_ref[r0:r0 + blk, :])
        xw_ref[r0:r0 + blk, :] = (jnp.concatenate([xf, xf], axis=1) * expand(hl, selw_ref)).astype(BF16)
        ee_ref[rows, :] = expand(hl, sele_ref)
        csb_ref[rows, :] = expand(hl, selc_ref)

    for c in range(n_ctx):
        sloc_ref[c] = jnp.dot(btc_ref[:, c * q:(c + 1) * q], xw_ref[c * q:(c + 1) * q, :],
                              preferred_element_type=F32)

    def s_body(c, carry):
        r0 = pl.multiple_of(c * q, q)
        bt = btl_ref[:, pl.ds(r0, q)]
        sloc_ref[n_ctx + c] = jnp.dot(bt, xw_ref[pl.ds(ctx_rows + r0, q), :], preferred_element_type=F32)
        cb_ref[c] = jnp.dot(cl_ref[pl.ds(r0, q), :], bt, preferred_element_type=F32)
        return carry

    lax.fori_loop(0, n_lat, s_body, 0, unroll=4)

    for d in range(2):
        lanes = slice(gw * d, gw * (d + 1))
        h = jnp.zeros((SSM_STATE, gw), F32)
        for c in (range(n_ctx - 1, -1, -1) if d else range(n_ctx)):
            h = h * etot_ref[d, c] + sloc_ref[c, :, lanes]

        def b_body(i, h, d=d, lanes=lanes):
            c = (n_lat - 1 - i) if d else i
            hin_ref[c, :, lanes] = h.astype(BF16)
            return h * etot_ref[d, n_ctx + c] + sloc_ref[n_ctx + c, :, lanes]

        lax.fori_loop(0, n_lat, b_body, h, unroll=4)

    lane_head = lax.broadcasted_iota(jnp.int32, (q, gw), 1) // SSM_HEAD_DIM
    below = ki > kj
    above = ki < kj

    def c_body(c, carry):
        r0 = pl.multiple_of(c * q, q)
        x = xl_ref[pl.ds(r0, q), :]
        r8 = pl.multiple_of(c * 2 * nh, 2 * nh)
        rk8 = rows_ref[pl.ds(r8, 2 * nh), :]
        lds8 = rows_ref[pl.ds(2 * nh * n_lat + r8, 2 * nh), :]
        cb = cb_ref[c]
        m_parts = []
        x_parts = []
        for hd in range(nh):
            segf = csb_ref[pl.ds(r0, q), q * hd:q * (hd + 1)] - rk8[hd:hd + 1, :]
            segb = csb_ref[pl.ds(r0, q), q * (nh + hd):q * (nh + hd + 1)] - rk8[nh + hd:nh + hd + 1, :]
            arg = jnp.where(below, segf, jnp.where(above, segb, lds8[hd:hd + 1, :]))
            m_parts.append((jnp.exp(arg) * cb).astype(BF16))
            x_parts.append(jnp.where(lane_head == hd, x, jnp.zeros_like(x)))
        m_all = jnp.concatenate(m_parts, axis=1)
        x_bd = jnp.concatenate(x_parts, axis=0)
        y = jnp.dot(m_all, x_bd, preferred_element_type=F32)
        y_off = ee_ref[pl.ds(r0, q), :] * jnp.dot(cl_ref[pl.ds(r0, q), :], hin_ref[c],
                                                  preferred_element_type=F32)
        o_ref[pl.ds(r0, q), :] = (y + y_off[:, 0:gw] + y_off[:, gw:2 * gw]).astype(BF16)
        return carry

    lax.fori_loop(0, n_lat, c_body, 0, unroll=2)


def _ssd_scan(xc_l, bt_l, dt_l, xc_c, bt_c, dt_c, *, batch, seq, ctx_len):
    g = SSM_GROUPS
    gw = GROUP_WIDTH
    n = SSM_STATE
    c_off = (g * gw) // n
    n_lat = seq // SSM_CHUNK
    n_all = n_lat + ctx_len // SSM_CHUNK
    sel_w = _expand_select(COL_W)
    sel_e = _expand_select(COL_ECS)
    sel_c = _broadcast_select(COL_CS, 2 * HEADS_PER_GROUP)
    return pl.pallas_call(
        _ssd_scan_kernel,
        grid=(batch, g),
        in_specs=[
            pl.BlockSpec((seq, gw), lambda b, k: (b, k)),
            pl.BlockSpec((n, seq), lambda b, k: (k, b)),
            pl.BlockSpec((seq, n), lambda b, k: (b, c_off + k)),
            pl.BlockSpec((DT_ROWS_PER_GROUP, seq), lambda b, k: (k, b)),
            pl.BlockSpec((ctx_len, gw), lambda b, k: (b, k)),
            pl.BlockSpec((n, ctx_len), lambda b, k: (k, b)),
            pl.BlockSpec((DT_ROWS_PER_GROUP, ctx_len), lambda b, k: (k, b)),
            _const_spec(sel_w.shape),
            _const_spec(sel_e.shape),
            _const_spec(sel_c.shape),
        ],
        out_specs=pl.BlockSpec((seq, gw), lambda b, k: (b, k)),
        out_shape=jax.ShapeDtypeStruct((batch * seq, g * gw), BF16),
        scratch_shapes=[
            pltpu.VMEM((n_all * SSM_CHUNK, SSM_CHUNK), F32),
            pltpu.VMEM((4 * HEADS_PER_GROUP * n_lat, SSM_CHUNK), F32),
            pltpu.VMEM((n_all * SSM_CHUNK, 2 * gw), BF16),
            pltpu.VMEM((seq, 2 * gw), F32),
            pltpu.VMEM((seq, 2 * HEADS_PER_GROUP * SSM_CHUNK), F32),
            pltpu.VMEM((n_lat, SSM_CHUNK, SSM_CHUNK), F32),
            pltpu.VMEM((n_all, n, 2 * gw), F32),
            pltpu.VMEM((n_lat, n, 2 * gw), BF16),
            pltpu.VMEM((2, n_all, 1, gw), F32),
        ],
        compiler_params=_cparams(2),
        name="ssd_scan",
    )(xc_l, bt_l, xc_l, dt_l, xc_c, bt_c, dt_c, sel_w, sel_e, sel_c)


def _mlp_tail(x1, g_ref, sh_ref, sc_ref, gate_ref, w1_ref, w2_ref, nff):
    h2 = _norm_mod(x1, g_ref[...], sh_ref[...], sc_ref[...]).astype(BF16)
    dff = w1_ref.shape[1]
    acc = None
    for j in range(dff // nff):
        c0 = j * nff
        a = jnp.dot(h2, w1_ref[:, c0:c0 + nff], preferred_element_type=F32)
        a = jnp.square(jnp.maximum(a, 0.0)).astype(BF16)
        p = jnp.dot(a, w2_ref[c0:c0 + nff, :], preferred_element_type=F32)
        acc = p if acc is None else acc + p
    return x1 + gate_ref[...] * acc


def _ssd_out_kernel(y_ref, xs_ref, z_ref, x_ref, dsk_ref, sng_ref, gm_ref, shf_ref, scf_ref, gf_ref,
                    ng_ref, wo_ref, w1_ref, w2_ref, o_ref):
    acc = None
    for g0 in range(0, y_ref.shape[1], GROUP_WIDTH):
        cols = slice(g0, g0 + GROUP_WIDTH)
        y = y_ref[:, cols].astype(F32) + dsk_ref[:, cols] * xs_ref[:, cols].astype(F32)
        y = y * _silu(z_ref[:, cols].astype(F32))
        y = y * lax.rsqrt(jnp.mean(y * y, axis=-1, keepdims=True) + EPS)
        yn = (y * sng_ref[:, cols]).astype(BF16)
        p = jnp.dot(yn, wo_ref[cols, :], preferred_element_type=F32)
        acc = p if acc is None else acc + p
    x1 = x_ref[...] + gm_ref[...] * acc
    o_ref[...] = _mlp_tail(x1, ng_ref, shf_ref, scf_ref, gf_ref, w1_ref, w2_ref, 1024)


def _mod_spec(d, row_of_tile, k):
    return pl.BlockSpec((None, 1, d), lambda i: (row_of_tile(i), 0, k))


def _ssd_out_mlp(y2d, xc2d, z2d, x2d, dskip_row, ssd_ng_row, mod3, row_of_tile, norm_g, wo, w1, w2, *, tm):
    rows, d = x2d.shape
    di = y2d.shape[1]
    dff = w1.shape[1]
    return pl.pallas_call(
        _ssd_out_kernel,
        grid=(rows // tm,),
        in_specs=[
            pl.BlockSpec((tm, di), lambda i: (i, 0)),
            pl.BlockSpec((tm, di), lambda i: (i, 0)),
            pl.BlockSpec((tm, di), lambda i: (i, 0)),
            pl.BlockSpec((tm, d), lambda i: (i, 0)),
            _const_spec((1, di)),
            _const_spec((1, di)),
            _mod_spec(d, row_of_tile, 2),
            _mod_spec(d, row_of_tile, 3),
            _mod_spec(d, row_of_tile, 4),
            _mod_spec(d, row_of_tile, 5),
            _const_spec((1, d)),
            _const_spec((di, d)),
            _const_spec((d, dff)),
            _const_spec((dff, d)),
        ],
        out_specs=pl.BlockSpec((tm, d), lambda i: (i, 0)),
        out_shape=jax.ShapeDtypeStruct((rows, d), F32),
        compiler_params=_cparams(1),
        name="ssd_out_mlp",
    )(y2d, xc2d, z2d, x2d, dskip_row, ssd_ng_row, mod3, mod3, mod3, mod3, norm_g, wo, w1, w2)


def _sc_layer_kernel(x_ref, shm_ref, scm_ref, gm_ref, shf_ref, scf_ref, gf_ref, ngm_ref, ngf_ref,
                     fg_ref, wi_ref, cw_ref, cwm_ref, wo_ref, w1_ref, w2_ref, o_ref, *, period):
    x = x_ref[...]
    h = _norm_mod(x, ngm_ref[...], shm_ref[...], scm_ref[...]).astype(BF16)
    w = wo_ref.shape[0]
    bg = jnp.dot(h, wi_ref[:, 0:w], preferred_element_type=F32)
    cg = jnp.dot(h, wi_ref[:, w:2 * w], preferred_element_type=F32)
    xv = jnp.dot(h, wi_ref[:, 2 * w:3 * w], preferred_element_type=F32)
    u = (bg * _row_conv3(cg * xv, cw_ref, cwm_ref, slice(0, w), period)).astype(BF16)
    y = jnp.dot(u, wo_ref[...], preferred_element_type=F32)
    x1 = x + gm_ref[...] * y
    x2 = _mlp_tail(x1, ngf_ref, shf_ref, scf_ref, gf_ref, w1_ref, w2_ref, 1024)
    ms = jnp.mean(x2 * x2, axis=-1, keepdims=True)
    o_ref[...] = x2 * lax.rsqrt(ms + EPS) * fg_ref[...]


def _sc_layer(x2d, mod3, row_of_tile, ng_mix, ng_mlp, final_g, wi, conv_w, wo, w1, w2, *, tm, period):
    rows, d = x2d.shape
    dff = w1.shape[1]
    kern = functools.partial(_sc_layer_kernel, period=period)
    return pl.pallas_call(
        kern,
        grid=(rows // tm,),
        in_specs=[
            pl.BlockSpec((tm, d), lambda i: (i, 0)),
            _mod_spec(d, row_of_tile, 0),
            _mod_spec(d, row_of_tile, 1),
            _mod_spec(d, row_of_tile, 2),
            _mod_spec(d, row_of_tile, 3),
            _mod_spec(d, row_of_tile, 4),
            _mod_spec(d, row_of_tile, 5),
            _const_spec((1, d)),
            _const_spec((1, d)),
            _const_spec((1, d)),
            _const_spec(wi.shape),
            _const_spec(conv_w.shape),
            _const_spec((2 * period, conv_w.shape[1])),
            _const_spec(wo.shape),
            _const_spec((d, dff)),
            _const_spec((dff, d)),
        ],
        out_specs=pl.BlockSpec((tm, d), lambda i: (i, 0)),
        out_shape=jax.ShapeDtypeStruct((rows, d), F32),
        compiler_params=_cparams(1),
        name="shortconv_layer",
    )(x2d, mod3, mod3, mod3, mod3, mod3, mod3, ng_mix, ng_mlp, final_g, wi, conv_w,
      _edge_masked_taps(conv_w, period), wo, w1, w2)


def kernel(x, c, ctx, c_ctx, ada_w, ada_b, norm_mix_g, norm_mlp_g, ssd_w_in, ssd_conv_w, ssd_conv_b,
           ssd_dt_bias, ssd_a_log, ssd_d, ssd_norm_g, ssd_w_out, sc_w_in, sc_conv_w, sc_w_out,
           mlp_w1, mlp_w2, final_norm_g):
    batch, seq, d = x.shape
    ctx_len = ctx.shape[1]
    depth = ada_w.shape[0]
    assert depth == 2 and ssd_w_in.shape[0] == 1 and sc_w_in.shape[0] == 1
    d_inner = ssd_w_out.shape[1]
    n_heads = ssd_d.shape[1]
    xbc_dim = ssd_conv_w.shape[2]
    assert n_heads == SSM_GROUPS * HEADS_PER_GROUP and d_inner == SSM_GROUPS * GROUP_WIDTH

    mod_rows = 16
    cvec = jnp.zeros((mod_rows, d), F32).at[:batch].set(c).at[batch].set(c_ctx)
    mod = _modulation(cvec, ada_w, ada_b)
    mod0 = mod[0].reshape(mod_rows, 1, 6 * d)
    mod1 = mod[1].reshape(mod_rows, 1, 6 * d)

    w_in = ssd_w_in[0]
    wz = w_in[:, :d_inner].astype(BF16)
    nbc = SSM_GROUPS * SSM_STATE
    xs = slice(d_inner, 2 * d_inner)
    bs = slice(2 * d_inner, 2 * d_inner + nbc)
    cs_ = slice(2 * d_inner + nbc, 2 * d_inner + 2 * nbc)
    wxc = jnp.concatenate([w_in[:, xs], w_in[:, cs_]], axis=1).astype(BF16)
    wbt = w_in[:, bs].T.astype(BF16)
    cw = ssd_conv_w[0]
    cbias = ssd_conv_b[0]
    cw_xc = jnp.concatenate([cw[:, :d_inner], cw[:, d_inner + nbc:]], axis=1)
    cb_xc = jnp.concatenate([cbias[:d_inner], cbias[d_inner + nbc:]]).reshape(1, -1)
    cwb = jnp.concatenate([cw[:, d_inner:d_inner + nbc].T, cbias[d_inner:d_inner + nbc, None],
                           jnp.zeros((nbc, 4), F32)], axis=1)
    gi = jnp.arange(SSM_GROUPS)[:, None, None]
    di_ = jnp.arange(2)[None, :, None]
    ri = jnp.arange(HEADS_PER_GROUP)[None, None, :]
    flat = (di_ * n_heads + gi * HEADS_PER_GROUP + ri).reshape(SSM_GROUPS, 2 * HEADS_PER_GROUP)
    dt_idx = jnp.concatenate([flat, flat], axis=1).reshape(-1)
    wdt_t = w_in[:, d_inner + xbc_dim:][:, dt_idx].T.astype(BF16)
    dtb_col = ssd_dt_bias[0].reshape(-1)[dt_idx].reshape(-1, 1).astype(F32)
    alog_col = ssd_a_log[0].reshape(-1)[dt_idx].reshape(-1, 1).astype(F32)
    ng_mix0 = norm_mix_g[0].reshape(1, d)

    tm = 512
    x2d = x.reshape(batch * seq, d)
    ctx2d = ctx.reshape(batch * ctx_len, d)
    lat_row = lambda i: (i * tm) // seq
    tm_in = 256
    z_l, xc_l, bt_l, dt_l = _ssd_in_proj(x2d, mod0, lambda i: (i * tm_in) // seq, ng_mix0, wz, wxc, wbt,
                                         wdt_t, cw_xc, cb_xc, cwb, dtb_col, alog_col, tm=tm_in,
                                         period=GRID_W, want_z=True)
    xc_c, bt_c, dt_c = _ssd_in_proj(ctx2d, mod0, lambda i: batch, ng_mix0, wz, wxc, wbt, wdt_t, cw_xc,
                                    cb_xc, cwb, dtb_col, alog_col, tm=ctx_len, period=ctx_len,
                                    want_z=False)

    dskip_row = jnp.repeat(ssd_d[0].astype(F32), SSM_HEAD_DIM).reshape(1, d_inner)
    ng_row = ssd_norm_g[0].reshape(1, d_inner)
    y_ssd = _ssd_scan(xc_l, bt_l, dt_l, xc_c, bt_c, dt_c, batch=batch, seq=seq, ctx_len=ctx_len)

    x1 = _ssd_out_mlp(y_ssd, xc_l, z_l, x2d, dskip_row, ng_row, mod0, lat_row, norm_mlp_g[0].reshape(1, d),
                      ssd_w_out[0].astype(BF16), mlp_w1[0].astype(BF16), mlp_w2[0].astype(BF16), tm=tm)

    out = _sc_layer(x1, mod1, lat_row, norm_mix_g[1].reshape(1, d), norm_mlp_g[1].reshape(1, d),
                    final_norm_g.reshape(1, d), sc_w_in[0].astype(BF16), sc_conv_w[0],
                    sc_w_out[0].astype(BF16), mlp_w1[1].astype(BF16), mlp_w2[1].astype(BF16),
                    tm=tm, period=GRID_W)
    return out.reshape(batch, seq, d)
```

```python
import functools

import numpy as np

import jax
import jax.numpy as jnp
from jax import lax
from jax.experimental import pallas as pl
from jax.experimental.pallas import tpu as pltpu

F32 = jnp.float32
BF16 = jnp.bfloat16

EPS = 1e-6
GRID_W = 64
SSM_HEAD_DIM = 64
SSM_GROUPS = 8
HEADS_PER_GROUP = 4
SSM_STATE = 128
SSM_CHUNK = 128
GROUP_WIDTH = HEADS_PER_GROUP * SSM_HEAD_DIM
DT_ROWS_PER_GROUP = 16
NEG_BIG = -1e30

VMEM_LIMIT_BYTES = 56 * 1024 * 1024


def _cparams(n_axes):
    return pltpu.CompilerParams(
        dimension_semantics=("arbitrary",) * n_axes,
        vmem_limit_bytes=VMEM_LIMIT_BYTES,
    )


def _const_spec(shape):
    nd = len(shape)
    return pl.BlockSpec(shape, lambda *_: (0,) * nd, pipeline_mode=pl.Buffered(1))


def _silu(u):
    return u * (1.0 / (1.0 + jnp.exp(-u)))


def _norm_mod(x, g, shift, scale):
    ms = jnp.mean(x * x, axis=-1, keepdims=True)
    y = x * lax.rsqrt(ms + EPS) * g
    return y * (1.0 + scale) + shift


def _edge_masked_taps(conv_w, period):
    r = jnp.arange(period)[:, None]
    return jnp.concatenate([jnp.where(r != 0, conv_w[0][None, :], 0.0),
                            jnp.where(r != period - 1, conv_w[2][None, :], 0.0)], axis=0)


def _row_conv3(u, w_ref, wm_ref, cols, period):
    rows = u.shape[0]
    reps = rows // period
    w_prev = jnp.concatenate([wm_ref[0:period, cols]] * reps, axis=0)
    w_next = jnp.concatenate([wm_ref[period:2 * period, cols]] * reps, axis=0)
    return pltpu.roll(u, 1, 0) * w_prev + u * w_ref[1:2, cols] + pltpu.roll(u, rows - 1, 0) * w_next


def _mod_kernel(c_ref, w_ref, b_ref, o_ref):
    s = _silu(c_ref[...]).astype(BF16)
    o_ref[...] = jnp.dot(s, w_ref[...].astype(BF16), preferred_element_type=F32) + b_ref[...]


def _modulation(cvec, ada_w, ada_b):
    depth, d, n = ada_w.shape
    rows = cvec.shape[0]
    tn = 1536
    return pl.pallas_call(
        _mod_kernel,
        grid=(depth, n // tn),
        in_specs=[
            pl.BlockSpec((rows, d), lambda i, j: (0, 0)),
            pl.BlockSpec((None, d, tn), lambda i, j: (i, 0, j)),
            pl.BlockSpec((None, 1, tn), lambda i, j: (i, 0, j)),
        ],
        out_specs=pl.BlockSpec((None, rows, tn), lambda i, j: (i, 0, j)),
        out_shape=jax.ShapeDtypeStruct((depth, rows, n), F32),
        compiler_params=_cparams(2),
        name="adaln_mod",
    )(cvec, ada_w, ada_b.reshape(depth, 1, n))


_NT_DIMS = (((1,), (1,)), ((), ()))


def _ssd_in_kernel(x_ref, sh_ref, sc_ref, g_ref, wz_ref, wxc_ref, wbt_ref, wdt_ref, cw_ref, cwm_ref, cb_ref,
                   cwb_ref, dtb_ref, alog_ref, *out_refs, period, ncol, nrow, want_z):
    if want_z:
        z_ref, xc_ref, bt_ref, dt_ref = out_refs
    else:
        xc_ref, bt_ref, dt_ref = out_refs
    h = _norm_mod(x_ref[...], g_ref[...], sh_ref[...], sc_ref[...]).astype(BF16)
    if want_z:
        z_ref[...] = jnp.dot(h, wz_ref[...], preferred_element_type=F32).astype(BF16)
    for j in range(xc_ref.shape[1] // ncol):
        cols = slice(j * ncol, (j + 1) * ncol)
        u = jnp.dot(h, wxc_ref[:, cols], preferred_element_type=F32)
        u = _row_conv3(u, cw_ref, cwm_ref, cols, period) + cb_ref[:, cols]
        xc_ref[:, cols] = _silu(u).astype(BF16)
    tm = x_ref.shape[0]
    pos = jnp.bitwise_and(lax.broadcasted_iota(jnp.int32, (1, tm), 1), period - 1)
    for j in range(bt_ref.shape[0] // nrow):
        rows = slice(j * nrow, (j + 1) * nrow)
        u = lax.dot_general(wbt_ref[rows, :], h, _NT_DIMS, preferred_element_type=F32)
        prev = jnp.where(pos != 0, pltpu.roll(u, 1, 1), 0.0)
        nxt = jnp.where(pos != period - 1, pltpu.roll(u, tm - 1, 1), 0.0)
        cwb = cwb_ref[rows, :]
        u = prev * cwb[:, 0:1] + u * cwb[:, 1:2] + nxt * cwb[:, 2:3] + cwb[:, 3:4]
        bt_ref[rows, :] = _silu(u).astype(BF16)
    raw = lax.dot_general(wdt_ref[...], h, _NT_DIMS, preferred_element_type=F32)
    v = raw + dtb_ref[...]
    sp = jnp.maximum(v, 0.0) + jnp.log1p(jnp.exp(-jnp.abs(v)))
    row = lax.broadcasted_iota(jnp.int32, (dt_ref.shape[0], 1), 0)
    is_la = jnp.bitwise_and(row, DT_ROWS_PER_GROUP - 1) >= DT_ROWS_PER_GROUP // 2
    dt_ref[...] = jnp.where(is_la, sp * (-jnp.exp(alog_ref[...])), sp)


def _ssd_in_proj(x2d, mod3, mod_row_of_tile, norm_g, wz, wxc, wbt, wdt_t, cw_xc, cb_xc, cwb, dtb_col,
                 alog_col, *, tm, period, want_z):
    rows, d = x2d.shape
    dz = wz.shape[1]
    dxc = wxc.shape[1]
    nb = wbt.shape[0]
    ndt = wdt_t.shape[0]
    ncol, nrow = 512, 256
    kern = functools.partial(_ssd_in_kernel, period=period, ncol=ncol, nrow=nrow, want_z=want_z)
    out_specs = [
        pl.BlockSpec((tm, dxc), lambda i: (i, 0)),
        pl.BlockSpec((nb, tm), lambda i: (0, i)),
        pl.BlockSpec((ndt, tm), lambda i: (0, i)),
    ]
    out_shape = [
        jax.ShapeDtypeStruct((rows, dxc), BF16),
        jax.ShapeDtypeStruct((nb, rows), BF16),
        jax.ShapeDtypeStruct((ndt, rows), F32),
    ]
    if want_z:
        out_specs.insert(0, pl.BlockSpec((tm, dz), lambda i: (i, 0)))
        out_shape.insert(0, jax.ShapeDtypeStruct((rows, dz), BF16))
    return pl.pallas_call(
        kern,
        grid=(rows // tm,),
        in_specs=[
            pl.BlockSpec((tm, d), lambda i: (i, 0)),
            pl.BlockSpec((None, 1, d), lambda i: (mod_row_of_tile(i), 0, 0)),
            pl.BlockSpec((None, 1, d), lambda i: (mod_row_of_tile(i), 0, 1)),
            _const_spec((1, d)),
            _const_spec((d, dz)),
            _const_spec((d, dxc)),
            _const_spec((nb, d)),
            _const_spec((ndt, d)),
            _const_spec((3, dxc)),
            _const_spec((2 * period, dxc)),
            _const_spec((1, dxc)),
            _const_spec(cwb.shape),
            _const_spec((ndt, 1)),
            _const_spec((ndt, 1)),
        ],
        out_specs=out_specs,
        out_shape=out_shape,
        compiler_params=_cparams(1),
        name="ssd_in_proj",
    )(x2d, mod3, mod3, norm_g, wz, wxc, wbt, wdt_t, cw_xc, _edge_masked_taps(cw_xc, period), cb_xc, cwb,
      dtb_col, alog_col)


COL_CS, COL_ECS, COL_W = 0, 8, 16


def _head_expand(colmat, lane0):
    r = colmat.shape[0]
    first = lax.broadcasted_iota(jnp.int32, (r, SSM_STATE), 1) < SSM_HEAD_DIM
    cols = [colmat[:, lane0 + hd:lane0 + hd + 1] for hd in range(HEADS_PER_GROUP)]
    lo = jnp.where(first, cols[0], cols[1])
    hi = jnp.where(first, cols[2], cols[3])
    return jnp.concatenate([lo, hi], axis=1)


def _expand_select(lane0):
    r = np.arange(2 * SSM_STATE)[:, None] % SSM_STATE
    l = np.arange(2 * GROUP_WIDTH)[None, :]
    src = lane0 + (l // GROUP_WIDTH) * HEADS_PER_GROUP + (l % GROUP_WIDTH) // SSM_HEAD_DIM
    return jnp.asarray(r == src, dtype=BF16)


def _broadcast_select(lane0, count):
    r = np.arange(2 * SSM_STATE)[:, None] % SSM_STATE
    l = np.arange(SSM_STATE * count)[None, :]
    return jnp.asarray(r == lane0 + l // SSM_STATE, dtype=BF16)


def _hi_lo(colmat):
    hi = colmat.astype(BF16)
    lo = (colmat - hi.astype(F32)).astype(BF16)
    return jnp.concatenate([hi, lo], axis=1)


def _decay_rows(dt_ref, n_chunks, upper, lower):
    q = SSM_CHUNK
    nh = HEADS_PER_GROUP
    dt = jnp.concatenate([dt_ref[0:2 * nh, c * q:(c + 1) * q] for c in range(n_chunks)], axis=0)
    la = jnp.concatenate([dt_ref[2 * nh:4 * nh, c * q:(c + 1) * q] for c in range(n_chunks)], axis=0)
    csf = jnp.dot(la, upper, preferred_element_type=F32, precision=lax.Precision.HIGHEST)
    csb = jnp.dot(la, lower, preferred_element_type=F32, precision=lax.Precision.HIGHEST)
    rows = dt.shape[0]
    row = lax.broadcasted_iota(jnp.int32, (rows, 1), 0)
    is_b = jnp.bitwise_and(row, nh) != 0
    cs = jnp.where(is_b, csb, csf)
    tot = jnp.where(is_b, csb[:, 0:1], csf[:, q - 1:q])
    other = jnp.where(is_b, pltpu.roll(dt, nh, 0), pltpu.roll(dt, rows - nh, 0))
    return cs, jnp.exp(cs), jnp.exp(tot - cs) * dt, cs - jnp.log(dt), jnp.log(dt + other)


def _ssd_scan_kernel(xl_ref, btl_ref, cl_ref, dtl_ref, xc_ref, btc_ref, dtc_ref,
                     selw_ref, sele_ref, selc_ref, o_ref,
                     cols_ref, rows_ref, xw_ref, ee_ref, csb_ref, cb_ref, sloc_ref, hin_ref, etot_ref):
    q = SSM_CHUNK
    nh = HEADS_PER_GROUP
    gw = GROUP_WIDTH
    n_lat = xl_ref.shape[0] // q
    n_ctx = xc_ref.shape[0] // q
    ctx_rows = n_ctx * q
    ki = lax.broadcasted_iota(jnp.int32, (q, q), 0)
    kj = lax.broadcasted_iota(jnp.int32, (q, q), 1)

    upper = (ki <= kj).astype(F32)
    lower = (ki >= kj).astype(F32)
    pad = jnp.zeros((q - 6 * nh, q), F32)
    for dref, n_chunks, base in ((dtc_ref, n_ctx, 0), (dtl_ref, n_lat, n_ctx)):
        cs, ecs, w, rk, lds = _decay_rows(dref, n_chunks, upper, lower)
        for c in range(n_chunks):
            r = slice(2 * nh * c, 2 * nh * (c + 1))
            colmat = jnp.concatenate([cs[r], ecs[r], w[r], pad], axis=0).T
            cols_ref[(base + c) * q:(base + c + 1) * q, :] = colmat
            for d in range(2):
                p0 = 0 if d else q - 1
                etot_ref[d, base + c] = _head_expand(colmat[p0:p0 + 1, :], COL_ECS + nh * d)
        if base:
            rows_ref[0:2 * nh * n_lat, :] = rk
            rows_ref[2 * nh * n_lat:4 * nh * n_lat, :] = lds

    def expand(hl, sel_ref):
        return jnp.dot(hl, sel_ref[...], preferred_element_type=F32)

    xf = xc_ref[...].astype(F32)
    hl = _hi_lo(cols_ref[0:ctx_rows, :])
    xw_ref[0:ctx_rows, :] = (jnp.concatenate([xf, xf], axis=1) * expand(hl, selw_ref)).astype(BF16)
    blk = 4 * q
    for j in range(n_lat * q // blk):
        rows = slice(j * blk, (j + 1) * blk)
        xf = xl_ref[rows, :].astype(F32)
        r0 = ctx_rows + j * blk
        hl = _hi_lo(cols_ref[r0:r0 + blk, :])
        xw_ref[r0:r0 + blk, :] = (jnp.concatenate([xf, xf], axis=1) * expand(hl, selw_ref)).astype(BF16)
        ee_ref[rows, :] = expand(hl, sele_ref)
        csb_ref[rows, :] = expand(hl, selc_ref)

    for c in range(n_ctx):
        sloc_ref[c] = jnp.dot(btc_ref[:, c * q:(c + 1) * q], xw_ref[c * q:(c + 1) * q, :],
                              preferred_element_type=F32)

    def s_body(c, carry):
        r0 = pl.multiple_of(c * q, q)
        bt = btl_ref[:, pl.ds(r0, q)]
        sloc_ref[n_ctx + c] = jnp.dot(bt, xw_ref[pl.ds(ctx_rows + r0, q), :], preferred_element_type=F32)
        cb_ref[c] = jnp.dot(cl_ref[pl.ds(r0, q), :], bt, preferred_element_type=F32)
        return carry

    lax.fori_loop(0, n_lat, s_body, 0, unroll=4)

    for d in range(2):
        lanes = slice(gw * d, gw * (d + 1))
        h = jnp.zeros((SSM_STATE, gw), F32)
        for c in (range(n_ctx - 1, -1, -1) if d else range(n_ctx)):
            h = h * etot_ref[d, c] + sloc_ref[c, :, lanes]

        def b_body(i, h, d=d, lanes=lanes):
            c = (n_lat - 1 - i) if d else i
            hin_ref[c, :, lanes] = h.astype(BF16)
            return h * etot_ref[d, n_ctx + c] + sloc_ref[n_ctx + c, :, lanes]

        lax.fori_loop(0, n_lat, b_body, h, unroll=4)

    lane_head = lax.broadcasted_iota(jnp.int32, (q, gw), 1) // SSM_HEAD_DIM
    below = ki > kj
    above = ki < kj

    def c_body(c, carry):
        r0 = pl.multiple_of(c * q, q)
        x = xl_ref[pl.ds(r0, q), :]
        r8 = pl.multiple_of(c * 2 * nh, 2 * nh)
        rk8 = rows_ref[pl.ds(r8, 2 * nh), :]
        lds8 = rows_ref[pl.ds(2 * nh * n_lat + r8, 2 * nh), :]
        cb = cb_ref[c]
        colmat = cols_ref[pl.ds(ctx_rows + r0, q), :]
        m_parts = []
        x_parts = []
        for hd in range(nh):
            segf = csb_ref[pl.ds(r0, q), q * hd:q * (hd + 1)] - rk8[hd:hd + 1, :]
            segb = colmat[:, COL_CS + nh + hd:COL_CS + nh + hd + 1] - rk8[nh + hd:nh + hd + 1, :]
            arg = jnp.where(below, segf, jnp.where(above, segb, lds8[hd:hd + 1, :]))
            m_parts.append((jnp.exp(arg) * cb).astype(BF16))
            x_parts.append(jnp.where(lane_head == hd, x, jnp.zeros_like(x)))
        m_all = jnp.concatenate(m_parts, axis=1)
        x_bd = jnp.concatenate(x_parts, axis=0)
        y = jnp.dot(m_all, x_bd, preferred_element_type=F32)
        y_off = ee_ref[pl.ds(r0, q), :] * jnp.dot(cl_ref[pl.ds(r0, q), :], hin_ref[c],
                                                  preferred_element_type=F32)
        o_ref[pl.ds(r0, q), :] = (y + y_off[:, 0:gw] + y_off[:, gw:2 * gw]).astype(BF16)
        return carry

    lax.fori_loop(0, n_lat, c_body, 0, unroll=4)


def _ssd_scan(xc_l, bt_l, dt_l, xc_c, bt_c, dt_c, *, batch, seq, ctx_len):
    g = SSM_GROUPS
    gw = GROUP_WIDTH
    n = SSM_STATE
    c_off = (g * gw) // n
    n_lat = seq // SSM_CHUNK
    n_all = n_lat + ctx_len // SSM_CHUNK
    sel_w = _expand_select(COL_W)
    sel_e = _expand_select(COL_ECS)
    sel_c = _broadcast_select(COL_CS, HEADS_PER_GROUP)
    return pl.pallas_call(
        _ssd_scan_kernel,
        grid=(batch, g),
        in_specs=[
            pl.BlockSpec((seq, gw), lambda b, k: (b, k)),
            pl.BlockSpec((n, seq), lambda b, k: (k, b)),
            pl.BlockSpec((seq, n), lambda b, k: (b, c_off + k)),
            pl.BlockSpec((DT_ROWS_PER_GROUP, seq), lambda b, k: (k, b)),
            pl.BlockSpec((ctx_len, gw), lambda b, k: (b, k)),
            pl.BlockSpec((n, ctx_len), lambda b, k: (k, b)),
            pl.BlockSpec((DT_ROWS_PER_GROUP, ctx_len), lambda b, k: (k, b)),
            _const_spec(sel_w.shape),
            _const_spec(sel_e.shape),
            _const_spec(sel_c.shape),
        ],
        out_specs=pl.BlockSpec((seq, gw), lambda b, k: (b, k)),
        out_shape=jax.ShapeDtypeStruct((batch * seq, g * gw), BF16),
        scratch_shapes=[
            pltpu.VMEM((n_all * SSM_CHUNK, SSM_CHUNK), F32),
            pltpu.VMEM((4 * HEADS_PER_GROUP * n_lat, SSM_CHUNK), F32),
            pltpu.VMEM((n_all * SSM_CHUNK, 2 * gw), BF16),
            pltpu.VMEM((seq, 2 * gw), F32),
            pltpu.VMEM((seq, HEADS_PER_GROUP * SSM_CHUNK), F32),
            pltpu.VMEM((n_lat, SSM_CHUNK, SSM_CHUNK), F32),
            pltpu.VMEM((n_all, n, 2 * gw), F32),
            pltpu.VMEM((n_lat, n, 2 * gw), BF16),
            pltpu.VMEM((2, n_all, 1, gw), F32),
        ],
        compiler_params=_cparams(2),
        name="ssd_scan",
    )(xc_l, bt_l, xc_l, dt_l, xc_c, bt_c, dt_c, sel_w, sel_e, sel_c)


def _mlp_tail(x1, g_ref, sh_ref, sc_ref, gate_ref, w1_ref, w2_ref, nff):
    h2 = _norm_mod(x1, g_ref[...], sh_ref[...], sc_ref[...]).astype(BF16)
    dff = w1_ref.shape[1]
    acc = None
    for j in range(dff // nff):
        c0 = j * nff
        a = jnp.dot(h2, w1_ref[:, c0:c0 + nff], preferred_element_type=F32)
        a = jnp.square(jnp.maximum(a, 0.0)).astype(BF16)
        p = jnp.dot(a, w2_ref[c0:c0 + nff, :], preferred_element_type=F32)
        acc = p if acc is None else acc + p
    return x1 + gate_ref[...] * acc


def _ssd_out_kernel(y_ref, xs_ref, z_ref, x_ref, dsk_ref, sng_ref, gm_ref, shf_ref, scf_ref, gf_ref,
                    ng_ref, wo_ref, w1_ref, w2_ref, o_ref):
    acc = None
    for g0 in range(0, y_ref.shape[1], GROUP_WIDTH):
        cols = slice(g0, g0 + GROUP_WIDTH)
        y = y_ref[:, cols].astype(F32) + dsk_ref[:, cols] * xs_ref[:, cols].astype(F32)
        y = y * _silu(z_ref[:, cols].astype(F32))
        y = y * lax.rsqrt(jnp.mean(y * y, axis=-1, keepdims=True) + EPS)
        yn = (y * sng_ref[:, cols]).astype(BF16)
        p = jnp.dot(yn, wo_ref[cols, :], preferred_element_type=F32)
        acc = p if acc is None else acc + p
    x1 = x_ref[...] + gm_ref[...] * acc
    o_ref[...] = _mlp_tail(x1, ng_ref, shf_ref, scf_ref, gf_ref, w1_ref, w2_ref, 1024)


def _mod_spec(d, row_of_tile, k):
    return pl.BlockSpec((None, 1, d), lambda i: (row_of_tile(i), 0, k))


def _ssd_out_mlp(y2d, xc2d, z2d, x2d, dskip_row, ssd_ng_row, mod3, row_of_tile, norm_g, wo, w1, w2, *, tm):
    rows, d = x2d.shape
    di = y2d.shape[1]
    dff = w1.shape[1]
    return pl.pallas_call(
        _ssd_out_kernel,
        grid=(rows // tm,),
        in_specs=[
            pl.BlockSpec((tm, di), lambda i: (i, 0)),
            pl.BlockSpec((tm, di), lambda i: (i, 0)),
            pl.BlockSpec((tm, di), lambda i: (i, 0)),
            pl.BlockSpec((tm, d), lambda i: (i, 0)),
            _const_spec((1, di)),
            _const_spec((1, di)),
            _mod_spec(d, row_of_tile, 2),
            _mod_spec(d, row_of_tile, 3),
            _mod_spec(d, row_of_tile, 4),
            _mod_spec(d, row_of_tile, 5),
            _const_spec((1, d)),
            _const_spec((di, d)),
            _const_spec((d, dff)),
            _const_spec((dff, d)),
        ],
        out_specs=pl.BlockSpec((tm, d), lambda i: (i, 0)),
        out_shape=jax.ShapeDtypeStruct((rows, d), F32),
        compiler_params=_cparams(1),
        name="ssd_out_mlp",
    )(y2d, xc2d, z2d, x2d, dskip_row, ssd_ng_row, mod3, mod3, mod3, mod3, norm_g, wo, w1, w2)


def _sc_layer_kernel(x_ref, shm_ref, scm_ref, gm_ref, shf_ref, scf_ref, gf_ref, ngm_ref, ngf_ref,
                     fg_ref, wi_ref, cw_ref, cwm_ref, wo_ref, w1_ref, w2_ref, o_ref, *, period):
    x = x_ref[...]
    h = _norm_mod(x, ngm_ref[...], shm_ref[...], scm_ref[...]).astype(BF16)
    w = wo_ref.shape[0]
    bg = jnp.dot(h, wi_ref[:, 0:w], preferred_element_type=F32)
    cg = jnp.dot(h, wi_ref[:, w:2 * w], preferred_element_type=F32)
    xv = jnp.dot(h, wi_ref[:, 2 * w:3 * w], preferred_element_type=F32)
    u = (bg * _row_conv3(cg * xv, cw_ref, cwm_ref, slice(0, w), period)).astype(BF16)
    y = jnp.dot(u, wo_ref[...], preferred_element_type=F32)
    x1 = x + gm_ref[...] * y
    x2 = _mlp_tail(x1, ngf_ref, shf_ref, scf_ref, gf_ref, w1_ref, w2_ref, 1024)
    ms = jnp.mean(x2 * x2, axis=-1, keepdims=True)
    o_ref[...] = x2 * lax.rsqrt(ms + EPS) * fg_ref[...]


def _sc_layer(x2d, mod3, row_of_tile, ng_mix, ng_mlp, final_g, wi, conv_w, wo, w1, w2, *, tm, period):
    rows, d = x2d.shape
    dff = w1.shape[1]
    kern = functools.partial(_sc_layer_kernel, period=period)
    return pl.pallas_call(
        kern,
        grid=(rows // tm,),
        in_specs=[
            pl.BlockSpec((tm, d), lambda i: (i, 0)),
            _mod_spec(d, row_of_tile, 0),
            _mod_spec(d, row_of_tile, 1),
            _mod_spec(d, row_of_tile, 2),
            _mod_spec(d, row_of_tile, 3),
            _mod_spec(d, row_of_tile, 4),
            _mod_spec(d, row_of_tile, 5),
            _const_spec((1, d)),
            _const_spec((1, d)),
            _const_spec((1, d)),
            _const_spec(wi.shape),
            _const_spec(conv_w.shape),
            _const_spec((2 * period, conv_w.shape[1])),
            _const_spec(wo.shape),
            _const_spec((d, dff)),
            _const_spec((dff, d)),
        ],
        out_specs=pl.BlockSpec((tm, d), lambda i: (i, 0)),
        out_shape=jax.ShapeDtypeStruct((rows, d), F32),
        compiler_params=_cparams(1),
        name="shortconv_layer",
    )(x2d, mod3, mod3, mod3, mod3, mod3, mod3, ng_mix, ng_mlp, final_g, wi, conv_w,
      _edge_masked_taps(conv_w, period), wo, w1, w2)


def kernel(x, c, ctx, c_ctx, ada_w, ada_b, norm_mix_g, norm_mlp_g, ssd_w_in, ssd_conv_w, ssd_conv_b,
           ssd_dt_bias, ssd_a_log, ssd_d, ssd_norm_g, ssd_w_out, sc_w_in, sc_conv_w, sc_w_out,
           mlp_w1, mlp_w2, final_norm_g):
    batch, seq, d = x.shape
    ctx_len = ctx.shape[1]
    depth = ada_w.shape[0]
    assert depth == 2 and ssd_w_in.shape[0] == 1 and sc_w_in.shape[0] == 1
    d_inner = ssd_w_out.shape[1]
    n_heads = ssd_d.shape[1]
    xbc_dim = ssd_conv_w.shape[2]
    assert n_heads == SSM_GROUPS * HEADS_PER_GROUP and d_inner == SSM_GROUPS * GROUP_WIDTH

    mod_rows = 16
    cvec = jnp.zeros((mod_rows, d), F32).at[:batch].set(c).at[batch].set(c_ctx)
    mod = _modulation(cvec, ada_w, ada_b)
    mod0 = mod[0].reshape(mod_rows, 1, 6 * d)
    mod1 = mod[1].reshape(mod_rows, 1, 6 * d)

    w_in = ssd_w_in[0]
    wz = w_in[:, :d_inner].astype(BF16)
    nbc = SSM_GROUPS * SSM_STATE
    xs = slice(d_inner, 2 * d_inner)
    bs = slice(2 * d_inner, 2 * d_inner + nbc)
    cs_ = slice(2 * d_inner + nbc, 2 * d_inner + 2 * nbc)
    wxc = jnp.concatenate([w_in[:, xs], w_in[:, cs_]], axis=1).astype(BF16)
    wbt = w_in[:, bs].T.astype(BF16)
    cw = ssd_conv_w[0]
    cbias = ssd_conv_b[0]
    cw_xc = jnp.concatenate([cw[:, :d_inner], cw[:, d_inner + nbc:]], axis=1)
    cb_xc = jnp.concatenate([cbias[:d_inner], cbias[d_inner + nbc:]]).reshape(1, -1)
    cwb = jnp.concatenate([cw[:, d_inner:d_inner + nbc].T, cbias[d_inner:d_inner + nbc, None],
                           jnp.zeros((nbc, 4), F32)], axis=1)
    gi = jnp.arange(SSM_GROUPS)[:, None, None]
    di_ = jnp.arange(2)[None, :, None]
    ri = jnp.arange(HEADS_PER_GROUP)[None, None, :]
    flat = (di_ * n_heads + gi * HEADS_PER_GROUP + ri).reshape(SSM_GROUPS, 2 * HEADS_PER_GROUP)
    dt_idx = jnp.concatenate([flat, flat], axis=1).reshape(-1)
    wdt_t = w_in[:, d_inner + xbc_dim:][:, dt_idx].T.astype(BF16)
    dtb_col = ssd_dt_bias[0].reshape(-1)[dt_idx].reshape(-1, 1).astype(F32)
    alog_col = ssd_a_log[0].reshape(-1)[dt_idx].reshape(-1, 1).astype(F32)
    ng_mix0 = norm_mix_g[0].reshape(1, d)

    tm = 512
    x2d = x.reshape(batch * seq, d)
    ctx2d = ctx.reshape(batch * ctx_len, d)
    lat_row = lambda i: (i * tm) // seq
    tm_in = 256
    z_l, xc_l, bt_l, dt_l = _ssd_in_proj(x2d, mod0, lambda i: (i * tm_in) // seq, ng_mix0, wz, wxc, wbt,
                                         wdt_t, cw_xc, cb_xc, cwb, dtb_col, alog_col, tm=tm_in,
                                         period=GRID_W, want_z=True)
    xc_c, bt_c, dt_c = _ssd_in_proj(ctx2d, mod0, lambda i: batch, ng_mix0, wz, wxc, wbt, wdt_t, cw_xc,
                                    cb_xc, cwb, dtb_col, alog_col, tm=ctx_len, period=ctx_len,
                                    want_z=False)

    dskip_row = jnp.repeat(ssd_d[0].astype(F32), SSM_HEAD_DIM).reshape(1, d_inner)
    ng_row = ssd_norm_g[0].reshape(1, d_inner)
    y_ssd = _ssd_scan(xc_l, bt_l, dt_l, xc_c, bt_c, dt_c, batch=batch, seq=seq, ctx_len=ctx_len)

    x1 = _ssd_out_mlp(y_ssd, xc_l, z_l, x2d, dskip_row, ng_row, mod0, lat_row, norm_mlp_g[0].reshape(1, d),
                      ssd_w_out[0].astype(BF16), mlp_w1[0].astype(BF16), mlp_w2[0].astype(BF16), tm=tm)

    out = _sc_layer(x1, mod1, lat_row, norm_mix_g[1].reshape(1, d), norm_mlp_g[1].reshape(1, d),
                    final_norm_g.reshape(1, d), sc_w_in[0].astype(BF16), sc_conv_w[0],
                    sc_w_out[0].astype(BF16), mlp_w1[1].astype(BF16), mlp_w2[1].astype(BF16),
                    tm=tm, period=GRID_W)
    return out.reshape(batch, seq, d)
```

```python
import functools

import numpy as np

import jax
import jax.numpy as jnp
from jax import lax
from jax.experimental import pallas as pl
from jax.experimental.pallas import tpu as pltpu

F32 = jnp.float32
BF16 = jnp.bfloat16

EPS = 1e-6
GRID_W = 64
SSM_HEAD_DIM = 64
SSM_GROUPS = 8
HEADS_PER_GROUP = 4
SSM_STATE = 128
SSM_CHUNK = 128
GROUP_WIDTH = HEADS_PER_GROUP * SSM_HEAD_DIM
DT_ROWS_PER_GROUP = 16
NEG_BIG = -1e30

VMEM_LIMIT_BYTES = 56 * 1024 * 1024


def _cparams(n_axes):
    return pltpu.CompilerParams(
        dimension_semantics=("arbitrary",) * n_axes,
        vmem_limit_bytes=VMEM_LIMIT_BYTES,
    )


def _const_spec(shape):
    nd = len(shape)
    return pl.BlockSpec(shape, lambda *_: (0,) * nd, pipeline_mode=pl.Buffered(1))


def _silu(u):
    return u * (1.0 / (1.0 + jnp.exp(-u)))


def _norm_mod(x, g, shift, scale):
    ms = jnp.mean(x * x, axis=-1, keepdims=True)
    y = x * lax.rsqrt(ms + EPS) * g
    return y * (1.0 + scale) + shift


def _edge_masked_taps(conv_w, period):
    r = jnp.arange(period)[:, None]
    return jnp.concatenate([jnp.where(r != 0, conv_w[0][None, :], 0.0),
                            jnp.where(r != period - 1, conv_w[2][None, :], 0.0)], axis=0)


def _row_conv3(u, w_ref, wm_ref, cols, period):
    rows = u.shape[0]
    reps = rows // period
    w_prev = jnp.concatenate([wm_ref[0:period, cols]] * reps, axis=0)
    w_next = jnp.concatenate([wm_ref[period:2 * period, cols]] * reps, axis=0)
    return pltpu.roll(u, 1, 0) * w_prev + u * w_ref[1:2, cols] + pltpu.roll(u, rows - 1, 0) * w_next


def _mod_kernel(c_ref, w_ref, b_ref, o_ref):
    s = _silu(c_ref[...]).astype(BF16)
    o_ref[...] = jnp.dot(s, w_ref[...].astype(BF16), preferred_element_type=F32) + b_ref[...]


def _modulation(cvec, ada_w, ada_b):
    depth, d, n = ada_w.shape
    rows = cvec.shape[0]
    tn = 1536
    return pl.pallas_call(
        _mod_kernel,
        grid=(depth, n // tn),
        in_specs=[
            pl.BlockSpec((rows, d), lambda i, j: (0, 0)),
            pl.BlockSpec((None, d, tn), lambda i, j: (i, 0, j)),
            pl.BlockSpec((None, 1, tn), lambda i, j: (i, 0, j)),
        ],
        out_specs=pl.BlockSpec((None, rows, tn), lambda i, j: (i, 0, j)),
        out_shape=jax.ShapeDtypeStruct((depth, rows, n), F32),
        compiler_params=_cparams(2),
        name="adaln_mod",
    )(cvec, ada_w, ada_b.reshape(depth, 1, n))


_NT_DIMS = (((1,), (1,)), ((), ()))


def _ssd_in_kernel(x_ref, sh_ref, sc_ref, g_ref, w_ref, wbt_ref, wdt_ref, cw_ref, cwm_ref, cb_ref,
                   cwb_ref, dtb_ref, alog_ref, *out_refs, period, ncol, nrow, want_z, xc_starts):
    if want_z:
        z_ref, xc_ref, bt_ref, dt_ref = out_refs
    else:
        xc_ref, bt_ref, dt_ref = out_refs
    h = _norm_mod(x_ref[...], g_ref[...], sh_ref[...], sc_ref[...]).astype(BF16)
    if want_z:
        z_ref[...] = jnp.dot(h, w_ref[:, 0:z_ref.shape[1]], preferred_element_type=F32).astype(BF16)
    for j, w0 in enumerate(xc_starts):
        cols = slice(j * ncol, (j + 1) * ncol)
        u = jnp.dot(h, w_ref[:, w0:w0 + ncol], preferred_element_type=F32)
        u = _row_conv3(u, cw_ref, cwm_ref, cols, period) + cb_ref[:, cols]
        xc_ref[:, cols] = _silu(u).astype(BF16)
    tm = x_ref.shape[0]
    pos = jnp.bitwise_and(lax.broadcasted_iota(jnp.int32, (1, tm), 1), period - 1)
    for j in range(bt_ref.shape[0] // nrow):
        rows = slice(j * nrow, (j + 1) * nrow)
        u = lax.dot_general(wbt_ref[rows, :], h, _NT_DIMS, preferred_element_type=F32)
        prev = jnp.where(pos != 0, pltpu.roll(u, 1, 1), 0.0)
        nxt = jnp.where(pos != period - 1, pltpu.roll(u, tm - 1, 1), 0.0)
        cwb = cwb_ref[rows, :]
        u = prev * cwb[:, 0:1] + u * cwb[:, 1:2] + nxt * cwb[:, 2:3] + cwb[:, 3:4]
        bt_ref[rows, :] = _silu(u).astype(BF16)
    raw = lax.dot_general(wdt_ref[...], h, _NT_DIMS, preferred_element_type=F32)
    v = raw + dtb_ref[...]
    sp = jnp.maximum(v, 0.0) + jnp.log1p(jnp.exp(-jnp.abs(v)))
    row = lax.broadcasted_iota(jnp.int32, (dt_ref.shape[0], 1), 0)
    is_la = jnp.bitwise_and(row, DT_ROWS_PER_GROUP - 1) >= DT_ROWS_PER_GROUP // 2
    dt_ref[...] = jnp.where(is_la, sp * (-jnp.exp(alog_ref[...])), sp)


def _ssd_in_proj(x2d, mod3, mod_row_of_tile, norm_g, w_main, wbt, wdt_t, cw_xc, cb_xc, cwb, dtb_col,
                 alog_col, *, tm, period, want_z, dz, xc_ranges):
    rows, d = x2d.shape
    dxc = cw_xc.shape[1]
    nb = wbt.shape[0]
    ndt = wdt_t.shape[0]
    ncol, nrow = 512, 256
    xc_starts = tuple(c for lo, hi in xc_ranges for c in range(lo, hi, ncol))
    assert len(xc_starts) * ncol == dxc
    kern = functools.partial(_ssd_in_kernel, period=period, ncol=ncol, nrow=nrow, want_z=want_z,
                             xc_starts=xc_starts)
    out_specs = [
        pl.BlockSpec((tm, dxc), lambda i: (i, 0)),
        pl.BlockSpec((nb, tm), lambda i: (0, i)),
        pl.BlockSpec((ndt, tm), lambda i: (0, i)),
    ]
    out_shape = [
        jax.ShapeDtypeStruct((rows, dxc), BF16),
        jax.ShapeDtypeStruct((nb, rows), BF16),
        jax.ShapeDtypeStruct((ndt, rows), F32),
    ]
    if want_z:
        out_specs.insert(0, pl.BlockSpec((tm, dz), lambda i: (i, 0)))
        out_shape.insert(0, jax.ShapeDtypeStruct((rows, dz), BF16))
    return pl.pallas_call(
        kern,
        grid=(rows // tm,),
        in_specs=[
            pl.BlockSpec((tm, d), lambda i: (i, 0)),
            pl.BlockSpec((None, 1, d), lambda i: (mod_row_of_tile(i), 0, 0)),
            pl.BlockSpec((None, 1, d), lambda i: (mod_row_of_tile(i), 0, 1)),
            _const_spec((1, d)),
            _const_spec(w_main.shape),
            _const_spec((nb, d)),
            _const_spec((ndt, d)),
            _const_spec((3, dxc)),
            _const_spec((2 * period, dxc)),
            _const_spec((1, dxc)),
            _const_spec(cwb.shape),
            _const_spec((ndt, 1)),
            _const_spec((ndt, 1)),
        ],
        out_specs=out_specs,
        out_shape=out_shape,
        compiler_params=_cparams(1),
        name="ssd_in_proj",
    )(x2d, mod3, mod3, norm_g, w_main, wbt, wdt_t, cw_xc, _edge_masked_taps(cw_xc, period), cb_xc, cwb,
      dtb_col, alog_col)


COL_CS, COL_ECS, COL_W = 0, 8, 16


def _head_expand(colmat, lane0):
    r = colmat.shape[0]
    first = lax.broadcasted_iota(jnp.int32, (r, SSM_STATE), 1) < SSM_HEAD_DIM
    cols = [colmat[:, lane0 + hd:lane0 + hd + 1] for hd in range(HEADS_PER_GROUP)]
    lo = jnp.where(first, cols[0], cols[1])
    hi = jnp.where(first, cols[2], cols[3])
    return jnp.concatenate([lo, hi], axis=1)


def _expand_select(lane0):
    r = np.arange(2 * SSM_STATE)[:, None] % SSM_STATE
    l = np.arange(2 * GROUP_WIDTH)[None, :]
    src = lane0 + (l // GROUP_WIDTH) * HEADS_PER_GROUP + (l % GROUP_WIDTH) // SSM_HEAD_DIM
    return jnp.asarray(r == src, dtype=BF16)


def _broadcast_select(lane0, count):
    r = np.arange(2 * SSM_STATE)[:, None] % SSM_STATE
    l = np.arange(SSM_STATE * count)[None, :]
    return jnp.asarray(r == lane0 + l // SSM_STATE, dtype=BF16)


def _hi_lo(colmat):
    hi = colmat.astype(BF16)
    lo = (colmat - hi.astype(F32)).astype(BF16)
    return jnp.concatenate([hi, lo], axis=1)


def _decay_rows(dt_ref, n_chunks, upper, lower):
    q = SSM_CHUNK
    nh = HEADS_PER_GROUP
    dt = jnp.concatenate([dt_ref[0:2 * nh, c * q:(c + 1) * q] for c in range(n_chunks)], axis=0)
    la = jnp.concatenate([dt_ref[2 * nh:4 * nh, c * q:(c + 1) * q] for c in range(n_chunks)], axis=0)
    csf = jnp.dot(la, upper, preferred_element_type=F32, precision=lax.Precision.HIGHEST)
    csb = jnp.dot(la, lower, preferred_element_type=F32, precision=lax.Precision.HIGHEST)
    rows = dt.shape[0]
    row = lax.broadcasted_iota(jnp.int32, (rows, 1), 0)
    is_b = jnp.bitwise_and(row, nh) != 0
    cs = jnp.where(is_b, csb, csf)
    tot = jnp.where(is_b, csb[:, 0:1], csf[:, q - 1:q])
    other = jnp.where(is_b, pltpu.roll(dt, nh, 0), pltpu.roll(dt, rows - nh, 0))
    return cs, jnp.exp(cs), jnp.exp(tot - cs) * dt, cs - jnp.log(dt), jnp.log(dt + other)


def _ssd_scan_kernel(xl_ref, btl_ref, cl_ref, dtl_ref, xc_ref, btc_ref, dtc_ref,
                     selw_ref, sele_ref, selc_ref, o_ref,
                     cols_ref, rows_ref, xw_ref, ee_ref, csb_ref, cb_ref, sloc_ref, hin_ref, etot_ref):
    q = SSM_CHUNK
    nh = HEADS_PER_GROUP
    gw = GROUP_WIDTH
    n_lat = xl_ref.shape[0] // q
    n_ctx = xc_ref.shape[0] // q
    ctx_rows = n_ctx * q
    ki = lax.broadcasted_iota(jnp.int32, (q, q), 0)
    kj = lax.broadcasted_iota(jnp.int32, (q, q), 1)

    upper = (ki <= kj).astype(F32)
    lower = (ki >= kj).astype(F32)
    pad = jnp.zeros((q - 6 * nh, q), F32)
    for dref, n_chunks, base in ((dtc_ref, n_ctx, 0), (dtl_ref, n_lat, n_ctx)):
        cs, ecs, w, rk, lds = _decay_rows(dref, n_chunks, upper, lower)
        for c in range(n_chunks):
            r = slice(2 * nh * c, 2 * nh * (c + 1))
            colmat = jnp.concatenate([cs[r], ecs[r], w[r], pad], axis=0).T
            cols_ref[(base + c) * q:(base + c + 1) * q, :] = colmat
            for d in range(2):
                p0 = 0 if d else q - 1
                etot_ref[d, base + c] = _head_expand(colmat[p0:p0 + 1, :], COL_ECS + nh * d)
        if base:
            rows_ref[0:2 * nh * n_lat, :] = rk
            rows_ref[2 * nh * n_lat:4 * nh * n_lat, :] = lds

    def expand(hl, sel_ref):
        return jnp.dot(hl, sel_ref[...], preferred_element_type=F32)

    xf = xc_ref[...].astype(F32)
    hl = _hi_lo(cols_ref[0:ctx_rows, :])
    xw_ref[0:ctx_rows, :] = (jnp.concatenate([xf, xf], axis=1) * expand(hl, selw_ref)).astype(BF16)
    blk = 4 * q
    for j in range(n_lat * q // blk):
        rows = slice(j * blk, (j + 1) * blk)
        xf = xl_ref[rows, :].astype(F32)
        r0 = ctx_rows + j * blk
        hl = _hi_lo(cols_ref[r0:r0 + blk, :])
        xw_ref[r0:r0 + blk, :] = (jnp.concatenate([xf, xf], axis=1) * expand(hl, selw_ref)).astype(BF16)
        ee_ref[rows, :] = expand(hl, sele_ref)
        csb_ref[rows, :] = expand(hl, selc_ref)

    for c in range(n_ctx):
        sloc_ref[c] = jnp.dot(btc_ref[:, c * q:(c + 1) * q], xw_ref[c * q:(c + 1) * q, :],
                              preferred_element_type=F32)

    def s_body(c, carry):
        r0 = pl.multiple_of(c * q, q)
        bt = btl_ref[:, pl.ds(r0, q)]
        sloc_ref[n_ctx + c] = jnp.dot(bt, xw_ref[pl.ds(ctx_rows + r0, q), :], preferred_element_type=F32)
        cb_ref[c] = jnp.dot(cl_ref[pl.ds(r0, q), :], bt, preferred_element_type=F32)
        return carry

    lax.fori_loop(0, n_lat, s_body, 0, unroll=4)

    for d in range(2):
        lanes = slice(gw * d, gw * (d + 1))
        h = jnp.zeros((SSM_STATE, gw), F32)
        for c in (range(n_ctx - 1, -1, -1) if d else range(n_ctx)):
            h = h * etot_ref[d, c] + sloc_ref[c, :, lanes]

        def b_body(i, h, d=d, lanes=lanes):
            c = (n_lat - 1 - i) if d else i
            hin_ref[c, :, lanes] = h.astype(BF16)
            return h * etot_ref[d, n_ctx + c] + sloc_ref[n_ctx + c, :, lanes]

        lax.fori_loop(0, n_lat, b_body, h, unroll=4)

    lane_head = lax.broadcasted_iota(jnp.int32, (q, gw), 1) // SSM_HEAD_DIM
    below = ki > kj
    above = ki < kj

    def c_body(c, carry):
        r0 = pl.multiple_of(c * q, q)
        x = xl_ref[pl.ds(r0, q), :]
        r8 = pl.multiple_of(c * 2 * nh, 2 * nh)
        rk8 = rows_ref[pl.ds(r8, 2 * nh), :]
        lds8 = rows_ref[pl.ds(2 * nh * n_lat + r8, 2 * nh), :]
        cb = cb_ref[c]
        colmat = cols_ref[pl.ds(ctx_rows + r0, q), :]
        m_parts = []
        x_parts = []
        for hd in range(nh):
            segf = csb_ref[pl.ds(r0, q), q * hd:q * (hd + 1)] - rk8[hd:hd + 1, :]
            segb = colmat[:, COL_CS + nh + hd:COL_CS + nh + hd + 1] - rk8[nh + hd:nh + hd + 1, :]
            arg = jnp.where(below, segf, jnp.where(above, segb, lds8[hd:hd + 1, :]))
            m_parts.append((jnp.exp(arg) * cb).astype(BF16))
            x_parts.append(jnp.where(lane_head == hd, x, jnp.zeros_like(x)))
        m_all = jnp.concatenate(m_parts, axis=1)
        x_bd = jnp.concatenate(x_parts, axis=0)
        y = jnp.dot(m_all, x_bd, preferred_element_type=F32)
        y_off = ee_ref[pl.ds(r0, q), :] * jnp.dot(cl_ref[pl.ds(r0, q), :], hin_ref[c],
                                                  preferred_element_type=F32)
        o_ref[pl.ds(r0, q), :] = (y + y_off[:, 0:gw] + y_off[:, gw:2 * gw]).astype(BF16)
        return carry

    lax.fori_loop(0, n_lat, c_body, 0, unroll=4)


def _ssd_scan(xc_l, bt_l, dt_l, xc_c, bt_c, dt_c, *, batch, seq, ctx_len):
    g = SSM_GROUPS
    gw = GROUP_WIDTH
    n = SSM_STATE
    c_off = (g * gw) // n
    n_lat = seq // SSM_CHUNK
    n_all = n_lat + ctx_len // SSM_CHUNK
    sel_w = _expand_select(COL_W)
    sel_e = _expand_select(COL_ECS)
    sel_c = _broadcast_select(COL_CS, HEADS_PER_GROUP)
    return pl.pallas_call(
        _ssd_scan_kernel,
        grid=(batch, g),
        in_specs=[
            pl.BlockSpec((seq, gw), lambda b, k: (b, k)),
            pl.BlockSpec((n, seq), lambda b, k: (k, b)),
            pl.BlockSpec((seq, n), lambda b, k: (b, c_off + k)),
            pl.BlockSpec((DT_ROWS_PER_GROUP, seq), lambda b, k: (k, b)),
            pl.BlockSpec((ctx_len, gw), lambda b, k: (b, k)),
            pl.BlockSpec((n, ctx_len), lambda b, k: (k, b)),
            pl.BlockSpec((DT_ROWS_PER_GROUP, ctx_len), lambda b, k: (k, b)),
            _const_spec(sel_w.shape),
            _const_spec(sel_e.shape),
            _const_spec(sel_c.shape),
        ],
        out_specs=pl.BlockSpec((seq, gw), lambda b, k: (b, k)),
        out_shape=jax.ShapeDtypeStruct((batch * seq, g * gw), BF16),
        scratch_shapes=[
            pltpu.VMEM((n_all * SSM_CHUNK, SSM_CHUNK), F32),
            pltpu.VMEM((4 * HEADS_PER_GROUP * n_lat, SSM_CHUNK), F32),
            pltpu.VMEM((n_all * SSM_CHUNK, 2 * gw), BF16),
            pltpu.VMEM((seq, 2 * gw), F32),
            pltpu.VMEM((seq, HEADS_PER_GROUP * SSM_CHUNK), F32),
            pltpu.VMEM((n_lat, SSM_CHUNK, SSM_CHUNK), F32),
            pltpu.VMEM((n_all, n, 2 * gw), F32),
            pltpu.VMEM((n_lat, n, 2 * gw), BF16),
            pltpu.VMEM((2, n_all, 1, gw), F32),
        ],
        compiler_params=_cparams(2),
        name="ssd_scan",
    )(xc_l, bt_l, xc_l, dt_l, xc_c, bt_c, dt_c, sel_w, sel_e, sel_c)


def _mlp_tail(x1, g_ref, sh_ref, sc_ref, gate_ref, w1_ref, w2_ref, nff):
    h2 = _norm_mod(x1, g_ref[...], sh_ref[...], sc_ref[...]).astype(BF16)
    dff = w1_ref.shape[1]
    acc = None
    for j in range(dff // nff):
        c0 = j * nff
        a = jnp.dot(h2, w1_ref[:, c0:c0 + nff], preferred_element_type=F32)
        a = jnp.square(jnp.maximum(a, 0.0)).astype(BF16)
        p = jnp.dot(a, w2_ref[c0:c0 + nff, :], preferred_element_type=F32)
        acc = p if acc is None else acc + p
    return x1 + gate_ref[...] * acc


def _ssd_out_kernel(y_ref, xs_ref, z_ref, x_ref, dsk_ref, sng_ref, gm_ref, shf_ref, scf_ref, gf_ref,
                    ng_ref, wo_ref, w1_ref, w2_ref, o_ref):
    acc = None
    for g0 in range(0, y_ref.shape[1], GROUP_WIDTH):
        cols = slice(g0, g0 + GROUP_WIDTH)
        y = y_ref[:, cols].astype(F32) + dsk_ref[:, cols] * xs_ref[:, cols].astype(F32)
        y = y * _silu(z_ref[:, cols].astype(F32))
        y = y * lax.rsqrt(jnp.mean(y * y, axis=-1, keepdims=True) + EPS)
        yn = (y * sng_ref[:, cols]).astype(BF16)
        p = jnp.dot(yn, wo_ref[cols, :], preferred_element_type=F32)
        acc = p if acc is None else acc + p
    x1 = x_ref[...] + gm_ref[...] * acc
    o_ref[...] = _mlp_tail(x1, ng_ref, shf_ref, scf_ref, gf_ref, w1_ref, w2_ref, 1024)


def _mod_spec(d, row_of_tile, k):
    return pl.BlockSpec((None, 1, d), lambda i: (row_of_tile(i), 0, k))


def _ssd_out_mlp(y2d, xc2d, z2d, x2d, dskip_row, ssd_ng_row, mod3, row_of_tile, norm_g, wo, w1, w2, *, tm):
    rows, d = x2d.shape
    di = y2d.shape[1]
    dff = w1.shape[1]
    return pl.pallas_call(
        _ssd_out_kernel,
        grid=(rows // tm,),
        in_specs=[
            pl.BlockSpec((tm, di), lambda i: (i, 0)),
            pl.BlockSpec((tm, di), lambda i: (i, 0)),
            pl.BlockSpec((tm, di), lambda i: (i, 0)),
            pl.BlockSpec((tm, d), lambda i: (i, 0)),
            _const_spec((1, di)),
            _const_spec((1, di)),
            _mod_spec(d, row_of_tile, 2),
            _mod_spec(d, row_of_tile, 3),
            _mod_spec(d, row_of_tile, 4),
            _mod_spec(d, row_of_tile, 5),
            _const_spec((1, d)),
            _const_spec((di, d)),
            _const_spec((d, dff)),
            _const_spec((dff, d)),
        ],
        out_specs=pl.BlockSpec((tm, d), lambda i: (i, 0)),
        out_shape=jax.ShapeDtypeStruct((rows, d), F32),
        compiler_params=_cparams(1),
        name="ssd_out_mlp",
    )(y2d, xc2d, z2d, x2d, dskip_row, ssd_ng_row, mod3, mod3, mod3, mod3, norm_g, wo, w1, w2)


def _sc_layer_kernel(x_ref, shm_ref, scm_ref, gm_ref, shf_ref, scf_ref, gf_ref, ngm_ref, ngf_ref,
                     fg_ref, wi_ref, cw_ref, cwm_ref, wo_ref, w1_ref, w2_ref, o_ref, *, period):
    x = x_ref[...]
    h = _norm_mod(x, ngm_ref[...], shm_ref[...], scm_ref[...]).astype(BF16)
    w = wo_ref.shape[0]
    bg = jnp.dot(h, wi_ref[:, 0:w], preferred_element_type=F32)
    cg = jnp.dot(h, wi_ref[:, w:2 * w], preferred_element_type=F32)
    xv = jnp.dot(h, wi_ref[:, 2 * w:3 * w], preferred_element_type=F32)
    u = (bg * _row_conv3(cg * xv, cw_ref, cwm_ref, slice(0, w), period)).astype(BF16)
    y = jnp.dot(u, wo_ref[...], preferred_element_type=F32)
    x1 = x + gm_ref[...] * y
    x2 = _mlp_tail(x1, ngf_ref, shf_ref, scf_ref, gf_ref, w1_ref, w2_ref, 1024)
    ms = jnp.mean(x2 * x2, axis=-1, keepdims=True)
    o_ref[...] = x2 * lax.rsqrt(ms + EPS) * fg_ref[...]


def _sc_layer(x2d, mod3, row_of_tile, ng_mix, ng_mlp, final_g, wi, conv_w, wo, w1, w2, *, tm, period):
    rows, d = x2d.shape
    dff = w1.shape[1]
    kern = functools.partial(_sc_layer_kernel, period=period)
    return pl.pallas_call(
        kern,
        grid=(rows // tm,),
        in_specs=[
            pl.BlockSpec((tm, d), lambda i: (i, 0)),
            _mod_spec(d, row_of_tile, 0),
            _mod_spec(d, row_of_tile, 1),
            _mod_spec(d, row_of_tile, 2),
            _mod_spec(d, row_of_tile, 3),
            _mod_spec(d, row_of_tile, 4),
            _mod_spec(d, row_of_tile, 5),
            _const_spec((1, d)),
            _const_spec((1, d)),
            _const_spec((1, d)),
            _const_spec(wi.shape),
            _const_spec(conv_w.shape),
            _const_spec((2 * period, conv_w.shape[1])),
            _const_spec(wo.shape),
            _const_spec((d, dff)),
            _const_spec((dff, d)),
        ],
        out_specs=pl.BlockSpec((tm, d), lambda i: (i, 0)),
        out_shape=jax.ShapeDtypeStruct((rows, d), F32),
        compiler_params=_cparams(1),
        name="shortconv_layer",
    )(x2d, mod3, mod3, mod3, mod3, mod3, mod3, ng_mix, ng_mlp, final_g, wi, conv_w,
      _edge_masked_taps(conv_w, period), wo, w1, w2)


def kernel(x, c, ctx, c_ctx, ada_w, ada_b, norm_mix_g, norm_mlp_g, ssd_w_in, ssd_conv_w, ssd_conv_b,
           ssd_dt_bias, ssd_a_log, ssd_d, ssd_norm_g, ssd_w_out, sc_w_in, sc_conv_w, sc_w_out,
           mlp_w1, mlp_w2, final_norm_g):
    batch, seq, d = x.shape
    ctx_len = ctx.shape[1]
    depth = ada_w.shape[0]
    assert depth == 2 and ssd_w_in.shape[0] == 1 and sc_w_in.shape[0] == 1
    d_inner = ssd_w_out.shape[1]
    n_heads = ssd_d.shape[1]
    xbc_dim = ssd_conv_w.shape[2]
    assert n_heads == SSM_GROUPS * HEADS_PER_GROUP and d_inner == SSM_GROUPS * GROUP_WIDTH

    mod_rows = 16
    cvec = jnp.zeros((mod_rows, d), F32).at[:batch].set(c).at[batch].set(c_ctx)
    mod = _modulation(cvec, ada_w, ada_b)
    mod0 = mod[0].reshape(mod_rows, 1, 6 * d)
    mod1 = mod[1].reshape(mod_rows, 1, 6 * d)

    w_in = ssd_w_in[0]
    nbc = SSM_GROUPS * SSM_STATE
    x_rng = (d_inner, 2 * d_inner)
    c_rng = (2 * d_inner + nbc, 2 * d_inner + 2 * nbc)
    w_main = w_in[:, :d_inner + xbc_dim].astype(BF16)
    wbt = w_in[:, 2 * d_inner:2 * d_inner + nbc].T.astype(BF16)
    cw = ssd_conv_w[0]
    cbias = ssd_conv_b[0]
    cw_xc = jnp.concatenate([cw[:, :d_inner], cw[:, d_inner + nbc:]], axis=1)
    cb_xc = jnp.concatenate([cbias[:d_inner], cbias[d_inner + nbc:]]).reshape(1, -1)
    cwb = jnp.concatenate([cw[:, d_inner:d_inner + nbc].T, cbias[d_inner:d_inner + nbc, None],
                           jnp.zeros((nbc, 4), F32)], axis=1)
    gi = jnp.arange(SSM_GROUPS)[:, None, None]
    di_ = jnp.arange(2)[None, :, None]
    ri = jnp.arange(HEADS_PER_GROUP)[None, None, :]
    flat = (di_ * n_heads + gi * HEADS_PER_GROUP + ri).reshape(SSM_GROUPS, 2 * HEADS_PER_GROUP)
    dt_idx = jnp.concatenate([flat, flat], axis=1).reshape(-1)
    wdt_t = w_in[:, d_inner + xbc_dim:][:, dt_idx].T.astype(BF16)
    dtb_col = ssd_dt_bias[0].reshape(-1)[dt_idx].reshape(-1, 1).astype(F32)
    alog_col = ssd_a_log[0].reshape(-1)[dt_idx].reshape(-1, 1).astype(F32)
    ng_mix0 = norm_mix_g[0].reshape(1, d)

    tm = 512
    x2d = x.reshape(batch * seq, d)
    ctx2d = ctx.reshape(batch * ctx_len, d)
    lat_row = lambda i: (i * tm) // seq
    tm_in = 256
    z_l, xc_l, bt_l, dt_l = _ssd_in_proj(x2d, mod0, lambda i: (i * tm_in) // seq, ng_mix0, w_main, wbt,
                                         wdt_t, cw_xc, cb_xc, cwb, dtb_col, alog_col, tm=tm_in,
                                         period=GRID_W, want_z=True, dz=d_inner, xc_ranges=(x_rng, c_rng))
    xc_c, bt_c, dt_c = _ssd_in_proj(ctx2d, mod0, lambda i: batch, ng_mix0, w_main, wbt, wdt_t,
                                    cw_xc[:, :d_inner], cb_xc[:, :d_inner], cwb, dtb_col, alog_col,
                                    tm=ctx_len, period=ctx_len, want_z=False, dz=d_inner,
                                    xc_ranges=(x_rng,))

    dskip_row = jnp.repeat(ssd_d[0].astype(F32), SSM_HEAD_DIM).reshape(1, d_inner)
    ng_row = ssd_norm_g[0].reshape(1, d_inner)
    y_ssd = _ssd_scan(xc_l, bt_l, dt_l, xc_c, bt_c, dt_c, batch=batch, seq=seq, ctx_len=ctx_len)

    x1 = _ssd_out_mlp(y_ssd, xc_l, z_l, x2d, dskip_row, ng_row, mod0, lat_row, norm_mlp_g[0].reshape(1, d),
                      ssd_w_out[0].astype(BF16), mlp_w1[0].astype(BF16), mlp_w2[0].astype(BF16), tm=tm)

    out = _sc_layer(x1, mod1, lat_row, norm_mix_g[1].reshape(1, d), norm_mlp_g[1].reshape(1, d),
                    final_norm_g.reshape(1, d), sc_w_in[0].astype(BF16), sc_conv_w[0],
                    sc_w_out[0].astype(BF16), mlp_w1[1].astype(BF16), mlp_w2[1].astype(BF16),
                    tm=tm, period=GRID_W)
    return out.reshape(batch, seq, d)
```

```python
import functools

import numpy as np

import jax
import jax.numpy as jnp
from jax import lax
from jax.experimental import pallas as pl
from jax.experimental.pallas import tpu as pltpu

F32 = jnp.float32
BF16 = jnp.bfloat16

EPS = 1e-6
GRID_W = 64
SSM_HEAD_DIM = 64
SSM_GROUPS = 8
HEADS_PER_GROUP = 4
SSM_STATE = 128
SSM_CHUNK = 128
GROUP_WIDTH = HEADS_PER_GROUP * SSM_HEAD_DIM
DT_ROWS_PER_GROUP = 16
NEG_BIG = -1e30

VMEM_LIMIT_BYTES = 56 * 1024 * 1024


def _cparams(n_axes):
    return pltpu.CompilerParams(
        dimension_semantics=("arbitrary",) * n_axes,
        vmem_limit_bytes=VMEM_LIMIT_BYTES,
    )


def _const_spec(shape):
    nd = len(shape)
    return pl.BlockSpec(shape, lambda *_: (0,) * nd, pipeline_mode=pl.Buffered(1))


def _silu(u):
    return u * (1.0 / (1.0 + jnp.exp(-u)))


def _norm_mod(x, g, shift, scale):
    ms = jnp.mean(x * x, axis=-1, keepdims=True)
    y = x * lax.rsqrt(ms + EPS) * g
    return y * (1.0 + scale) + shift


def _edge_masked_taps(conv_w, period):
    r = jnp.arange(period)[:, None]
    return jnp.concatenate([jnp.where(r != 0, conv_w[0][None, :], 0.0),
                            jnp.where(r != period - 1, conv_w[2][None, :], 0.0)], axis=0)


def _row_conv3(u, w_ref, wm_ref, cols, period):
    rows = u.shape[0]
    reps = rows // period
    w_prev = jnp.concatenate([wm_ref[0:period, cols]] * reps, axis=0)
    w_next = jnp.concatenate([wm_ref[period:2 * period, cols]] * reps, axis=0)
    return pltpu.roll(u, 1, 0) * w_prev + u * w_ref[1:2, cols] + pltpu.roll(u, rows - 1, 0) * w_next


def _mod_kernel(c_ref, w_ref, b_ref, o_ref):
    s = _silu(c_ref[...]).astype(BF16)
    o_ref[...] = jnp.dot(s, w_ref[...].astype(BF16), preferred_element_type=F32) + b_ref[...]


def _modulation(cvec, ada_w, ada_b):
    depth, d, n = ada_w.shape
    rows = cvec.shape[0]
    tn = 1536
    return pl.pallas_call(
        _mod_kernel,
        grid=(depth, n // tn),
        in_specs=[
            pl.BlockSpec((rows, d), lambda i, j: (0, 0)),
            pl.BlockSpec((None, d, tn), lambda i, j: (i, 0, j)),
            pl.BlockSpec((None, 1, tn), lambda i, j: (i, 0, j)),
        ],
        out_specs=pl.BlockSpec((None, rows, tn), lambda i, j: (i, 0, j)),
        out_shape=jax.ShapeDtypeStruct((depth, rows, n), F32),
        compiler_params=_cparams(2),
        name="adaln_mod",
    )(cvec, ada_w, ada_b.reshape(depth, 1, n))


def _cast_kernel(w_ref, o_ref):
    o_ref[...] = w_ref[...].astype(o_ref.dtype)


def _cast_rows_bf16(w, block_rows=128):
    rows, cols = w.shape
    return pl.pallas_call(
        _cast_kernel,
        grid=(rows // block_rows,),
        in_specs=[pl.BlockSpec((block_rows, cols), lambda i: (i, 0))],
        out_specs=pl.BlockSpec((block_rows, cols), lambda i: (i, 0)),
        out_shape=jax.ShapeDtypeStruct((rows, cols), BF16),
        compiler_params=_cparams(1),
        name="cast_bf16",
    )(w)


_NT_DIMS = (((1,), (1,)), ((), ()))


def _ssd_in_kernel(x_ref, sh_ref, sc_ref, g_ref, w_ref, wbt_ref, wdt_ref, cw_ref, cwm_ref, cb_ref,
                   cwb_ref, dtb_ref, alog_ref, *out_refs, period, ncol, nrow, want_z, xc_starts):
    if want_z:
        z_ref, xc_ref, bt_ref, dt_ref = out_refs
    else:
        xc_ref, bt_ref, dt_ref = out_refs
    h = _norm_mod(x_ref[...], g_ref[...], sh_ref[...], sc_ref[...]).astype(BF16)
    if want_z:
        z_ref[...] = jnp.dot(h, w_ref[:, 0:z_ref.shape[1]], preferred_element_type=F32).astype(BF16)
    for j, w0 in enumerate(xc_starts):
        cols = slice(j * ncol, (j + 1) * ncol)
        u = jnp.dot(h, w_ref[:, w0:w0 + ncol], preferred_element_type=F32)
        u = _row_conv3(u, cw_ref, cwm_ref, cols, period) + cb_ref[:, cols]
        xc_ref[:, cols] = _silu(u).astype(BF16)
    tm = x_ref.shape[0]
    pos = jnp.bitwise_and(lax.broadcasted_iota(jnp.int32, (1, tm), 1), period - 1)
    for j in range(bt_ref.shape[0] // nrow):
        rows = slice(j * nrow, (j + 1) * nrow)
        u = lax.dot_general(wbt_ref[rows, :], h, _NT_DIMS, preferred_element_type=F32)
        prev = jnp.where(pos != 0, pltpu.roll(u, 1, 1), 0.0)
        nxt = jnp.where(pos != period - 1, pltpu.roll(u, tm - 1, 1), 0.0)
        cwb = cwb_ref[rows, :]
        u = prev * cwb[:, 0:1] + u * cwb[:, 1:2] + nxt * cwb[:, 2:3] + cwb[:, 3:4]
        bt_ref[rows, :] = _silu(u).astype(BF16)
    raw = lax.dot_general(wdt_ref[...], h, _NT_DIMS, preferred_element_type=F32)
    v = raw + dtb_ref[...]
    sp = jnp.maximum(v, 0.0) + jnp.log1p(jnp.exp(-jnp.abs(v)))
    row = lax.broadcasted_iota(jnp.int32, (dt_ref.shape[0], 1), 0)
    is_la = jnp.bitwise_and(row, DT_ROWS_PER_GROUP - 1) >= DT_ROWS_PER_GROUP // 2
    dt_ref[...] = jnp.where(is_la, sp * (-jnp.exp(alog_ref[...])), sp)


def _ssd_in_proj(x2d, mod3, mod_row_of_tile, norm_g, w_main, wbt, wdt_t, cw_xc, cb_xc, cwb, dtb_col,
                 alog_col, *, tm, period, want_z, dz, xc_ranges):
    rows, d = x2d.shape
    dxc = cw_xc.shape[1]
    nb = wbt.shape[0]
    ndt = wdt_t.shape[0]
    ncol, nrow = 512, 256
    xc_starts = tuple(c for lo, hi in xc_ranges for c in range(lo, hi, ncol))
    assert len(xc_starts) * ncol == dxc
    kern = functools.partial(_ssd_in_kernel, period=period, ncol=ncol, nrow=nrow, want_z=want_z,
                             xc_starts=xc_starts)
    out_specs = [
        pl.BlockSpec((tm, dxc), lambda i: (i, 0)),
        pl.BlockSpec((nb, tm), lambda i: (0, i)),
        pl.BlockSpec((ndt, tm), lambda i: (0, i)),
    ]
    out_shape = [
        jax.ShapeDtypeStruct((rows, dxc), BF16),
        jax.ShapeDtypeStruct((nb, rows), BF16),
        jax.ShapeDtypeStruct((ndt, rows), F32),
    ]
    if want_z:
        out_specs.insert(0, pl.BlockSpec((tm, dz), lambda i: (i, 0)))
        out_shape.insert(0, jax.ShapeDtypeStruct((rows, dz), BF16))
    return pl.pallas_call(
        kern,
        grid=(rows // tm,),
        in_specs=[
            pl.BlockSpec((tm, d), lambda i: (i, 0)),
            pl.BlockSpec((None, 1, d), lambda i: (mod_row_of_tile(i), 0, 0)),
            pl.BlockSpec((None, 1, d), lambda i: (mod_row_of_tile(i), 0, 1)),
            _const_spec((1, d)),
            _const_spec(w_main.shape),
            _const_spec((nb, d)),
            _const_spec((ndt, d)),
            _const_spec((3, dxc)),
            _const_spec((2 * period, dxc)),
            _const_spec((1, dxc)),
            _const_spec(cwb.shape),
            _const_spec((ndt, 1)),
            _const_spec((ndt, 1)),
        ],
        out_specs=out_specs,
        out_shape=out_shape,
        compiler_params=_cparams(1),
        name="ssd_in_proj",
    )(x2d, mod3, mod3, norm_g, w_main, wbt, wdt_t, cw_xc, _edge_masked_taps(cw_xc, period), cb_xc, cwb,
      dtb_col, alog_col)


COL_CS, COL_ECS, COL_W = 0, 8, 16


def _head_expand(colmat, lane0):
    r = colmat.shape[0]
    first = lax.broadcasted_iota(jnp.int32, (r, SSM_STATE), 1) < SSM_HEAD_DIM
    cols = [colmat[:, lane0 + hd:lane0 + hd + 1] for hd in range(HEADS_PER_GROUP)]
    lo = jnp.where(first, cols[0], cols[1])
    hi = jnp.where(first, cols[2], cols[3])
    return jnp.concatenate([lo, hi], axis=1)


def _expand_select(lane0):
    r = np.arange(2 * SSM_STATE)[:, None] % SSM_STATE
    l = np.arange(2 * GROUP_WIDTH)[None, :]
    src = lane0 + (l // GROUP_WIDTH) * HEADS_PER_GROUP + (l % GROUP_WIDTH) // SSM_HEAD_DIM
    return jnp.asarray(r == src, dtype=BF16)


def _broadcast_select(lane0, count):
    r = np.arange(2 * SSM_STATE)[:, None] % SSM_STATE
    l = np.arange(SSM_STATE * count)[None, :]
    return jnp.asarray(r == lane0 + l // SSM_STATE, dtype=BF16)


def _hi_lo(colmat):
    hi = colmat.astype(BF16)
    lo = (colmat - hi.astype(F32)).astype(BF16)
    return jnp.concatenate([hi, lo], axis=1)


def _decay_rows(dt_ref, n_chunks, upper, lower):
    q = SSM_CHUNK
    nh = HEADS_PER_GROUP
    dt = jnp.concatenate([dt_ref[0:2 * nh, c * q:(c + 1) * q] for c in range(n_chunks)], axis=0)
    la = jnp.concatenate([dt_ref[2 * nh:4 * nh, c * q:(c + 1) * q] for c in range(n_chunks)], axis=0)
    csf = jnp.dot(la, upper, preferred_element_type=F32, precision=lax.Precision.HIGHEST)
    csb = jnp.dot(la, lower, preferred_element_type=F32, precision=lax.Precision.HIGHEST)
    rows = dt.shape[0]
    row = lax.broadcasted_iota(jnp.int32, (rows, 1), 0)
    is_b = jnp.bitwise_and(row, nh) != 0
    cs = jnp.where(is_b, csb, csf)
    tot = jnp.where(is_b, csb[:, 0:1], csf[:, q - 1:q])
    other = jnp.where(is_b, pltpu.roll(dt, nh, 0), pltpu.roll(dt, rows - nh, 0))
    return cs, jnp.exp(cs), jnp.exp(tot - cs) * dt, cs - jnp.log(dt), jnp.log(dt + other)


def _ssd_scan_kernel(xl_ref, btl_ref, cl_ref, dtl_ref, xc_ref, btc_ref, dtc_ref,
                     selw_ref, sele_ref, selc_ref, o_ref,
                     cols_ref, rows_ref, xw_ref, ee_ref, csb_ref, cb_ref, sloc_ref, hin_ref, etot_ref):
    q = SSM_CHUNK
    nh = HEADS_PER_GROUP
    gw = GROUP_WIDTH
    n_lat = xl_ref.shape[0] // q
    n_ctx = xc_ref.shape[0] // q
    ctx_rows = n_ctx * q
    ki = lax.broadcasted_iota(jnp.int32, (q, q), 0)
    kj = lax.broadcasted_iota(jnp.int32, (q, q), 1)

    upper = (ki <= kj).astype(F32)
    lower = (ki >= kj).astype(F32)
    pad = jnp.zeros((q - 6 * nh, q), F32)
    for dref, n_chunks, base in ((dtc_ref, n_ctx, 0), (dtl_ref, n_lat, n_ctx)):
        cs, ecs, w, rk, lds = _decay_rows(dref, n_chunks, upper, lower)
        for c in range(n_chunks):
            r = slice(2 * nh * c, 2 * nh * (c + 1))
            colmat = jnp.concatenate([cs[r], ecs[r], w[r], pad], axis=0).T
            cols_ref[(base + c) * q:(base + c + 1) * q, :] = colmat
            for d in range(2):
                p0 = 0 if d else q - 1
                etot_ref[d, base + c] = _head_expand(colmat[p0:p0 + 1, :], COL_ECS + nh * d)
        if base:
            rows_ref[0:2 * nh * n_lat, :] = rk
            rows_ref[2 * nh * n_lat:4 * nh * n_lat, :] = lds

    def expand(hl, sel_ref):
        return jnp.dot(hl, sel_ref[...], preferred_element_type=F32)

    xf = xc_ref[...].astype(F32)
    hl = _hi_lo(cols_ref[0:ctx_rows, :])
    xw_ref[0:ctx_rows, :] = (jnp.concatenate([xf, xf], axis=1) * expand(hl, selw_ref)).astype(BF16)
    blk = 4 * q
    for j in range(n_lat * q // blk):
        rows = slice(j * blk, (j + 1) * blk)
        xf = xl_ref[rows, :].astype(F32)
        r0 = ctx_rows + j * blk
        hl = _hi_lo(cols_ref[r0:r0 + blk, :])
        xw_ref[r0:r0 + blk, :] = (jnp.concatenate([xf, xf], axis=1) * expand(hl, selw_ref)).astype(BF16)
        ee_ref[rows, :] = expand(hl, sele_ref)
        csb_ref[rows, :] = expand(hl, selc_ref)

    for c in range(n_ctx):
        sloc_ref[c] = jnp.dot(btc_ref[:, c * q:(c + 1) * q], xw_ref[c * q:(c + 1) * q, :],
                              preferred_element_type=F32)

    def s_body(c, carry):
        r0 = pl.multiple_of(c * q, q)
        bt = btl_ref[:, pl.ds(r0, q)]
        sloc_ref[n_ctx + c] = jnp.dot(bt, xw_ref[pl.ds(ctx_rows + r0, q), :], preferred_element_type=F32)
        cb_ref[c] = jnp.dot(cl_ref[pl.ds(r0, q), :], bt, preferred_element_type=F32)
        return carry

    lax.fori_loop(0, n_lat, s_body, 0, unroll=4)

    for d in range(2):
        lanes = slice(gw * d, gw * (d + 1))
        h = jnp.zeros((SSM_STATE, gw), F32)
        for c in (range(n_ctx - 1, -1, -1) if d else range(n_ctx)):
            h = h * etot_ref[d, c] + sloc_ref[c, :, lanes]

        def b_body(i, h, d=d, lanes=lanes):
            c = (n_lat - 1 - i) if d else i
            hin_ref[c, :, lanes] = h.astype(BF16)
            return h * etot_ref[d, n_ctx + c] + sloc_ref[n_ctx + c, :, lanes]

        lax.fori_loop(0, n_lat, b_body, h, unroll=4)

    lane_head = lax.broadcasted_iota(jnp.int32, (q, gw), 1) // SSM_HEAD_DIM
    below = ki > kj
    above = ki < kj

    def c_body(c, carry):
        r0 = pl.multiple_of(c * q, q)
        x = xl_ref[pl.ds(r0, q), :]
        r8 = pl.multiple_of(c * 2 * nh, 2 * nh)
        rk8 = rows_ref[pl.ds(r8, 2 * nh), :]
        lds8 = rows_ref[pl.ds(2 * nh * n_lat + r8, 2 * nh), :]
        cb = cb_ref[c]
        colmat = cols_ref[pl.ds(ctx_rows + r0, q), :]
        m_parts = []
        x_parts = []
        for hd in range(nh):
            segf = csb_ref[pl.ds(r0, q), q * hd:q * (hd + 1)] - rk8[hd:hd + 1, :]
            segb = colmat[:, COL_CS + nh + hd:COL_CS + nh + hd + 1] - rk8[nh + hd:nh + hd + 1, :]
            arg = jnp.where(below, segf, jnp.where(above, segb, lds8[hd:hd + 1, :]))
            m_parts.append((jnp.exp(arg) * cb).astype(BF16))
            x_parts.append(jnp.where(lane_head == hd, x, jnp.zeros_like(x)))
        m_all = jnp.concatenate(m_parts, axis=1)
        x_bd = jnp.concatenate(x_parts, axis=0)
        y = jnp.dot(m_all, x_bd, preferred_element_type=F32)
        y_off = ee_ref[pl.ds(r0, q), :] * jnp.dot(cl_ref[pl.ds(r0, q), :], hin_ref[c],
                                                  preferred_element_type=F32)
        o_ref[pl.ds(r0, q), :] = (y + y_off[:, 0:gw] + y_off[:, gw:2 * gw]).astype(BF16)
        return carry

    lax.fori_loop(0, n_lat, c_body, 0, unroll=4)


def _ssd_scan(xc_l, bt_l, dt_l, xc_c, bt_c, dt_c, *, batch, seq, ctx_len):
    g = SSM_GROUPS
    gw = GROUP_WIDTH
    n = SSM_STATE
    c_off = (g * gw) // n
    n_lat = seq // SSM_CHUNK
    n_all = n_lat + ctx_len // SSM_CHUNK
    sel_w = _expand_select(COL_W)
    sel_e = _expand_select(COL_ECS)
    sel_c = _broadcast_select(COL_CS, HEADS_PER_GROUP)
    return pl.pallas_call(
        _ssd_scan_kernel,
        grid=(batch, g),
        in_specs=[
            pl.BlockSpec((seq, gw), lambda b, k: (b, k)),
            pl.BlockSpec((n, seq), lambda b, k: (k, b)),
            pl.BlockSpec((seq, n), lambda b, k: (b, c_off + k)),
            pl.BlockSpec((DT_ROWS_PER_GROUP, seq), lambda b, k: (k, b)),
            pl.BlockSpec((ctx_len, gw), lambda b, k: (b, k)),
            pl.BlockSpec((n, ctx_len), lambda b, k: (k, b)),
            pl.BlockSpec((DT_ROWS_PER_GROUP, ctx_len), lambda b, k: (k, b)),
            _const_spec(sel_w.shape),
            _const_spec(sel_e.shape),
            _const_spec(sel_c.shape),
        ],
        out_specs=pl.BlockSpec((seq, gw), lambda b, k: (b, k)),
        out_shape=jax.ShapeDtypeStruct((batch * seq, g * gw), BF16),
        scratch_shapes=[
            pltpu.VMEM((n_all * SSM_CHUNK, SSM_CHUNK), F32),
            pltpu.VMEM((4 * HEADS_PER_GROUP * n_lat, SSM_CHUNK), F32),
            pltpu.VMEM((n_all * SSM_CHUNK, 2 * gw), BF16),
            pltpu.VMEM((seq, 2 * gw), F32),
            pltpu.VMEM((seq, HEADS_PER_GROUP * SSM_CHUNK), F32),
            pltpu.VMEM((n_lat, SSM_CHUNK, SSM_CHUNK), F32),
            pltpu.VMEM((n_all, n, 2 * gw), F32),
            pltpu.VMEM((n_lat, n, 2 * gw), BF16),
            pltpu.VMEM((2, n_all, 1, gw), F32),
        ],
        compiler_params=_cparams(2),
        name="ssd_scan",
    )(xc_l, bt_l, xc_l, dt_l, xc_c, bt_c, dt_c, sel_w, sel_e, sel_c)


def _mlp_tail(x1, g_ref, sh_ref, sc_ref, gate_ref, w1_ref, w2_ref, nff):
    h2 = _norm_mod(x1, g_ref[...], sh_ref[...], sc_ref[...]).astype(BF16)
    dff = w1_ref.shape[1]
    acc = None
    for j in range(dff // nff):
        c0 = j * nff
        a = jnp.dot(h2, w1_ref[:, c0:c0 + nff], preferred_element_type=F32)
        a = jnp.square(jnp.maximum(a, 0.0)).astype(BF16)
        p = jnp.dot(a, w2_ref[c0:c0 + nff, :], preferred_element_type=F32)
        acc = p if acc is None else acc + p
    return x1 + gate_ref[...] * acc


def _ssd_out_kernel(y_ref, xs_ref, z_ref, x_ref, dsk_ref, sng_ref, gm_ref, shf_ref, scf_ref, gf_ref,
                    ng_ref, wo_ref, w1_ref, w2_ref, o_ref):
    acc = None
    for g0 in range(0, y_ref.shape[1], GROUP_WIDTH):
        cols = slice(g0, g0 + GROUP_WIDTH)
        y = y_ref[:, cols].astype(F32) + dsk_ref[:, cols] * xs_ref[:, cols].astype(F32)
        y = y * _silu(z_ref[:, cols].astype(F32))
        y = y * lax.rsqrt(jnp.mean(y * y, axis=-1, keepdims=True) + EPS)
        yn = (y * sng_ref[:, cols]).astype(BF16)
        p = jnp.dot(yn, wo_ref[cols, :], preferred_element_type=F32)
        acc = p if acc is None else acc + p
    x1 = x_ref[...] + gm_ref[...] * acc
    o_ref[...] = _mlp_tail(x1, ng_ref, shf_ref, scf_ref, gf_ref, w1_ref, w2_ref, 1024)


def _mod_spec(d, row_of_tile, k):
    return pl.BlockSpec((None, 1, d), lambda i: (row_of_tile(i), 0, k))


def _ssd_out_mlp(y2d, xc2d, z2d, x2d, dskip_row, ssd_ng_row, mod3, row_of_tile, norm_g, wo, w1, w2, *, tm):
    rows, d = x2d.shape
    di = y2d.shape[1]
    dff = w1.shape[1]
    return pl.pallas_call(
        _ssd_out_kernel,
        grid=(rows // tm,),
        in_specs=[
            pl.BlockSpec((tm, di), lambda i: (i, 0)),
            pl.BlockSpec((tm, di), lambda i: (i, 0)),
            pl.BlockSpec((tm, di), lambda i: (i, 0)),
            pl.BlockSpec((tm, d), lambda i: (i, 0)),
            _const_spec((1, di)),
            _const_spec((1, di)),
            _mod_spec(d, row_of_tile, 2),
            _mod_spec(d, row_of_tile, 3),
            _mod_spec(d, row_of_tile, 4),
            _mod_spec(d, row_of_tile, 5),
            _const_spec((1, d)),
            _const_spec((di, d)),
            _const_spec((d, dff)),
            _const_spec((dff, d)),
        ],
        out_specs=pl.BlockSpec((tm, d), lambda i: (i, 0)),
        out_shape=jax.ShapeDtypeStruct((rows, d), F32),
        compiler_params=_cparams(1),
        name="ssd_out_mlp",
    )(y2d, xc2d, z2d, x2d, dskip_row, ssd_ng_row, mod3, mod3, mod3, mod3, norm_g, wo, w1, w2)


def _sc_layer_kernel(x_ref, shm_ref, scm_ref, gm_ref, shf_ref, scf_ref, gf_ref, ngm_ref, ngf_ref,
                     fg_ref, wi_ref, cw_ref, cwm_ref, wo_ref, w1_ref, w2_ref, o_ref, *, period):
    x = x_ref[...]
    h = _norm_mod(x, ngm_ref[...], shm_ref[...], scm_ref[...]).astype(BF16)
    w = wo_ref.shape[0]
    bg = jnp.dot(h, wi_ref[:, 0:w], preferred_element_type=F32)
    cg = jnp.dot(h, wi_ref[:, w:2 * w], preferred_element_type=F32)
    xv = jnp.dot(h, wi_ref[:, 2 * w:3 * w], preferred_element_type=F32)
    u = (bg * _row_conv3(cg * xv, cw_ref, cwm_ref, slice(0, w), period)).astype(BF16)
    y = jnp.dot(u, wo_ref[...], preferred_element_type=F32)
    x1 = x + gm_ref[...] * y
    x2 = _mlp_tail(x1, ngf_ref, shf_ref, scf_ref, gf_ref, w1_ref, w2_ref, 1024)
    ms = jnp.mean(x2 * x2, axis=-1, keepdims=True)
    o_ref[...] = x2 * lax.rsqrt(ms + EPS) * fg_ref[...]


def _sc_layer(x2d, mod3, row_of_tile, ng_mix, ng_mlp, final_g, wi, conv_w, wo, w1, w2, *, tm, period):
    rows, d = x2d.shape
    dff = w1.shape[1]
    kern = functools.partial(_sc_layer_kernel, period=period)
    return pl.pallas_call(
        kern,
        grid=(rows // tm,),
        in_specs=[
            pl.BlockSpec((tm, d), lambda i: (i, 0)),
            _mod_spec(d, row_of_tile, 0),
            _mod_spec(d, row_of_tile, 1),
            _mod_spec(d, row_of_tile, 2),
            _mod_spec(d, row_of_tile, 3),
            _mod_spec(d, row_of_tile, 4),
            _mod_spec(d, row_of_tile, 5),
            _const_spec((1, d)),
            _const_spec((1, d)),
            _const_spec((1, d)),
            _const_spec(wi.shape),
            _const_spec(conv_w.shape),
            _const_spec((2 * period, conv_w.shape[1])),
            _const_spec(wo.shape),
            _const_spec((d, dff)),
            _const_spec((dff, d)),
        ],
        out_specs=pl.BlockSpec((tm, d), lambda i: (i, 0)),
        out_shape=jax.ShapeDtypeStruct((rows, d), F32),
        compiler_params=_cparams(1),
        name="shortconv_layer",
    )(x2d, mod3, mod3, mod3, mod3, mod3, mod3, ng_mix, ng_mlp, final_g, wi, conv_w,
      _edge_masked_taps(conv_w, period), wo, w1, w2)


def kernel(x, c, ctx, c_ctx, ada_w, ada_b, norm_mix_g, norm_mlp_g, ssd_w_in, ssd_conv_w, ssd_conv_b,
           ssd_dt_bias, ssd_a_log, ssd_d, ssd_norm_g, ssd_w_out, sc_w_in, sc_conv_w, sc_w_out,
           mlp_w1, mlp_w2, final_norm_g):
    batch, seq, d = x.shape
    ctx_len = ctx.shape[1]
    depth = ada_w.shape[0]
    assert depth == 2 and ssd_w_in.shape[0] == 1 and sc_w_in.shape[0] == 1
    d_inner = ssd_w_out.shape[1]
    n_heads = ssd_d.shape[1]
    xbc_dim = ssd_conv_w.shape[2]
    assert n_heads == SSM_GROUPS * HEADS_PER_GROUP and d_inner == SSM_GROUPS * GROUP_WIDTH

    mod_rows = 16
    cvec = jnp.zeros((mod_rows, d), F32).at[:batch].set(c).at[batch].set(c_ctx)
    mod = _modulation(cvec, ada_w, ada_b)
    mod0 = mod[0].reshape(mod_rows, 1, 6 * d)
    mod1 = mod[1].reshape(mod_rows, 1, 6 * d)

    w_in = ssd_w_in[0]
    nbc = SSM_GROUPS * SSM_STATE
    x_rng = (d_inner, 2 * d_inner)
    c_rng = (2 * d_inner + nbc, 2 * d_inner + 2 * nbc)
    w_main = _cast_rows_bf16(w_in)
    w_bdt = lax.optimization_barrier(w_in[:, 2 * d_inner:])
    wbt = w_bdt[:, :nbc].T.astype(BF16)
    cw = ssd_conv_w[0]
    cbias = ssd_conv_b[0]
    cw_xc = jnp.concatenate([cw[:, :d_inner], cw[:, d_inner + nbc:]], axis=1)
    cb_xc = jnp.concatenate([cbias[:d_inner], cbias[d_inner + nbc:]]).reshape(1, -1)
    cwb = jnp.concatenate([cw[:, d_inner:d_inner + nbc].T, cbias[d_inner:d_inner + nbc, None],
                           jnp.zeros((nbc, 4), F32)], axis=1)
    gi = jnp.arange(SSM_GROUPS)[:, None, None]
    di_ = jnp.arange(2)[None, :, None]
    ri = jnp.arange(HEADS_PER_GROUP)[None, None, :]
    flat = (di_ * n_heads + gi * HEADS_PER_GROUP + ri).reshape(SSM_GROUPS, 2 * HEADS_PER_GROUP)
    dt_idx = jnp.concatenate([flat, flat], axis=1).reshape(-1)
    wdt_t = w_bdt[:, 2 * nbc:][:, dt_idx].T.astype(BF16)
    dtb_col = ssd_dt_bias[0].reshape(-1)[dt_idx].reshape(-1, 1).astype(F32)
    alog_col = ssd_a_log[0].reshape(-1)[dt_idx].reshape(-1, 1).astype(F32)
    ng_mix0 = norm_mix_g[0].reshape(1, d)

    tm = 512
    x2d = x.reshape(batch * seq, d)
    ctx2d = ctx.reshape(batch * ctx_len, d)
    lat_row = lambda i: (i * tm) // seq
    tm_in = 256
    z_l, xc_l, bt_l, dt_l = _ssd_in_proj(x2d, mod0, lambda i: (i * tm_in) // seq, ng_mix0, w_main, wbt,
                                         wdt_t, cw_xc, cb_xc, cwb, dtb_col, alog_col, tm=tm_in,
                                         period=GRID_W, want_z=True, dz=d_inner, xc_ranges=(x_rng, c_rng))
    xc_c, bt_c, dt_c = _ssd_in_proj(ctx2d, mod0, lambda i: batch, ng_mix0, w_main, wbt, wdt_t,
                                    cw_xc[:, :d_inner], cb_xc[:, :d_inner], cwb, dtb_col, alog_col,
                                    tm=ctx_len, period=ctx_len, want_z=False, dz=d_inner,
                                    xc_ranges=(x_rng,))

    dskip_row = jnp.repeat(ssd_d[0].astype(F32), SSM_HEAD_DIM).reshape(1, d_inner)
    ng_row = ssd_norm_g[0].reshape(1, d_inner)
    y_ssd = _ssd_scan(xc_l, bt_l, dt_l, xc_c, bt_c, dt_c, batch=batch, seq=seq, ctx_len=ctx_len)

    x1 = _ssd_out_mlp(y_ssd, xc_l, z_l, x2d, dskip_row, ng_row, mod0, lat_row, norm_mlp_g[0].reshape(1, d),
                      ssd_w_out[0].astype(BF16), mlp_w1[0].astype(BF16), mlp_w2[0].astype(BF16), tm=tm)

    out = _sc_layer(x1, mod1, lat_row, norm_mix_g[1].reshape(1, d), norm_mlp_g[1].reshape(1, d),
                    final_norm_g.reshape(1, d), sc_w_in[0].astype(BF16), sc_conv_w[0],
                    sc_w_out[0].astype(BF16), mlp_w1[1].astype(BF16), mlp_w2[1].astype(BF16),
                    tm=tm, period=GRID_W)
    return out.reshape(batch, seq, d)
```

```python
import functools

import numpy as np

import jax
import jax.numpy as jnp
from jax import lax
from jax.experimental import pallas as pl
from jax.experimental.pallas import tpu as pltpu

F32 = jnp.float32
BF16 = jnp.bfloat16

EPS = 1e-6
GRID_W = 64
SSM_HEAD_DIM = 64
SSM_GROUPS = 8
HEADS_PER_GROUP = 4
SSM_STATE = 128
SSM_CHUNK = 128
GROUP_WIDTH = HEADS_PER_GROUP * SSM_HEAD_DIM
DT_ROWS_PER_GROUP = 16
NEG_BIG = -1e30

VMEM_LIMIT_BYTES = 56 * 1024 * 1024


def _cparams(n_axes):
    return pltpu.CompilerParams(
        dimension_semantics=("arbitrary",) * n_axes,
        vmem_limit_bytes=VMEM_LIMIT_BYTES,
    )


def _const_spec(shape):
    nd = len(shape)
    return pl.BlockSpec(shape, lambda *_: (0,) * nd, pipeline_mode=pl.Buffered(1))


def _silu(u):
    return u * (1.0 / (1.0 + jnp.exp(-u)))


def _norm_mod(x, g, shift, scale):
    ms = jnp.mean(x * x, axis=-1, keepdims=True)
    y = x * lax.rsqrt(ms + EPS) * g
    return y * (1.0 + scale) + shift


def _edge_masked_taps(conv_w, period):
    r = jnp.arange(period)[:, None]
    return jnp.concatenate([jnp.where(r != 0, conv_w[0][None, :], 0.0),
                            jnp.where(r != period - 1, conv_w[2][None, :], 0.0)], axis=0)


def _row_conv3(u, w_ref, wm_ref, cols, period):
    rows = u.shape[0]
    reps = rows // period
    w_prev = jnp.concatenate([wm_ref[0:period, cols]] * reps, axis=0)
    w_next = jnp.concatenate([wm_ref[period:2 * period, cols]] * reps, axis=0)
    return pltpu.roll(u, 1, 0) * w_prev + u * w_ref[1:2, cols] + pltpu.roll(u, rows - 1, 0) * w_next


def _mod_kernel(c_ref, w_ref, b_ref, o_ref):
    s = _silu(c_ref[...]).astype(BF16)
    o_ref[...] = jnp.dot(s, w_ref[...].astype(BF16), preferred_element_type=F32) + b_ref[...]


def _modulation(cvec, ada_w, ada_b):
    depth, d, n = ada_w.shape
    rows = cvec.shape[0]
    tn = 1536
    return pl.pallas_call(
        _mod_kernel,
        grid=(depth, n // tn),
        in_specs=[
            pl.BlockSpec((rows, d), lambda i, j: (0, 0)),
            pl.BlockSpec((None, d, tn), lambda i, j: (i, 0, j)),
            pl.BlockSpec((None, 1, tn), lambda i, j: (i, 0, j)),
        ],
        out_specs=pl.BlockSpec((None, rows, tn), lambda i, j: (i, 0, j)),
        out_shape=jax.ShapeDtypeStruct((depth, rows, n), F32),
        compiler_params=_cparams(2),
        name="adaln_mod",
    )(cvec, ada_w, ada_b.reshape(depth, 1, n))


_NT_DIMS = (((1,), (1,)), ((), ()))


def _ssd_in_kernel(x_ref, sh_ref, sc_ref, g_ref, w_ref, wbt_ref, wdt_ref, cw_ref, cwm_ref, cb_ref,
                   cwb_ref, dtb_ref, alog_ref, *out_refs, period, ncol, nrow, want_z, xc_starts):
    if want_z:
        z_ref, xc_ref, bt_ref, dt_ref = out_refs
    else:
        xc_ref, bt_ref, dt_ref = out_refs
    h = _norm_mod(x_ref[...], g_ref[...], sh_ref[...], sc_ref[...]).astype(BF16)
    if want_z:
        z_ref[...] = jnp.dot(h, w_ref[:, 0:z_ref.shape[1]], preferred_element_type=F32).astype(BF16)
    for j, w0 in enumerate(xc_starts):
        cols = slice(j * ncol, (j + 1) * ncol)
        u = jnp.dot(h, w_ref[:, w0:w0 + ncol], preferred_element_type=F32)
        u = _row_conv3(u, cw_ref, cwm_ref, cols, period) + cb_ref[:, cols]
        xc_ref[:, cols] = _silu(u).astype(BF16)
    tm = x_ref.shape[0]
    pos = jnp.bitwise_and(lax.broadcasted_iota(jnp.int32, (1, tm), 1), period - 1)
    for j in range(bt_ref.shape[0] // nrow):
        rows = slice(j * nrow, (j + 1) * nrow)
        u = lax.dot_general(wbt_ref[rows, :], h, _NT_DIMS, preferred_element_type=F32)
        prev = jnp.where(pos != 0, pltpu.roll(u, 1, 1), 0.0)
        nxt = jnp.where(pos != period - 1, pltpu.roll(u, tm - 1, 1), 0.0)
        cwb = cwb_ref[rows, :]
        u = prev * cwb[:, 0:1] + u * cwb[:, 1:2] + nxt * cwb[:, 2:3] + cwb[:, 3:4]
        bt_ref[rows, :] = _silu(u).astype(BF16)
    raw = lax.dot_general(wdt_ref[...], h, _NT_DIMS, preferred_element_type=F32)
    v = raw + dtb_ref[...]
    sp = jnp.maximum(v, 0.0) + jnp.log1p(jnp.exp(-jnp.abs(v)))
    row = lax.broadcasted_iota(jnp.int32, (dt_ref.shape[0], 1), 0)
    is_la = jnp.bitwise_and(row, DT_ROWS_PER_GROUP - 1) >= DT_ROWS_PER_GROUP // 2
    dt_ref[...] = jnp.where(is_la, sp * (-jnp.exp(alog_ref[...])), sp)


def _ssd_in_proj(x2d, mod3, mod_row_of_tile, norm_g, w_main, wbt, wdt_t, cw_xc, cb_xc, cwb, dtb_col,
                 alog_col, *, tm, period, want_z, dz, xc_ranges):
    rows, d = x2d.shape
    dxc = cw_xc.shape[1]
    nb = wbt.shape[0]
    ndt = wdt_t.shape[0]
    ncol, nrow = 512, 256
    xc_starts = tuple(c for lo, hi in xc_ranges for c in range(lo, hi, ncol))
    assert len(xc_starts) * ncol == dxc
    kern = functools.partial(_ssd_in_kernel, period=period, ncol=ncol, nrow=nrow, want_z=want_z,
                             xc_starts=xc_starts)
    out_specs = [
        pl.BlockSpec((tm, dxc), lambda i: (i, 0)),
        pl.BlockSpec((nb, tm), lambda i: (0, i)),
        pl.BlockSpec((ndt, tm), lambda i: (0, i)),
    ]
    out_shape = [
        jax.ShapeDtypeStruct((rows, dxc), BF16),
        jax.ShapeDtypeStruct((nb, rows), BF16),
        jax.ShapeDtypeStruct((ndt, rows), F32),
    ]
    if want_z:
        out_specs.insert(0, pl.BlockSpec((tm, dz), lambda i: (i, 0)))
        out_shape.insert(0, jax.ShapeDtypeStruct((rows, dz), BF16))
    return pl.pallas_call(
        kern,
        grid=(rows // tm,),
        in_specs=[
            pl.BlockSpec((tm, d), lambda i: (i, 0)),
            pl.BlockSpec((None, 1, d), lambda i: (mod_row_of_tile(i), 0, 0)),
            pl.BlockSpec((None, 1, d), lambda i: (mod_row_of_tile(i), 0, 1)),
            _const_spec((1, d)),
            _const_spec(w_main.shape),
            _const_spec((nb, d)),
            _const_spec((ndt, d)),
            _const_spec((3, dxc)),
            _const_spec((2 * period, dxc)),
            _const_spec((1, dxc)),
            _const_spec(cwb.shape),
            _const_spec((ndt, 1)),
            _const_spec((ndt, 1)),
        ],
        out_specs=out_specs,
        out_shape=out_shape,
        compiler_params=_cparams(1),
        name="ssd_in_proj",
    )(x2d, mod3, mod3, norm_g, w_main, wbt, wdt_t, cw_xc, _edge_masked_taps(cw_xc, period), cb_xc, cwb,
      dtb_col, alog_col)


COL_CS, COL_ECS, COL_W = 0, 8, 16


def _head_expand(colmat, lane0):
    r = colmat.shape[0]
    first = lax.broadcasted_iota(jnp.int32, (r, SSM_STATE), 1) < SSM_HEAD_DIM
    cols = [colmat[:, lane0 + hd:lane0 + hd + 1] for hd in range(HEADS_PER_GROUP)]
    lo = jnp.where(first, cols[0], cols[1])
    hi = jnp.where(first, cols[2], cols[3])
    return jnp.concatenate([lo, hi], axis=1)


def _expand_select(lane0):
    r = np.arange(2 * SSM_STATE)[:, None] % SSM_STATE
    l = np.arange(2 * GROUP_WIDTH)[None, :]
    src = lane0 + (l // GROUP_WIDTH) * HEADS_PER_GROUP + (l % GROUP_WIDTH) // SSM_HEAD_DIM
    return jnp.asarray(r == src, dtype=BF16)


def _broadcast_select(lane0, count):
    r = np.arange(2 * SSM_STATE)[:, None] % SSM_STATE
    l = np.arange(SSM_STATE * count)[None, :]
    return jnp.asarray(r == lane0 + l // SSM_STATE, dtype=BF16)


def _hi_lo(colmat):
    hi = colmat.astype(BF16)
    lo = (colmat - hi.astype(F32)).astype(BF16)
    return jnp.concatenate([hi, lo], axis=1)


def _decay_rows(dt_ref, n_chunks, upper, lower):
    q = SSM_CHUNK
    nh = HEADS_PER_GROUP
    dt = jnp.concatenate([dt_ref[0:2 * nh, c * q:(c + 1) * q] for c in range(n_chunks)], axis=0)
    la = jnp.concatenate([dt_ref[2 * nh:4 * nh, c * q:(c + 1) * q] for c in range(n_chunks)], axis=0)
    csf = jnp.dot(la, upper, preferred_element_type=F32, precision=lax.Precision.HIGHEST)
    csb = jnp.dot(la, lower, preferred_element_type=F32, precision=lax.Precision.HIGHEST)
    rows = dt.shape[0]
    row = lax.broadcasted_iota(jnp.int32, (rows, 1), 0)
    is_b = jnp.bitwise_and(row, nh) != 0
    cs = jnp.where(is_b, csb, csf)
    tot = jnp.where(is_b, csb[:, 0:1], csf[:, q - 1:q])
    other = jnp.where(is_b, pltpu.roll(dt, nh, 0), pltpu.roll(dt, rows - nh, 0))
    return cs, jnp.exp(cs), jnp.exp(tot - cs) * dt, cs - jnp.log(dt), jnp.log(dt + other)


def _ssd_scan_kernel(xl_ref, btl_ref, cl_ref, dtl_ref, xc_ref, btc_ref, dtc_ref,
                     selw_ref, sele_ref, selc_ref, o_ref,
                     cols_ref, rows_ref, xw_ref, ee_ref, csb_ref, cb_ref, sloc_ref, hin_ref, etot_ref):
    q = SSM_CHUNK
    nh = HEADS_PER_GROUP
    gw = GROUP_WIDTH
    n_lat = xl_ref.shape[0] // q
    n_ctx = xc_ref.shape[0] // q
    ctx_rows = n_ctx * q
    ki = lax.broadcasted_iota(jnp.int32, (q, q), 0)
    kj = lax.broadcasted_iota(jnp.int32, (q, q), 1)

    upper = (ki <= kj).astype(F32)
    lower = (ki >= kj).astype(F32)
    pad = jnp.zeros((q - 6 * nh, q), F32)
    for dref, n_chunks, base in ((dtc_ref, n_ctx, 0), (dtl_ref, n_lat, n_ctx)):
        cs, ecs, w, rk, lds = _decay_rows(dref, n_chunks, upper, lower)
        for c in range(n_chunks):
            r = slice(2 * nh * c, 2 * nh * (c + 1))
            colmat = jnp.concatenate([cs[r], ecs[r], w[r], pad], axis=0).T
            cols_ref[(base + c) * q:(base + c + 1) * q, :] = colmat
            for d in range(2):
                p0 = 0 if d else q - 1
                etot_ref[d, base + c] = _head_expand(colmat[p0:p0 + 1, :], COL_ECS + nh * d)
        if base:
            rows_ref[0:2 * nh * n_lat, :] = rk
            rows_ref[2 * nh * n_lat:4 * nh * n_lat, :] = lds

    def expand(hl, sel_ref):
        return jnp.dot(hl, sel_ref[...], preferred_element_type=F32)

    xf = xc_ref[...].astype(F32)
    hl = _hi_lo(cols_ref[0:ctx_rows, :])
    xw_ref[0:ctx_rows, :] = (jnp.concatenate([xf, xf], axis=1) * expand(hl, selw_ref)).astype(BF16)
    blk = 4 * q
    for j in range(n_lat * q // blk):
        rows = slice(j * blk, (j + 1) * blk)
        xf = xl_ref[rows, :].astype(F32)
        r0 = ctx_rows + j * blk
        hl = _hi_lo(cols_ref[r0:r0 + blk, :])
        xw_ref[r0:r0 + blk, :] = (jnp.concatenate([xf, xf], axis=1) * expand(hl, selw_ref)).astype(BF16)
        ee_ref[rows, :] = expand(hl, sele_ref)
        csb_ref[rows, :] = expand(hl, selc_ref)

    for c in range(n_ctx):
        sloc_ref[c] = jnp.dot(btc_ref[:, c * q:(c + 1) * q], xw_ref[c * q:(c + 1) * q, :],
                              preferred_element_type=F32)

    def s_body(c, carry):
        r0 = pl.multiple_of(c * q, q)
        bt = btl_ref[:, pl.ds(r0, q)]
        sloc_ref[n_ctx + c] = jnp.dot(bt, xw_ref[pl.ds(ctx_rows + r0, q), :], preferred_element_type=F32)
        cb_ref[c] = jnp.dot(cl_ref[pl.ds(r0, q), :], bt, preferred_element_type=F32)
        return carry

    lax.fori_loop(0, n_lat, s_body, 0, unroll=4)

    for d in range(2):
        lanes = slice(gw * d, gw * (d + 1))
        h = jnp.zeros((SSM_STATE, gw), F32)
        for c in (range(n_ctx - 1, -1, -1) if d else range(n_ctx)):
            h = h * etot_ref[d, c] + sloc_ref[c, :, lanes]

        def b_body(i, h, d=d, lanes=lanes):
            c = (n_lat - 1 - i) if d else i
            hin_ref[c, :, lanes] = h.astype(BF16)
            return h * etot_ref[d, n_ctx + c] + sloc_ref[n_ctx + c, :, lanes]

        lax.fori_loop(0, n_lat, b_body, h, unroll=4)

    lane_head = lax.broadcasted_iota(jnp.int32, (q, gw), 1) // SSM_HEAD_DIM
    below = ki > kj
    above = ki < kj

    def c_body(c, carry):
        r0 = pl.multiple_of(c * q, q)
        x = xl_ref[pl.ds(r0, q), :]
        r8 = pl.multiple_of(c * 2 * nh, 2 * nh)
        rk8 = rows_ref[pl.ds(r8, 2 * nh), :]
        lds8 = rows_ref[pl.ds(2 * nh * n_lat + r8, 2 * nh), :]
        cb = cb_ref[c]
        colmat = cols_ref[pl.ds(ctx_rows + r0, q), :]
        m_parts = []
        x_parts = []
        for hd in range(nh):
            segf = csb_ref[pl.ds(r0, q), q * hd:q * (hd + 1)] - rk8[hd:hd + 1, :]
            segb = colmat[:, COL_CS + nh + hd:COL_CS + nh + hd + 1] - rk8[nh + hd:nh + hd + 1, :]
            arg = jnp.where(below, segf, jnp.where(above, segb, lds8[hd:hd + 1, :]))
            m_parts.append((jnp.exp(arg) * cb).astype(BF16))
            x_parts.append(jnp.where(lane_head == hd, x, jnp.zeros_like(x)))
        m_all = jnp.concatenate(m_parts, axis=1)
        x_bd = jnp.concatenate(x_parts, axis=0)
        y = jnp.dot(m_all, x_bd, preferred_element_type=F32)
        y_off = ee_ref[pl.ds(r0, q), :] * jnp.dot(cl_ref[pl.ds(r0, q), :], hin_ref[c],
                                                  preferred_element_type=F32)
        o_ref[pl.ds(r0, q), :] = (y + y_off[:, 0:gw] + y_off[:, gw:2 * gw]).astype(BF16)
        return carry

    lax.fori_loop(0, n_lat, c_body, 0, unroll=4)


def _ssd_scan(xc_l, bt_l, dt_l, xc_c, bt_c, dt_c, *, batch, seq, ctx_len):
    g = SSM_GROUPS
    gw = GROUP_WIDTH
    n = SSM_STATE
    c_off = (g * gw) // n
    n_lat = seq // SSM_CHUNK
    n_all = n_lat + ctx_len // SSM_CHUNK
    sel_w = _expand_select(COL_W)
    sel_e = _expand_select(COL_ECS)
    sel_c = _broadcast_select(COL_CS, HEADS_PER_GROUP)
    return pl.pallas_call(
        _ssd_scan_kernel,
        grid=(batch, g),
        in_specs=[
            pl.BlockSpec((seq, gw), lambda b, k: (b, k)),
            pl.BlockSpec((n, seq), lambda b, k: (k, b)),
            pl.BlockSpec((seq, n), lambda b, k: (b, c_off + k)),
            pl.BlockSpec((DT_ROWS_PER_GROUP, seq), lambda b, k: (k, b)),
            pl.BlockSpec((ctx_len, gw), lambda b, k: (b, k)),
            pl.BlockSpec((n, ctx_len), lambda b, k: (k, b)),
            pl.BlockSpec((DT_ROWS_PER_GROUP, ctx_len), lambda b, k: (k, b)),
            _const_spec(sel_w.shape),
            _const_spec(sel_e.shape),
            _const_spec(sel_c.shape),
        ],
        out_specs=pl.BlockSpec((seq, gw), lambda b, k: (b, k)),
        out_shape=jax.ShapeDtypeStruct((batch * seq, g * gw), BF16),
        scratch_shapes=[
            pltpu.VMEM((n_all * SSM_CHUNK, SSM_CHUNK), F32),
            pltpu.VMEM((4 * HEADS_PER_GROUP * n_lat, SSM_CHUNK), F32),
            pltpu.VMEM((n_all * SSM_CHUNK, 2 * gw), BF16),
            pltpu.VMEM((seq, 2 * gw), F32),
            pltpu.VMEM((seq, HEADS_PER_GROUP * SSM_CHUNK), F32),
            pltpu.VMEM((n_lat, SSM_CHUNK, SSM_CHUNK), F32),
            pltpu.VMEM((n_all, n, 2 * gw), F32),
            pltpu.VMEM((n_lat, n, 2 * gw), BF16),
            pltpu.VMEM((2, n_all, 1, gw), F32),
        ],
        compiler_params=_cparams(2),
        name="ssd_scan",
    )(xc_l, bt_l, xc_l, dt_l, xc_c, bt_c, dt_c, sel_w, sel_e, sel_c)


def _mlp_tail(x1, g_ref, sh_ref, sc_ref, gate_ref, w1_ref, w2_ref, nff):
    h2 = _norm_mod(x1, g_ref[...], sh_ref[...], sc_ref[...]).astype(BF16)
    dff = w1_ref.shape[1]
    acc = None
    for j in range(dff // nff):
        c0 = j * nff
        a = jnp.dot(h2, w1_ref[:, c0:c0 + nff], preferred_element_type=F32)
        a = jnp.square(jnp.maximum(a, 0.0)).astype(BF16)
        p = jnp.dot(a, w2_ref[c0:c0 + nff, :], preferred_element_type=F32)
        acc = p if acc is None else acc + p
    return x1 + gate_ref[...] * acc


def _ssd_out_kernel(y_ref, xs_ref, z_ref, x_ref, dsk_ref, sng_ref, gm_ref, shf_ref, scf_ref, gf_ref,
                    ng_ref, wo_ref, w1_ref, w2_ref, o_ref):
    acc = None
    for g0 in range(0, y_ref.shape[1], GROUP_WIDTH):
        cols = slice(g0, g0 + GROUP_WIDTH)
        y = y_ref[:, cols].astype(F32) + dsk_ref[:, cols] * xs_ref[:, cols].astype(F32)
        y = y * _silu(z_ref[:, cols].astype(F32))
        y = y * lax.rsqrt(jnp.mean(y * y, axis=-1, keepdims=True) + EPS)
        yn = (y * sng_ref[:, cols]).astype(BF16)
        p = jnp.dot(yn, wo_ref[cols, :], preferred_element_type=F32)
        acc = p if acc is None else acc + p
    x1 = x_ref[...] + gm_ref[...] * acc
    o_ref[...] = _mlp_tail(x1, ng_ref, shf_ref, scf_ref, gf_ref, w1_ref, w2_ref, 1024)


def _mod_spec(d, row_of_tile, k):
    return pl.BlockSpec((None, 1, d), lambda i: (row_of_tile(i), 0, k))


def _ssd_out_mlp(y2d, xc2d, z2d, x2d, dskip_row, ssd_ng_row, mod3, row_of_tile, norm_g, wo, w1, w2, *, tm):
    rows, d = x2d.shape
    di = y2d.shape[1]
    dff = w1.shape[1]
    return pl.pallas_call(
        _ssd_out_kernel,
        grid=(rows // tm,),
        in_specs=[
            pl.BlockSpec((tm, di), lambda i: (i, 0)),
            pl.BlockSpec((tm, di), lambda i: (i, 0)),
            pl.BlockSpec((tm, di), lambda i: (i, 0)),
            pl.BlockSpec((tm, d), lambda i: (i, 0)),
            _const_spec((1, di)),
            _const_spec((1, di)),
            _mod_spec(d, row_of_tile, 2),
            _mod_spec(d, row_of_tile, 3),
            _mod_spec(d, row_of_tile, 4),
            _mod_spec(d, row_of_tile, 5),
            _const_spec((1, d)),
            _const_spec((di, d)),
            _const_spec((d, dff)),
            _const_spec((dff, d)),
        ],
        out_specs=pl.BlockSpec((tm, d), lambda i: (i, 0)),
        out_shape=jax.ShapeDtypeStruct((rows, d), F32),
        compiler_params=_cparams(1),
        name="ssd_out_mlp",
    )(y2d, xc2d, z2d, x2d, dskip_row, ssd_ng_row, mod3, mod3, mod3, mod3, norm_g, wo, w1, w2)


def _sc_layer_kernel(x_ref, shm_ref, scm_ref, gm_ref, shf_ref, scf_ref, gf_ref, ngm_ref, ngf_ref,
                     fg_ref, wi_ref, cw_ref, cwm_ref, wo_ref, w1_ref, w2_ref, o_ref, *, period):
    x = x_ref[...]
    h = _norm_mod(x, ngm_ref[...], shm_ref[...], scm_ref[...]).astype(BF16)
    w = wo_ref.shape[0]
    bg = jnp.dot(h, wi_ref[:, 0:w], preferred_element_type=F32)
    cg = jnp.dot(h, wi_ref[:, w:2 * w], preferred_element_type=F32)
    xv = jnp.dot(h, wi_ref[:, 2 * w:3 * w], preferred_element_type=F32)
    u = (bg * _row_conv3(cg * xv, cw_ref, cwm_ref, slice(0, w), period)).astype(BF16)
    y = jnp.dot(u, wo_ref[...], preferred_element_type=F32)
    x1 = x + gm_ref[...] * y
    x2 = _mlp_tail(x1, ngf_ref, shf_ref, scf_ref, gf_ref, w1_ref, w2_ref, 1024)
    ms = jnp.mean(x2 * x2, axis=-1, keepdims=True)
    o_ref[...] = x2 * lax.rsqrt(ms + EPS) * fg_ref[...]


def _sc_layer(x2d, mod3, row_of_tile, ng_mix, ng_mlp, final_g, wi, conv_w, wo, w1, w2, *, tm, period):
    rows, d = x2d.shape
    dff = w1.shape[1]
    kern = functools.partial(_sc_layer_kernel, period=period)
    return pl.pallas_call(
        kern,
        grid=(rows // tm,),
        in_specs=[
            pl.BlockSpec((tm, d), lambda i: (i, 0)),
            _mod_spec(d, row_of_tile, 0),
            _mod_spec(d, row_of_tile, 1),
            _mod_spec(d, row_of_tile, 2),
            _mod_spec(d, row_of_tile, 3),
            _mod_spec(d, row_of_tile, 4),
            _mod_spec(d, row_of_tile, 5),
            _const_spec((1, d)),
            _const_spec((1, d)),
            _const_spec((1, d)),
            _const_spec(wi.shape),
            _const_spec(conv_w.shape),
            _const_spec((2 * period, conv_w.shape[1])),
            _const_spec(wo.shape),
            _const_spec((d, dff)),
            _const_spec((dff, d)),
        ],
        out_specs=pl.BlockSpec((tm, d), lambda i: (i, 0)),
        out_shape=jax.ShapeDtypeStruct((rows, d), F32),
        compiler_params=_cparams(1),
        name="shortconv_layer",
    )(x2d, mod3, mod3, mod3, mod3, mod3, mod3, ng_mix, ng_mlp, final_g, wi, conv_w,
      _edge_masked_taps(conv_w, period), wo, w1, w2)


def kernel(x, c, ctx, c_ctx, ada_w, ada_b, norm_mix_g, norm_mlp_g, ssd_w_in, ssd_conv_w, ssd_conv_b,
           ssd_dt_bias, ssd_a_log, ssd_d, ssd_norm_g, ssd_w_out, sc_w_in, sc_conv_w, sc_w_out,
           mlp_w1, mlp_w2, final_norm_g):
    batch, seq, d = x.shape
    ctx_len = ctx.shape[1]
    depth = ada_w.shape[0]
    assert depth == 2 and ssd_w_in.shape[0] == 1 and sc_w_in.shape[0] == 1
    d_inner = ssd_w_out.shape[1]
    n_heads = ssd_d.shape[1]
    xbc_dim = ssd_conv_w.shape[2]
    assert n_heads == SSM_GROUPS * HEADS_PER_GROUP and d_inner == SSM_GROUPS * GROUP_WIDTH

    mod_rows = 16
    cvec = jnp.zeros((mod_rows, d), F32).at[:batch].set(c).at[batch].set(c_ctx)
    mod = _modulation(cvec, ada_w, ada_b)
    mod0 = mod[0].reshape(mod_rows, 1, 6 * d)
    mod1 = mod[1].reshape(mod_rows, 1, 6 * d)

    nbc = SSM_GROUPS * SSM_STATE
    x_rng = (d_inner, 2 * d_inner)
    c_rng = (2 * d_inner + nbc, 2 * d_inner + 2 * nbc)
    w_main = ssd_w_in[0].astype(BF16)
    wbt = w_main[:, 2 * d_inner:2 * d_inner + nbc].T
    cw = ssd_conv_w[0]
    cbias = ssd_conv_b[0]
    cw_xc = jnp.concatenate([cw[:, :d_inner], cw[:, d_inner + nbc:]], axis=1)
    cb_xc = jnp.concatenate([cbias[:d_inner], cbias[d_inner + nbc:]]).reshape(1, -1)
    cwb = jnp.concatenate([cw[:, d_inner:d_inner + nbc].T, cbias[d_inner:d_inner + nbc, None],
                           jnp.zeros((nbc, 4), F32)], axis=1)
    gi = jnp.arange(SSM_GROUPS)[:, None, None]
    di_ = jnp.arange(2)[None, :, None]
    ri = jnp.arange(HEADS_PER_GROUP)[None, None, :]
    flat = (di_ * n_heads + gi * HEADS_PER_GROUP + ri).reshape(SSM_GROUPS, 2 * HEADS_PER_GROUP)
    dt_idx = jnp.concatenate([flat, flat], axis=1).reshape(-1)
    wdt_t = w_main[:, d_inner + xbc_dim:][:, dt_idx].T
    dtb_col = ssd_dt_bias[0].reshape(-1)[dt_idx].reshape(-1, 1).astype(F32)
    alog_col = ssd_a_log[0].reshape(-1)[dt_idx].reshape(-1, 1).astype(F32)
    ng_mix0 = norm_mix_g[0].reshape(1, d)

    tm = 512
    x2d = x.reshape(batch * seq, d)
    ctx2d = ctx.reshape(batch * ctx_len, d)
    lat_row = lambda i: (i * tm) // seq
    tm_in = 256
    z_l, xc_l, bt_l, dt_l = _ssd_in_proj(x2d, mod0, lambda i: (i * tm_in) // seq, ng_mix0, w_main, wbt,
                                         wdt_t, cw_xc, cb_xc, cwb, dtb_col, alog_col, tm=tm_in,
                                         period=GRID_W, want_z=True, dz=d_inner, xc_ranges=(x_rng, c_rng))
    xc_c, bt_c, dt_c = _ssd_in_proj(ctx2d, mod0, lambda i: batch, ng_mix0, w_main, wbt, wdt_t,
                                    cw_xc[:, :d_inner], cb_xc[:, :d_inner], cwb, dtb_col, alog_col,
                                    tm=ctx_len, period=ctx_len, want_z=False, dz=d_inner,
                                    xc_ranges=(x_rng,))

    dskip_row = jnp.repeat(ssd_d[0].astype(F32), SSM_HEAD_DIM).reshape(1, d_inner)
    ng_row = ssd_norm_g[0].reshape(1, d_inner)
    y_ssd = _ssd_scan(xc_l, bt_l, dt_l, xc_c, bt_c, dt_c, batch=batch, seq=seq, ctx_len=ctx_len)

    x1 = _ssd_out_mlp(y_ssd, xc_l, z_l, x2d, dskip_row, ng_row, mod0, lat_row, norm_mlp_g[0].reshape(1, d),
                      ssd_w_out[0].astype(BF16), mlp_w1[0].astype(BF16), mlp_w2[0].astype(BF16), tm=tm)

    out = _sc_layer(x1, mod1, lat_row, norm_mix_g[1].reshape(1, d), norm_mlp_g[1].reshape(1, d),
                    final_norm_g.reshape(1, d), sc_w_in[0].astype(BF16), sc_conv_w[0],
                    sc_w_out[0].astype(BF16), mlp_w1[1].astype(BF16), mlp_w2[1].astype(BF16),
                    tm=tm, period=GRID_W)
    return out.reshape(batch, seq, d)
```

```python
import functools

import numpy as np

import jax
import jax.numpy as jnp
from jax import lax
from jax.experimental import pallas as pl
from jax.experimental.pallas import tpu as pltpu

F32 = jnp.float32
BF16 = jnp.bfloat16

EPS = 1e-6
GRID_W = 64
SSM_HEAD_DIM = 64
SSM_GROUPS = 8
HEADS_PER_GROUP = 4
SSM_STATE = 128
SSM_CHUNK = 128
GROUP_WIDTH = HEADS_PER_GROUP * SSM_HEAD_DIM
DT_ROWS_PER_GROUP = 16
NEG_BIG = -1e30

VMEM_LIMIT_BYTES = 56 * 1024 * 1024


def _cparams(n_axes):
    return pltpu.CompilerParams(
        dimension_semantics=("arbitrary",) * n_axes,
        vmem_limit_bytes=VMEM_LIMIT_BYTES,
    )


def _const_spec(shape):
    nd = len(shape)
    return pl.BlockSpec(shape, lambda *_: (0,) * nd, pipeline_mode=pl.Buffered(1))


def _silu(u):
    return u * (1.0 / (1.0 + jnp.exp(-u)))


def _norm_mod(x, g, shift, scale):
    ms = jnp.mean(x * x, axis=-1, keepdims=True)
    y = x * lax.rsqrt(ms + EPS) * g
    return y * (1.0 + scale) + shift


def _edge_masked_taps(conv_w, period):
    r = jnp.arange(period)[:, None]
    return jnp.concatenate([jnp.where(r != 0, conv_w[0][None, :], 0.0),
                            jnp.where(r != period - 1, conv_w[2][None, :], 0.0)], axis=0)


def _row_conv3(u, w_ref, wm_ref, cols, period):
    rows = u.shape[0]
    reps = rows // period
    w_prev = jnp.concatenate([wm_ref[0:period, cols]] * reps, axis=0)
    w_next = jnp.concatenate([wm_ref[period:2 * period, cols]] * reps, axis=0)
    return pltpu.roll(u, 1, 0) * w_prev + u * w_ref[1:2, cols] + pltpu.roll(u, rows - 1, 0) * w_next


def _mod_kernel(c_ref, w_ref, b_ref, o_ref):
    s = _silu(c_ref[...]).astype(BF16)
    o_ref[...] = jnp.dot(s, w_ref[...].astype(BF16), preferred_element_type=F32) + b_ref[...]


def _modulation(cvec, ada_w, ada_b):
    depth, d, n = ada_w.shape
    rows = cvec.shape[0]
    tn = 1536
    return pl.pallas_call(
        _mod_kernel,
        grid=(depth, n // tn),
        in_specs=[
            pl.BlockSpec((rows, d), lambda i, j: (0, 0)),
            pl.BlockSpec((None, d, tn), lambda i, j: (i, 0, j)),
            pl.BlockSpec((None, 1, tn), lambda i, j: (i, 0, j)),
        ],
        out_specs=pl.BlockSpec((None, rows, tn), lambda i, j: (i, 0, j)),
        out_shape=jax.ShapeDtypeStruct((depth, rows, n), F32),
        compiler_params=_cparams(2),
        name="adaln_mod",
    )(cvec, ada_w, ada_b.reshape(depth, 1, n))


_NT_DIMS = (((1,), (1,)), ((), ()))


def _ssd_in_kernel(x_ref, sh_ref, sc_ref, g_ref, w_ref, wdt_ref, cw_ref, cwm_ref, cb_ref,
                   cwb_ref, dtb_ref, alog_ref, *out_refs, period, ncol, nrow, want_z, xc_starts, b_start):
    if want_z:
        z_ref, xc_ref, bt_ref, dt_ref = out_refs
    else:
        xc_ref, bt_ref, dt_ref = out_refs
    h = _norm_mod(x_ref[...], g_ref[...], sh_ref[...], sc_ref[...]).astype(BF16)
    if want_z:
        z_ref[...] = lax.dot_general(h, w_ref[0:z_ref.shape[1], :], _NT_DIMS,
                                     preferred_element_type=F32).astype(BF16)
    for j, w0 in enumerate(xc_starts):
        cols = slice(j * ncol, (j + 1) * ncol)
        u = lax.dot_general(h, w_ref[w0:w0 + ncol, :], _NT_DIMS, preferred_element_type=F32)
        u = _row_conv3(u, cw_ref, cwm_ref, cols, period) + cb_ref[:, cols]
        xc_ref[:, cols] = _silu(u).astype(BF16)
    tm = x_ref.shape[0]
    pos = jnp.bitwise_and(lax.broadcasted_iota(jnp.int32, (1, tm), 1), period - 1)
    for j in range(bt_ref.shape[0] // nrow):
        rows = slice(j * nrow, (j + 1) * nrow)
        u = lax.dot_general(w_ref[b_start + j * nrow:b_start + (j + 1) * nrow, :], h, _NT_DIMS,
                            preferred_element_type=F32)
        prev = jnp.where(pos != 0, pltpu.roll(u, 1, 1), 0.0)
        nxt = jnp.where(pos != period - 1, pltpu.roll(u, tm - 1, 1), 0.0)
        cwb = cwb_ref[rows, :]
        u = prev * cwb[:, 0:1] + u * cwb[:, 1:2] + nxt * cwb[:, 2:3] + cwb[:, 3:4]
        bt_ref[rows, :] = _silu(u).astype(BF16)
    raw = lax.dot_general(wdt_ref[...], h, _NT_DIMS, preferred_element_type=F32)
    v = raw + dtb_ref[...]
    sp = jnp.maximum(v, 0.0) + jnp.log1p(jnp.exp(-jnp.abs(v)))
    row = lax.broadcasted_iota(jnp.int32, (dt_ref.shape[0], 1), 0)
    is_la = jnp.bitwise_and(row, DT_ROWS_PER_GROUP - 1) >= DT_ROWS_PER_GROUP // 2
    dt_ref[...] = jnp.where(is_la, sp * (-jnp.exp(alog_ref[...])), sp)


def _ssd_in_proj(x2d, mod3, mod_row_of_tile, norm_g, w_t, wdt_t, cw_xc, cb_xc, cwb, dtb_col,
                 alog_col, *, tm, period, want_z, dz, xc_ranges, b_range):
    rows, d = x2d.shape
    dxc = cw_xc.shape[1]
    nb = b_range[1] - b_range[0]
    ndt = wdt_t.shape[0]
    ncol, nrow = 512, 256
    xc_starts = tuple(c for lo, hi in xc_ranges for c in range(lo, hi, ncol))
    assert len(xc_starts) * ncol == dxc
    kern = functools.partial(_ssd_in_kernel, period=period, ncol=ncol, nrow=nrow, want_z=want_z,
                             xc_starts=xc_starts, b_start=b_range[0])
    out_specs = [
        pl.BlockSpec((tm, dxc), lambda i: (i, 0)),
        pl.BlockSpec((nb, tm), lambda i: (0, i)),
        pl.BlockSpec((ndt, tm), lambda i: (0, i)),
    ]
    out_shape = [
        jax.ShapeDtypeStruct((rows, dxc), BF16),
        jax.ShapeDtypeStruct((nb, rows), BF16),
        jax.ShapeDtypeStruct((ndt, rows), F32),
    ]
    if want_z:
        out_specs.insert(0, pl.BlockSpec((tm, dz), lambda i: (i, 0)))
        out_shape.insert(0, jax.ShapeDtypeStruct((rows, dz), BF16))
    return pl.pallas_call(
        kern,
        grid=(rows // tm,),
        in_specs=[
            pl.BlockSpec((tm, d), lambda i: (i, 0)),
            pl.BlockSpec((None, 1, d), lambda i: (mod_row_of_tile(i), 0, 0)),
            pl.BlockSpec((None, 1, d), lambda i: (mod_row_of_tile(i), 0, 1)),
            _const_spec((1, d)),
            _const_spec(w_t.shape),
            _const_spec((ndt, d)),
            _const_spec((3, dxc)),
            _const_spec((2 * period, dxc)),
            _const_spec((1, dxc)),
            _const_spec(cwb.shape),
            _const_spec((ndt, 1)),
            _const_spec((ndt, 1)),
        ],
        out_specs=out_specs,
        out_shape=out_shape,
        compiler_params=_cparams(1),
        name="ssd_in_proj",
    )(x2d, mod3, mod3, norm_g, w_t, wdt_t, cw_xc, _edge_masked_taps(cw_xc, period), cb_xc, cwb,
      dtb_col, alog_col)


COL_CS, COL_ECS, COL_W = 0, 8, 16


def _head_expand(colmat, lane0):
    r = colmat.shape[0]
    first = lax.broadcasted_iota(jnp.int32, (r, SSM_STATE), 1) < SSM_HEAD_DIM
    cols = [colmat[:, lane0 + hd:lane0 + hd + 1] for hd in range(HEADS_PER_GROUP)]
    lo = jnp.where(first, cols[0], cols[1])
    hi = jnp.where(first, cols[2], cols[3])
    return jnp.concatenate([lo, hi], axis=1)


def _expand_select(lane0):
    r = np.arange(2 * SSM_STATE)[:, None] % SSM_STATE
    l = np.arange(2 * GROUP_WIDTH)[None, :]
    src = lane0 + (l // GROUP_WIDTH) * HEADS_PER_GROUP + (l % GROUP_WIDTH) // SSM_HEAD_DIM
    return jnp.asarray(r == src, dtype=BF16)


def _broadcast_select(lane0, count):
    r = np.arange(2 * SSM_STATE)[:, None] % SSM_STATE
    l = np.arange(SSM_STATE * count)[None, :]
    return jnp.asarray(r == lane0 + l // SSM_STATE, dtype=BF16)


def _hi_lo(colmat):
    hi = colmat.astype(BF16)
    lo = (colmat - hi.astype(F32)).astype(BF16)
    return jnp.concatenate([hi, lo], axis=1)


def _decay_rows(dt_ref, n_chunks, upper, lower):
    q = SSM_CHUNK
    nh = HEADS_PER_GROUP
    dt = jnp.concatenate([dt_ref[0:2 * nh, c * q:(c + 1) * q] for c in range(n_chunks)], axis=0)
    la = jnp.concatenate([dt_ref[2 * nh:4 * nh, c * q:(c + 1) * q] for c in range(n_chunks)], axis=0)
    csf = jnp.dot(la, upper, preferred_element_type=F32, precision=lax.Precision.HIGHEST)
    csb = jnp.dot(la, lower, preferred_element_type=F32, precision=lax.Precision.HIGHEST)
    rows = dt.shape[0]
    row = lax.broadcasted_iota(jnp.int32, (rows, 1), 0)
    is_b = jnp.bitwise_and(row, nh) != 0
    cs = jnp.where(is_b, csb, csf)
    tot = jnp.where(is_b, csb[:, 0:1], csf[:, q - 1:q])
    other = jnp.where(is_b, pltpu.roll(dt, nh, 0), pltpu.roll(dt, rows - nh, 0))
    return cs, jnp.exp(cs), jnp.exp(tot - cs) * dt, cs - jnp.log(dt), jnp.log(dt + other)


def _ssd_scan_kernel(xl_ref, btl_ref, cl_ref, dtl_ref, xc_ref, btc_ref, dtc_ref,
                     selw_ref, sele_ref, selc_ref, o_ref,
                     cols_ref, rows_ref, xw_ref, ee_ref, csb_ref, cb_ref, sloc_ref, hin_ref, etot_ref):
    q = SSM_CHUNK
    nh = HEADS_PER_GROUP
    gw = GROUP_WIDTH
    n_lat = xl_ref.shape[0] // q
    n_ctx = xc_ref.shape[0] // q
    ctx_rows = n_ctx * q
    ki = lax.broadcasted_iota(jnp.int32, (q, q), 0)
    kj = lax.broadcasted_iota(jnp.int32, (q, q), 1)

    upper = (ki <= kj).astype(F32)
    lower = (ki >= kj).astype(F32)
    pad = jnp.zeros((q - 6 * nh, q), F32)
    for dref, n_chunks, base in ((dtc_ref, n_ctx, 0), (dtl_ref, n_lat, n_ctx)):
        cs, ecs, w, rk, lds = _decay_rows(dref, n_chunks, upper, lower)
        for c in range(n_chunks):
            r = slice(2 * nh * c, 2 * nh * (c + 1))
            colmat = jnp.concatenate([cs[r], ecs[r], w[r], pad], axis=0).T
            cols_ref[(base + c) * q:(base + c + 1) * q, :] = colmat
            for d in range(2):
                p0 = 0 if d else q - 1
                etot_ref[d, base + c] = _head_expand(colmat[p0:p0 + 1, :], COL_ECS + nh * d)
        if base:
            rows_ref[0:2 * nh * n_lat, :] = rk
            rows_ref[2 * nh * n_lat:4 * nh * n_lat, :] = lds

    def expand(hl, sel_ref):
        return jnp.dot(hl, sel_ref[...], preferred_element_type=F32)

    xf = xc_ref[...].astype(F32)
    hl = _hi_lo(cols_ref[0:ctx_rows, :])
    xw_ref[0:ctx_rows, :] = (jnp.concatenate([xf, xf], axis=1) * expand(hl, selw_ref)).astype(BF16)
    blk = 4 * q
    for j in range(n_lat * q // blk):
        rows = slice(j * blk, (j + 1) * blk)
        xf = xl_ref[rows, :].astype(F32)
        r0 = ctx_rows + j * blk
        hl = _hi_lo(cols_ref[r0:r0 + blk, :])
        xw_ref[r0:r0 + blk, :] = (jnp.concatenate([xf, xf], axis=1) * expand(hl, selw_ref)).astype(BF16)
        ee_ref[rows, :] = expand(hl, sele_ref)
        csb_ref[rows, :] = expand(hl, selc_ref)

    for c in range(n_ctx):
        sloc_ref[c] = jnp.dot(btc_ref[:, c * q:(c + 1) * q], xw_ref[c * q:(c + 1) * q, :],
                              preferred_element_type=F32)

    def s_body(c, carry):
        r0 = pl.multiple_of(c * q, q)
        bt = btl_ref[:, pl.ds(r0, q)]
        sloc_ref[n_ctx + c] = jnp.dot(bt, xw_ref[pl.ds(ctx_rows + r0, q), :], preferred_element_type=F32)
        cb_ref[c] = jnp.dot(cl_ref[pl.ds(r0, q), :], bt, preferred_element_type=F32)
        return carry

    lax.fori_loop(0, n_lat, s_body, 0, unroll=4)

    for d in range(2):
        lanes = slice(gw * d, gw * (d + 1))
        h = jnp.zeros((SSM_STATE, gw), F32)
        for c in (range(n_ctx - 1, -1, -1) if d else range(n_ctx)):
            h = h * etot_ref[d, c] + sloc_ref[c, :, lanes]

        def b_body(i, h, d=d, lanes=lanes):
            c = (n_lat - 1 - i) if d else i
            hin_ref[c, :, lanes] = h.astype(BF16)
            return h * etot_ref[d, n_ctx + c] + sloc_ref[n_ctx + c, :, lanes]

        lax.fori_loop(0, n_lat, b_body, h, unroll=4)

    lane_head = lax.broadcasted_iota(jnp.int32, (q, gw), 1) // SSM_HEAD_DIM
    below = ki > kj
    above = ki < kj

    def c_body(c, carry):
        r0 = pl.multiple_of(c * q, q)
        x = xl_ref[pl.ds(r0, q), :]
        r8 = pl.multiple_of(c * 2 * nh, 2 * nh)
        rk8 = rows_ref[pl.ds(r8, 2 * nh), :]
        lds8 = rows_ref[pl.ds(2 * nh * n_lat + r8, 2 * nh), :]
        cb = cb_ref[c]
        colmat = cols_ref[pl.ds(ctx_rows + r0, q), :]
        m_parts = []
        x_parts = []
        for hd in range(nh):
            segf = csb_ref[pl.ds(r0, q), q * hd:q * (hd + 1)] - rk8[hd:hd + 1, :]
            segb = colmat[:, COL_CS + nh + hd:COL_CS + nh + hd + 1] - rk8[nh + hd:nh + hd + 1, :]
            arg = jnp.where(below, segf, jnp.where(above, segb, lds8[hd:hd + 1, :]))
            m_parts.append((jnp.exp(arg) * cb).astype(BF16))
            x_parts.append(jnp.where(lane_head == hd, x, jnp.zeros_like(x)))
        m_all = jnp.concatenate(m_parts, axis=1)
        x_bd = jnp.concatenate(x_parts, axis=0)
        y = jnp.dot(m_all, x_bd, preferred_element_type=F32)
        y_off = ee_ref[pl.ds(r0, q), :] * jnp.dot(cl_ref[pl.ds(r0, q), :], hin_ref[c],
                                                  preferred_element_type=F32)
        o_ref[pl.ds(r0, q), :] = (y + y_off[:, 0:gw] + y_off[:, gw:2 * gw]).astype(BF16)
        return carry

    lax.fori_loop(0, n_lat, c_body, 0, unroll=4)


def _ssd_scan(xc_l, bt_l, dt_l, xc_c, bt_c, dt_c, *, batch, seq, ctx_len):
    g = SSM_GROUPS
    gw = GROUP_WIDTH
    n = SSM_STATE
    c_off = (g * gw) // n
    n_lat = seq // SSM_CHUNK
    n_all = n_lat + ctx_len // SSM_CHUNK
    sel_w = _expand_select(COL_W)
    sel_e = _expand_select(COL_ECS)
    sel_c = _broadcast_select(COL_CS, HEADS_PER_GROUP)
    return pl.pallas_call(
        _ssd_scan_kernel,
        grid=(batch, g),
        in_specs=[
            pl.BlockSpec((seq, gw), lambda b, k: (b, k)),
            pl.BlockSpec((n, seq), lambda b, k: (k, b)),
            pl.BlockSpec((seq, n), lambda b, k: (b, c_off + k)),
            pl.BlockSpec((DT_ROWS_PER_GROUP, seq), lambda b, k: (k, b)),
            pl.BlockSpec((ctx_len, gw), lambda b, k: (b, k)),
            pl.BlockSpec((n, ctx_len), lambda b, k: (k, b)),
            pl.BlockSpec((DT_ROWS_PER_GROUP, ctx_len), lambda b, k: (k, b)),
            _const_spec(sel_w.shape),
            _const_spec(sel_e.shape),
            _const_spec(sel_c.shape),
        ],
        out_specs=pl.BlockSpec((seq, gw), lambda b, k: (b, k)),
        out_shape=jax.ShapeDtypeStruct((batch * seq, g * gw), BF16),
        scratch_shapes=[
            pltpu.VMEM((n_all * SSM_CHUNK, SSM_CHUNK), F32),
            pltpu.VMEM((4 * HEADS_PER_GROUP * n_lat, SSM_CHUNK), F32),
            pltpu.VMEM((n_all * SSM_CHUNK, 2 * gw), BF16),
            pltpu.VMEM((seq, 2 * gw), F32),
            pltpu.VMEM((seq, HEADS_PER_GROUP * SSM_CHUNK), F32),
            pltpu.VMEM((n_lat, SSM_CHUNK, SSM_CHUNK), F32),
            pltpu.VMEM((n_all, n, 2 * gw), F32),
            pltpu.VMEM((n_lat, n, 2 * gw), BF16),
            pltpu.VMEM((2, n_all, 1, gw), F32),
        ],
        compiler_params=_cparams(2),
        name="ssd_scan",
    )(xc_l, bt_l, xc_l, dt_l, xc_c, bt_c, dt_c, sel_w, sel_e, sel_c)


def _mlp_tail(x1, g_ref, sh_ref, sc_ref, gate_ref, w1_ref, w2_ref, nff):
    h2 = _norm_mod(x1, g_ref[...], sh_ref[...], sc_ref[...]).astype(BF16)
    dff = w1_ref.shape[1]
    acc = None
    for j in range(dff // nff):
        c0 = j * nff
        a = jnp.dot(h2, w1_ref[:, c0:c0 + nff], preferred_element_type=F32)
        a = jnp.square(jnp.maximum(a, 0.0)).astype(BF16)
        p = jnp.dot(a, w2_ref[c0:c0 + nff, :], preferred_element_type=F32)
        acc = p if acc is None else acc + p
    return x1 + gate_ref[...] * acc


def _ssd_out_kernel(y_ref, xs_ref, z_ref, x_ref, dsk_ref, sng_ref, gm_ref, shf_ref, scf_ref, gf_ref,
                    ng_ref, wo_ref, w1_ref, w2_ref, o_ref):
    acc = None
    for g0 in range(0, y_ref.shape[1], GROUP_WIDTH):
        cols = slice(g0, g0 + GROUP_WIDTH)
        y = y_ref[:, cols].astype(F32) + dsk_ref[:, cols] * xs_ref[:, cols].astype(F32)
        y = y * _silu(z_ref[:, cols].astype(F32))
        y = y * lax.rsqrt(jnp.mean(y * y, axis=-1, keepdims=True) + EPS)
        yn = (y * sng_ref[:, cols]).astype(BF16)
        p = jnp.dot(yn, wo_ref[cols, :], preferred_element_type=F32)
        acc = p if acc is None else acc + p
    x1 = x_ref[...] + gm_ref[...] * acc
    o_ref[...] = _mlp_tail(x1, ng_ref, shf_ref, scf_ref, gf_ref, w1_ref, w2_ref, 1024)


def _mod_spec(d, row_of_tile, k):
    return pl.BlockSpec((None, 1, d), lambda i: (row_of_tile(i), 0, k))


def _ssd_out_mlp(y2d, xc2d, z2d, x2d, dskip_row, ssd_ng_row, mod3, row_of_tile, norm_g, wo, w1, w2, *, tm):
    rows, d = x2d.shape
    di = y2d.shape[1]
    dff = w1.shape[1]
    return pl.pallas_call(
        _ssd_out_kernel,
        grid=(rows // tm,),
        in_specs=[
            pl.BlockSpec((tm, di), lambda i: (i, 0)),
            pl.BlockSpec((tm, di), lambda i: (i, 0)),
            pl.BlockSpec((tm, di), lambda i: (i, 0)),
            pl.BlockSpec((tm, d), lambda i: (i, 0)),
            _const_spec((1, di)),
            _const_spec((1, di)),
            _mod_spec(d, row_of_tile, 2),
            _mod_spec(d, row_of_tile, 3),
            _mod_spec(d, row_of_tile, 4),
            _mod_spec(d, row_of_tile, 5),
            _const_spec((1, d)),
            _const_spec((di, d)),
            _const_spec((d, dff)),
            _const_spec((dff, d)),
        ],
        out_specs=pl.BlockSpec((tm, d), lambda i: (i, 0)),
        out_shape=jax.ShapeDtypeStruct((rows, d), F32),
        compiler_params=_cparams(1),
        name="ssd_out_mlp",
    )(y2d, xc2d, z2d, x2d, dskip_row, ssd_ng_row, mod3, mod3, mod3, mod3, norm_g, wo, w1, w2)


def _sc_layer_kernel(x_ref, shm_ref, scm_ref, gm_ref, shf_ref, scf_ref, gf_ref, ngm_ref, ngf_ref,
                     fg_ref, wi_ref, cw_ref, cwm_ref, wo_ref, w1_ref, w2_ref, o_ref, *, period):
    x = x_ref[...]
    h = _norm_mod(x, ngm_ref[...], shm_ref[...], scm_ref[...]).astype(BF16)
    w = wo_ref.shape[0]
    bg = jnp.dot(h, wi_ref[:, 0:w], preferred_element_type=F32)
    cg = jnp.dot(h, wi_ref[:, w:2 * w], preferred_element_type=F32)
    xv = jnp.dot(h, wi_ref[:, 2 * w:3 * w], preferred_element_type=F32)
    u = (bg * _row_conv3(cg * xv, cw_ref, cwm_ref, slice(0, w), period)).astype(BF16)
    y = jnp.dot(u, wo_ref[...], preferred_element_type=F32)
    x1 = x + gm_ref[...] * y
    x2 = _mlp_tail(x1, ngf_ref, shf_ref, scf_ref, gf_ref, w1_ref, w2_ref, 1024)
    ms = jnp.mean(x2 * x2, axis=-1, keepdims=True)
    o_ref[...] = x2 * lax.rsqrt(ms + EPS) * fg_ref[...]


def _sc_layer(x2d, mod3, row_of_tile, ng_mix, ng_mlp, final_g, wi, conv_w, wo, w1, w2, *, tm, period):
    rows, d = x2d.shape
    dff = w1.shape[1]
    kern = functools.partial(_sc_layer_kernel, period=period)
    return pl.pallas_call(
        kern,
        grid=(rows // tm,),
        in_specs=[
            pl.BlockSpec((tm, d), lambda i: (i, 0)),
            _mod_spec(d, row_of_tile, 0),
            _mod_spec(d, row_of_tile, 1),
            _mod_spec(d, row_of_tile, 2),
            _mod_spec(d, row_of_tile, 3),
            _mod_spec(d, row_of_tile, 4),
            _mod_spec(d, row_of_tile, 5),
            _const_spec((1, d)),
            _const_spec((1, d)),
            _const_spec((1, d)),
            _const_spec(wi.shape),
            _const_spec(conv_w.shape),
            _const_spec((2 * period, conv_w.shape[1])),
            _const_spec(wo.shape),
            _const_spec((d, dff)),
            _const_spec((dff, d)),
        ],
        out_specs=pl.BlockSpec((tm, d), lambda i: (i, 0)),
        out_shape=jax.ShapeDtypeStruct((rows, d), F32),
        compiler_params=_cparams(1),
        name="shortconv_layer",
    )(x2d, mod3, mod3, mod3, mod3, mod3, mod3, ng_mix, ng_mlp, final_g, wi, conv_w,
      _edge_masked_taps(conv_w, period), wo, w1, w2)


def kernel(x, c, ctx, c_ctx, ada_w, ada_b, norm_mix_g, norm_mlp_g, ssd_w_in, ssd_conv_w, ssd_conv_b,
           ssd_dt_bias, ssd_a_log, ssd_d, ssd_norm_g, ssd_w_out, sc_w_in, sc_conv_w, sc_w_out,
           mlp_w1, mlp_w2, final_norm_g):
    batch, seq, d = x.shape
    ctx_len = ctx.shape[1]
    depth = ada_w.shape[0]
    assert depth == 2 and ssd_w_in.shape[0] == 1 and sc_w_in.shape[0] == 1
    d_inner = ssd_w_out.shape[1]
    n_heads = ssd_d.shape[1]
    xbc_dim = ssd_conv_w.shape[2]
    assert n_heads == SSM_GROUPS * HEADS_PER_GROUP and d_inner == SSM_GROUPS * GROUP_WIDTH

    mod_rows = 16
    cvec = jnp.zeros((mod_rows, d), F32).at[:batch].set(c).at[batch].set(c_ctx)
    mod = _modulation(cvec, ada_w, ada_b)
    mod0 = mod[0].reshape(mod_rows, 1, 6 * d)
    mod1 = mod[1].reshape(mod_rows, 1, 6 * d)

    nbc = SSM_GROUPS * SSM_STATE
    x_rng = (d_inner, 2 * d_inner)
    b_rng = (2 * d_inner, 2 * d_inner + nbc)
    c_rng = (2 * d_inner + nbc, 2 * d_inner + 2 * nbc)
    w_t = jnp.swapaxes(ssd_w_in[0], 0, 1).astype(BF16)
    cw = ssd_conv_w[0]
    cbias = ssd_conv_b[0]
    cw_xc = jnp.concatenate([cw[:, :d_inner], cw[:, d_inner + nbc:]], axis=1)
    cb_xc = jnp.concatenate([cbias[:d_inner], cbias[d_inner + nbc:]]).reshape(1, -1)
    cwb = jnp.concatenate([cw[:, d_inner:d_inner + nbc].T, cbias[d_inner:d_inner + nbc, None],
                           jnp.zeros((nbc, 4), F32)], axis=1)
    gi = jnp.arange(SSM_GROUPS)[:, None, None]
    di_ = jnp.arange(2)[None, :, None]
    ri = jnp.arange(HEADS_PER_GROUP)[None, None, :]
    flat = (di_ * n_heads + gi * HEADS_PER_GROUP + ri).reshape(SSM_GROUPS, 2 * HEADS_PER_GROUP)
    dt_idx = jnp.concatenate([flat, flat], axis=1).reshape(-1)
    wdt_t = w_t[d_inner + xbc_dim:][dt_idx]
    dtb_col = ssd_dt_bias[0].reshape(-1)[dt_idx].reshape(-1, 1).astype(F32)
    alog_col = ssd_a_log[0].reshape(-1)[dt_idx].reshape(-1, 1).astype(F32)
    ng_mix0 = norm_mix_g[0].reshape(1, d)

    tm = 512
    x2d = x.reshape(batch * seq, d)
    ctx2d = ctx.reshape(batch * ctx_len, d)
    lat_row = lambda i: (i * tm) // seq
    tm_in = 256
    z_l, xc_l, bt_l, dt_l = _ssd_in_proj(x2d, mod0, lambda i: (i * tm_in) // seq, ng_mix0, w_t,
                                         wdt_t, cw_xc, cb_xc, cwb, dtb_col, alog_col, tm=tm_in,
                                         period=GRID_W, want_z=True, dz=d_inner, xc_ranges=(x_rng, c_rng),
                                         b_range=b_rng)
    xc_c, bt_c, dt_c = _ssd_in_proj(ctx2d, mod0, lambda i: batch, ng_mix0, w_t, wdt_t,
                                    cw_xc[:, :d_inner], cb_xc[:, :d_inner], cwb, dtb_col, alog_col,
                                    tm=ctx_len, period=ctx_len, want_z=False, dz=d_inner,
                                    xc_ranges=(x_rng,), b_range=b_rng)

    dskip_row = jnp.repeat(ssd_d[0].astype(F32), SSM_HEAD_DIM).reshape(1, d_inner)
    ng_row = ssd_norm_g[0].reshape(1, d_inner)
    y_ssd = _ssd_scan(xc_l, bt_l, dt_l, xc_c, bt_c, dt_c, batch=batch, seq=seq, ctx_len=ctx_len)

    x1 = _ssd_out_mlp(y_ssd, xc_l, z_l, x2d, dskip_row, ng_row, mod0, lat_row, norm_mlp_g[0].reshape(1, d),
                      ssd_w_out[0].astype(BF16), mlp_w1[0].astype(BF16), mlp_w2[0].astype(BF16), tm=tm)

    out = _sc_layer(x1, mod1, lat_row, norm_mix_g[1].reshape(1, d), norm_mlp_g[1].reshape(1, d),
                    final_norm_g.reshape(1, d), sc_w_in[0].astype(BF16), sc_conv_w[0],
                    sc_w_out[0].astype(BF16), mlp_w1[1].astype(BF16), mlp_w2[1].astype(BF16),
                    tm=tm, period=GRID_W)
    return out.reshape(batch, seq, d)
```

```python
import functools

import numpy as np

import jax
import jax.numpy as jnp
from jax import lax
from jax.experimental import pallas as pl
from jax.experimental.pallas import tpu as pltpu

F32 = jnp.float32
BF16 = jnp.bfloat16

EPS = 1e-6
GRID_W = 64
SSM_HEAD_DIM = 64
SSM_GROUPS = 8
HEADS_PER_GROUP = 4
SSM_STATE = 128
SSM_CHUNK = 128
GROUP_WIDTH = HEADS_PER_GROUP * SSM_HEAD_DIM
DT_ROWS_PER_GROUP = 16
NEG_BIG = -1e30

VMEM_LIMIT_BYTES = 56 * 1024 * 1024


def _cparams(n_axes):
    return pltpu.CompilerParams(
        dimension_semantics=("arbitrary",) * n_axes,
        vmem_limit_bytes=VMEM_LIMIT_BYTES,
    )


LANE_PAD = 128


def _cast_pad_kernel(w_ref, o_ref):
    k = w_ref.shape[1]
    o_ref[:, 0:k] = w_ref[...].astype(o_ref.dtype)
    o_ref[:, k:] = jnp.zeros((o_ref.shape[0], o_ref.shape[1] - k), o_ref.dtype)


def _bf16_padded(w, block_rows):
    rows, k = w.shape
    assert rows % block_rows == 0 and block_rows % 16 == 0
    return pl.pallas_call(
        _cast_pad_kernel,
        grid=(rows // block_rows,),
        in_specs=[pl.BlockSpec((block_rows, k), lambda i: (i, 0))],
        out_specs=pl.BlockSpec((block_rows, k + LANE_PAD), lambda i: (i, 0)),
        out_shape=jax.ShapeDtypeStruct((rows, k + LANE_PAD), BF16),
        compiler_params=_cparams(1),
        name="cast_pad_bf16",
    )(w)


def _const_spec(shape):
    nd = len(shape)
    return pl.BlockSpec(shape, lambda *_: (0,) * nd, pipeline_mode=pl.Buffered(1))


def _silu(u):
    return u * (1.0 / (1.0 + jnp.exp(-u)))


def _norm_mod(x, g, shift, scale):
    ms = jnp.mean(x * x, axis=-1, keepdims=True)
    y = x * lax.rsqrt(ms + EPS) * g
    return y * (1.0 + scale) + shift


def _edge_masked_taps(conv_w, period):
    r = jnp.arange(period)[:, None]
    return jnp.concatenate([jnp.where(r != 0, conv_w[0][None, :], 0.0),
                            jnp.where(r != period - 1, conv_w[2][None, :], 0.0)], axis=0)


def _row_conv3(u, w_ref, wm_ref, cols, period):
    rows = u.shape[0]
    reps = rows // period
    w_prev = jnp.concatenate([wm_ref[0:period, cols]] * reps, axis=0)
    w_next = jnp.concatenate([wm_ref[period:2 * period, cols]] * reps, axis=0)
    return pltpu.roll(u, 1, 0) * w_prev + u * w_ref[1:2, cols] + pltpu.roll(u, rows - 1, 0) * w_next


def _mod_kernel(c_ref, w_ref, b_ref, o_ref):
    s = _silu(c_ref[...]).astype(BF16)
    o_ref[...] = jnp.dot(s, w_ref[...].astype(BF16), preferred_element_type=F32) + b_ref[...]


def _modulation(cvec, ada_w, ada_b):
    depth, d, n = ada_w.shape
    rows = cvec.shape[0]
    tn = 1536
    return pl.pallas_call(
        _mod_kernel,
        grid=(depth, n // tn),
        in_specs=[
            pl.BlockSpec((rows, d), lambda i, j: (0, 0)),
            pl.BlockSpec((None, d, tn), lambda i, j: (i, 0, j)),
            pl.BlockSpec((None, 1, tn), lambda i, j: (i, 0, j)),
        ],
        out_specs=pl.BlockSpec((None, rows, tn), lambda i, j: (i, 0, j)),
        out_shape=jax.ShapeDtypeStruct((depth, rows, n), F32),
        compiler_params=_cparams(2),
        name="adaln_mod",
    )(cvec, ada_w, ada_b.reshape(depth, 1, n))


_NT_DIMS = (((1,), (1,)), ((), ()))


def _ssd_in_kernel(x_ref, sh_ref, sc_ref, g_ref, w_ref, wdt_ref, cw_ref, cwm_ref, cb_ref,
                   cwb_ref, dtb_ref, alog_ref, *out_refs, period, ncol, nrow, want_z, xc_starts, b_start):
    if want_z:
        z_ref, xc_ref, bt_ref, dt_ref = out_refs
    else:
        xc_ref, bt_ref, dt_ref = out_refs
    h = _norm_mod(x_ref[...], g_ref[...], sh_ref[...], sc_ref[...]).astype(BF16)
    kd = x_ref.shape[1]
    if want_z:
        z_ref[...] = lax.dot_general(h, w_ref[0:z_ref.shape[1], 0:kd], _NT_DIMS,
                                     preferred_element_type=F32).astype(BF16)
    for j, w0 in enumerate(xc_starts):
        cols = slice(j * ncol, (j + 1) * ncol)
        u = lax.dot_general(h, w_ref[w0:w0 + ncol, 0:kd], _NT_DIMS, preferred_element_type=F32)
        u = _row_conv3(u, cw_ref, cwm_ref, cols, period) + cb_ref[:, cols]
        xc_ref[:, cols] = _silu(u).astype(BF16)
    tm = x_ref.shape[0]
    pos = jnp.bitwise_and(lax.broadcasted_iota(jnp.int32, (1, tm), 1), period - 1)
    for j in range(bt_ref.shape[0] // nrow):
        rows = slice(j * nrow, (j + 1) * nrow)
        u = lax.dot_general(w_ref[b_start + j * nrow:b_start + (j + 1) * nrow, 0:kd], h, _NT_DIMS,
                            preferred_element_type=F32)
        prev = jnp.where(pos != 0, pltpu.roll(u, 1, 1), 0.0)
        nxt = jnp.where(pos != period - 1, pltpu.roll(u, tm - 1, 1), 0.0)
        cwb = cwb_ref[rows, :]
        u = prev * cwb[:, 0:1] + u * cwb[:, 1:2] + nxt * cwb[:, 2:3] + cwb[:, 3:4]
        bt_ref[rows, :] = _silu(u).astype(BF16)
    raw = lax.dot_general(wdt_ref[...], h, _NT_DIMS, preferred_element_type=F32)
    v = raw + dtb_ref[...]
    sp = jnp.maximum(v, 0.0) + jnp.log1p(jnp.exp(-jnp.abs(v)))
    row = lax.broadcasted_iota(jnp.int32, (dt_ref.shape[0], 1), 0)
    is_la = jnp.bitwise_and(row, DT_ROWS_PER_GROUP - 1) >= DT_ROWS_PER_GROUP // 2
    dt_ref[...] = jnp.where(is_la, sp * (-jnp.exp(alog_ref[...])), sp)


def _ssd_in_proj(x2d, mod3, mod_row_of_tile, norm_g, w_t, wdt_t, cw_xc, cb_xc, cwb, dtb_col,
                 alog_col, *, tm, period, want_z, dz, xc_ranges, b_range):
    rows, d = x2d.shape
    dxc = cw_xc.shape[1]
    nb = b_range[1] - b_range[0]
    ndt = wdt_t.shape[0]
    ncol, nrow = 512, 256
    xc_starts = tuple(c for lo, hi in xc_ranges for c in range(lo, hi, ncol))
    assert len(xc_starts) * ncol == dxc
    kern = functools.partial(_ssd_in_kernel, period=period, ncol=ncol, nrow=nrow, want_z=want_z,
                             xc_starts=xc_starts, b_start=b_range[0])
    out_specs = [
        pl.BlockSpec((tm, dxc), lambda i: (i, 0)),
        pl.BlockSpec((nb, tm), lambda i: (0, i)),
        pl.BlockSpec((ndt, tm), lambda i: (0, i)),
    ]
    out_shape = [
        jax.ShapeDtypeStruct((rows, dxc), BF16),
        jax.ShapeDtypeStruct((nb, rows), BF16),
        jax.ShapeDtypeStruct((ndt, rows), F32),
    ]
    if want_z:
        out_specs.insert(0, pl.BlockSpec((tm, dz), lambda i: (i, 0)))
        out_shape.insert(0, jax.ShapeDtypeStruct((rows, dz), BF16))
    return pl.pallas_call(
        kern,
        grid=(rows // tm,),
        in_specs=[
            pl.BlockSpec((tm, d), lambda i: (i, 0)),
            pl.BlockSpec((None, 1, d), lambda i: (mod_row_of_tile(i), 0, 0)),
            pl.BlockSpec((None, 1, d), lambda i: (mod_row_of_tile(i), 0, 1)),
            _const_spec((1, d)),
            _const_spec(w_t.shape),
            _const_spec((ndt, d)),
            _const_spec((3, dxc)),
            _const_spec((2 * period, dxc)),
            _const_spec((1, dxc)),
            _const_spec(cwb.shape),
            _const_spec((ndt, 1)),
            _const_spec((ndt, 1)),
        ],
        out_specs=out_specs,
        out_shape=out_shape,
        compiler_params=_cparams(1),
        name="ssd_in_proj",
    )(x2d, mod3, mod3, norm_g, w_t, wdt_t, cw_xc, _edge_masked_taps(cw_xc, period), cb_xc, cwb,
      dtb_col, alog_col)


COL_CS, COL_ECS, COL_W = 0, 8, 16


def _head_expand(colmat, lane0):
    r = colmat.shape[0]
    first = lax.broadcasted_iota(jnp.int32, (r, SSM_STATE), 1) < SSM_HEAD_DIM
    cols = [colmat[:, lane0 + hd:lane0 + hd + 1] for hd in range(HEADS_PER_GROUP)]
    lo = jnp.where(first, cols[0], cols[1])
    hi = jnp.where(first, cols[2], cols[3])
    return jnp.concatenate([lo, hi], axis=1)


def _expand_select(lane0):
    r = np.arange(2 * SSM_STATE)[:, None] % SSM_STATE
    l = np.arange(2 * GROUP_WIDTH)[None, :]
    src = lane0 + (l // GROUP_WIDTH) * HEADS_PER_GROUP + (l % GROUP_WIDTH) // SSM_HEAD_DIM
    return jnp.asarray(r == src, dtype=BF16)


def _broadcast_select(lane0, count):
    r = np.arange(2 * SSM_STATE)[:, None] % SSM_STATE
    l = np.arange(SSM_STATE * count)[None, :]
    return jnp.asarray(r == lane0 + l // SSM_STATE, dtype=BF16)


def _hi_lo(colmat):
    hi = colmat.astype(BF16)
    lo = (colmat - hi.astype(F32)).astype(BF16)
    return jnp.concatenate([hi, lo], axis=1)


def _decay_rows(dt_ref, n_chunks, upper, lower):
    q = SSM_CHUNK
    nh = HEADS_PER_GROUP
    dt = jnp.concatenate([dt_ref[0:2 * nh, c * q:(c + 1) * q] for c in range(n_chunks)], axis=0)
    la = jnp.concatenate([dt_ref[2 * nh:4 * nh, c * q:(c + 1) * q] for c in range(n_chunks)], axis=0)
    csf = jnp.dot(la, upper, preferred_element_type=F32, precision=lax.Precision.HIGHEST)
    csb = jnp.dot(la, lower, preferred_element_type=F32, precision=lax.Precision.HIGHEST)
    rows = dt.shape[0]
    row = lax.broadcasted_iota(jnp.int32, (rows, 1), 0)
    is_b = jnp.bitwise_and(row, nh) != 0
    cs = jnp.where(is_b, csb, csf)
    tot = jnp.where(is_b, csb[:, 0:1], csf[:, q - 1:q])
    other = jnp.where(is_b, pltpu.roll(dt, nh, 0), pltpu.roll(dt, rows - nh, 0))
    return cs, jnp.exp(cs), jnp.exp(tot - cs) * dt, cs - jnp.log(dt), jnp.log(dt + other)


def _ssd_scan_kernel(xl_ref, btl_ref, cl_ref, dtl_ref, xc_ref, btc_ref, dtc_ref,
                     selw_ref, sele_ref, selc_ref, o_ref,
                     cols_ref, rows_ref, xw_ref, ee_ref, csb_ref, cb_ref, sloc_ref, hin_ref, etot_ref):
    q = SSM_CHUNK
    nh = HEADS_PER_GROUP
    gw = GROUP_WIDTH
    n_lat = xl_ref.shape[0] // q
    n_ctx = xc_ref.shape[0] // q
    ctx_rows = n_ctx * q
    ki = lax.broadcasted_iota(jnp.int32, (q, q), 0)
    kj = lax.broadcasted_iota(jnp.int32, (q, q), 1)

    upper = (ki <= kj).astype(F32)
    lower = (ki >= kj).astype(F32)
    pad = jnp.zeros((q - 6 * nh, q), F32)
    for dref, n_chunks, base in ((dtc_ref, n_ctx, 0), (dtl_ref, n_lat, n_ctx)):
        cs, ecs, w, rk, lds = _decay_rows(dref, n_chunks, upper, lower)
        for c in range(n_chunks):
            r = slice(2 * nh * c, 2 * nh * (c + 1))
            colmat = jnp.concatenate([cs[r], ecs[r], w[r], pad], axis=0).T
            cols_ref[(base + c) * q:(base + c + 1) * q, :] = colmat
            for d in range(2):
                p0 = 0 if d else q - 1
                etot_ref[d, base + c] = _head_expand(colmat[p0:p0 + 1, :], COL_ECS + nh * d)
        if base:
            rows_ref[0:2 * nh * n_lat, :] = rk
            rows_ref[2 * nh * n_lat:4 * nh * n_lat, :] = lds

    def expand(hl, sel_ref):
        return jnp.dot(hl, sel_ref[...], preferred_element_type=F32)

    xf = xc_ref[...].astype(F32)
    hl = _hi_lo(cols_ref[0:ctx_rows, :])
    xw_ref[0:ctx_rows, :] = (jnp.concatenate([xf, xf], axis=1) * expand(hl, selw_ref)).astype(BF16)
    blk = 4 * q
    for j in range(n_lat * q // blk):
        rows = slice(j * blk, (j + 1) * blk)
        xf = xl_ref[rows, :].astype(F32)
        r0 = ctx_rows + j * blk
        hl = _hi_lo(cols_ref[r0:r0 + blk, :])
        xw_ref[r0:r0 + blk, :] = (jnp.concatenate([xf, xf], axis=1) * expand(hl, selw_ref)).astype(BF16)
        ee_ref[rows, :] = expand(hl, sele_ref)
        csb_ref[rows, :] = expand(hl, selc_ref)

    for c in range(n_ctx):
        sloc_ref[c] = jnp.dot(btc_ref[:, c * q:(c + 1) * q], xw_ref[c * q:(c + 1) * q, :],
                              preferred_element_type=F32)

    def s_body(c, carry):
        r0 = pl.multiple_of(c * q, q)
        bt = btl_ref[:, pl.ds(r0, q)]
        sloc_ref[n_ctx + c] = jnp.dot(bt, xw_ref[pl.ds(ctx_rows + r0, q), :], preferred_element_type=F32)
        cb_ref[c] = jnp.dot(cl_ref[pl.ds(r0, q), :], bt, preferred_element_type=F32)
        return carry

    lax.fori_loop(0, n_lat, s_body, 0, unroll=4)

    for d in range(2):
        lanes = slice(gw * d, gw * (d + 1))
        h = jnp.zeros((SSM_STATE, gw), F32)
        for c in (range(n_ctx - 1, -1, -1) if d else range(n_ctx)):
            h = h * etot_ref[d, c] + sloc_ref[c, :, lanes]

        def b_body(i, h, d=d, lanes=lanes):
            c = (n_lat - 1 - i) if d else i
            hin_ref[c, :, lanes] = h.astype(BF16)
            return h * etot_ref[d, n_ctx + c] + sloc_ref[n_ctx + c, :, lanes]

        lax.fori_loop(0, n_lat, b_body, h, unroll=4)

    lane_head = lax.broadcasted_iota(jnp.int32, (q, gw), 1) // SSM_HEAD_DIM
    below = ki > kj
    above = ki < kj

    def c_body(c, carry):
        r0 = pl.multiple_of(c * q, q)
        x = xl_ref[pl.ds(r0, q), :]
        r8 = pl.multiple_of(c * 2 * nh, 2 * nh)
        rk8 = rows_ref[pl.ds(r8, 2 * nh), :]
        lds8 = rows_ref[pl.ds(2 * nh * n_lat + r8, 2 * nh), :]
        cb = cb_ref[c]
        colmat = cols_ref[pl.ds(ctx_rows + r0, q), :]
        m_parts = []
        x_parts = []
        for hd in range(nh):
            segf = csb_ref[pl.ds(r0, q), q * hd:q * (hd + 1)] - rk8[hd:hd + 1, :]
            segb = colmat[:, COL_CS + nh + hd:COL_CS + nh + hd + 1] - rk8[nh + hd:nh + hd + 1, :]
            arg = jnp.where(below, segf, jnp.where(above, segb, lds8[hd:hd + 1, :]))
            m_parts.append((jnp.exp(arg) * cb).astype(BF16))
            x_parts.append(jnp.where(lane_head == hd, x, jnp.zeros_like(x)))
        m_all = jnp.concatenate(m_parts, axis=1)
        x_bd = jnp.concatenate(x_parts, axis=0)
        y = jnp.dot(m_all, x_bd, preferred_element_type=F32)
        y_off = ee_ref[pl.ds(r0, q), :] * jnp.dot(cl_ref[pl.ds(r0, q), :], hin_ref[c],
                                                  preferred_element_type=F32)
        o_ref[pl.ds(r0, q), :] = (y + y_off[:, 0:gw] + y_off[:, gw:2 * gw]).astype(BF16)
        return carry

    lax.fori_loop(0, n_lat, c_body, 0, unroll=4)


def _ssd_scan(xc_l, bt_l, dt_l, xc_c, bt_c, dt_c, *, batch, seq, ctx_len):
    g = SSM_GROUPS
    gw = GROUP_WIDTH
    n = SSM_STATE
    c_off = (g * gw) // n
    n_lat = seq // SSM_CHUNK
    n_all = n_lat + ctx_len // SSM_CHUNK
    sel_w = _expand_select(COL_W)
    sel_e = _expand_select(COL_ECS)
    sel_c = _broadcast_select(COL_CS, HEADS_PER_GROUP)
    return pl.pallas_call(
        _ssd_scan_kernel,
        grid=(batch, g),
        in_specs=[
            pl.BlockSpec((seq, gw), lambda b, k: (b, k)),
            pl.BlockSpec((n, seq), lambda b, k: (k, b)),
            pl.BlockSpec((seq, n), lambda b, k: (b, c_off + k)),
            pl.BlockSpec((DT_ROWS_PER_GROUP, seq), lambda b, k: (k, b)),
            pl.BlockSpec((ctx_len, gw), lambda b, k: (b, k)),
            pl.BlockSpec((n, ctx_len), lambda b, k: (k, b)),
            pl.BlockSpec((DT_ROWS_PER_GROUP, ctx_len), lambda b, k: (k, b)),
            _const_spec(sel_w.shape),
            _const_spec(sel_e.shape),
            _const_spec(sel_c.shape),
        ],
        out_specs=pl.BlockSpec((seq, gw), lambda b, k: (b, k)),
        out_shape=jax.ShapeDtypeStruct((batch * seq, g * gw), BF16),
        scratch_shapes=[
            pltpu.VMEM((n_all * SSM_CHUNK, SSM_CHUNK), F32),
            pltpu.VMEM((4 * HEADS_PER_GROUP * n_lat, SSM_CHUNK), F32),
            pltpu.VMEM((n_all * SSM_CHUNK, 2 * gw), BF16),
            pltpu.VMEM((seq, 2 * gw), F32),
            pltpu.VMEM((seq, HEADS_PER_GROUP * SSM_CHUNK), F32),
            pltpu.VMEM((n_lat, SSM_CHUNK, SSM_CHUNK), F32),
            pltpu.VMEM((n_all, n, 2 * gw), F32),
            pltpu.VMEM((n_lat, n, 2 * gw), BF16),
            pltpu.VMEM((2, n_all, 1, gw), F32),
        ],
        compiler_params=_cparams(2),
        name="ssd_scan",
    )(xc_l, bt_l, xc_l, dt_l, xc_c, bt_c, dt_c, sel_w, sel_e, sel_c)


def _mlp_tail(x1, g_ref, sh_ref, sc_ref, gate_ref, w1_ref, w2_ref, nff):
    h2 = _norm_mod(x1, g_ref[...], sh_ref[...], sc_ref[...]).astype(BF16)
    d = x1.shape[1]
    dff = w2_ref.shape[0]
    acc = None
    for j in range(dff // nff):
        c0 = j * nff
        a = jnp.dot(h2, w1_ref[:, c0:c0 + nff], preferred_element_type=F32)
        a = jnp.square(jnp.maximum(a, 0.0)).astype(BF16)
        p = jnp.dot(a, w2_ref[c0:c0 + nff, 0:d], preferred_element_type=F32)
        acc = p if acc is None else acc + p
    return x1 + gate_ref[...] * acc


def _ssd_out_kernel(y_ref, xs_ref, z_ref, x_ref, dsk_ref, sng_ref, gm_ref, shf_ref, scf_ref, gf_ref,
                    ng_ref, wo_ref, w1_ref, w2_ref, o_ref):
    acc = None
    for g0 in range(0, y_ref.shape[1], GROUP_WIDTH):
        cols = slice(g0, g0 + GROUP_WIDTH)
        y = y_ref[:, cols].astype(F32) + dsk_ref[:, cols] * xs_ref[:, cols].astype(F32)
        y = y * _silu(z_ref[:, cols].astype(F32))
        y = y * lax.rsqrt(jnp.mean(y * y, axis=-1, keepdims=True) + EPS)
        yn = (y * sng_ref[:, cols]).astype(BF16)
        p = jnp.dot(yn, wo_ref[cols, 0:x_ref.shape[1]], preferred_element_type=F32)
        acc = p if acc is None else acc + p
    x1 = x_ref[...] + gm_ref[...] * acc
    o_ref[...] = _mlp_tail(x1, ng_ref, shf_ref, scf_ref, gf_ref, w1_ref, w2_ref, 1024)


def _mod_spec(d, row_of_tile, k):
    return pl.BlockSpec((None, 1, d), lambda i: (row_of_tile(i), 0, k))


def _ssd_out_mlp(y2d, xc2d, z2d, x2d, dskip_row, ssd_ng_row, mod3, row_of_tile, norm_g, wo, w1, w2, *, tm):
    rows, d = x2d.shape
    di = y2d.shape[1]
    return pl.pallas_call(
        _ssd_out_kernel,
        grid=(rows // tm,),
        in_specs=[
            pl.BlockSpec((tm, di), lambda i: (i, 0)),
            pl.BlockSpec((tm, di), lambda i: (i, 0)),
            pl.BlockSpec((tm, di), lambda i: (i, 0)),
            pl.BlockSpec((tm, d), lambda i: (i, 0)),
            _const_spec((1, di)),
            _const_spec((1, di)),
            _mod_spec(d, row_of_tile, 2),
            _mod_spec(d, row_of_tile, 3),
            _mod_spec(d, row_of_tile, 4),
            _mod_spec(d, row_of_tile, 5),
            _const_spec((1, d)),
            _const_spec(wo.shape),
            _const_spec(w1.shape),
            _const_spec(w2.shape),
        ],
        out_specs=pl.BlockSpec((tm, d), lambda i: (i, 0)),
        out_shape=jax.ShapeDtypeStruct((rows, d), F32),
        compiler_params=_cparams(1),
        name="ssd_out_mlp",
    )(y2d, xc2d, z2d, x2d, dskip_row, ssd_ng_row, mod3, mod3, mod3, mod3, norm_g, wo, w1, w2)


def _sc_layer_kernel(x_ref, shm_ref, scm_ref, gm_ref, shf_ref, scf_ref, gf_ref, ngm_ref, ngf_ref,
                     fg_ref, wi_ref, cw_ref, cwm_ref, wo_ref, w1_ref, w2_ref, o_ref, *, period):
    x = x_ref[...]
    h = _norm_mod(x, ngm_ref[...], shm_ref[...], scm_ref[...]).astype(BF16)
    w = wo_ref.shape[0]
    bg = jnp.dot(h, wi_ref[:, 0:w], preferred_element_type=F32)
    cg = jnp.dot(h, wi_ref[:, w:2 * w], preferred_element_type=F32)
    xv = jnp.dot(h, wi_ref[:, 2 * w:3 * w], preferred_element_type=F32)
    u = (bg * _row_conv3(cg * xv, cw_ref, cwm_ref, slice(0, w), period)).astype(BF16)
    y = jnp.dot(u, wo_ref[:, 0:x.shape[1]], preferred_element_type=F32)
    x1 = x + gm_ref[...] * y
    x2 = _mlp_tail(x1, ngf_ref, shf_ref, scf_ref, gf_ref, w1_ref, w2_ref, 1024)
    ms = jnp.mean(x2 * x2, axis=-1, keepdims=True)
    o_ref[...] = x2 * lax.rsqrt(ms + EPS) * fg_ref[...]


def _sc_layer(x2d, mod3, row_of_tile, ng_mix, ng_mlp, final_g, wi, conv_w, wo, w1, w2, *, tm, period):
    rows, d = x2d.shape
    kern = functools.partial(_sc_layer_kernel, period=period)
    return pl.pallas_call(
        kern,
        grid=(rows // tm,),
        in_specs=[
            pl.BlockSpec((tm, d), lambda i: (i, 0)),
            _mod_spec(d, row_of_tile, 0),
            _mod_spec(d, row_of_tile, 1),
            _mod_spec(d, row_of_tile, 2),
            _mod_spec(d, row_of_tile, 3),
            _mod_spec(d, row_of_tile, 4),
            _mod_spec(d, row_of_tile, 5),
            _const_spec((1, d)),
            _const_spec((1, d)),
            _const_spec((1, d)),
            _const_spec(wi.shape),
            _const_spec(conv_w.shape),
            _const_spec((2 * period, conv_w.shape[1])),
            _const_spec(wo.shape),
            _const_spec(w1.shape),
            _const_spec(w2.shape),
        ],
        out_specs=pl.BlockSpec((tm, d), lambda i: (i, 0)),
        out_shape=jax.ShapeDtypeStruct((rows, d), F32),
        compiler_params=_cparams(1),
        name="shortconv_layer",
    )(x2d, mod3, mod3, mod3, mod3, mod3, mod3, ng_mix, ng_mlp, final_g, wi, conv_w,
      _edge_masked_taps(conv_w, period), wo, w1, w2)


def kernel(x, c, ctx, c_ctx, ada_w, ada_b, norm_mix_g, norm_mlp_g, ssd_w_in, ssd_conv_w, ssd_conv_b,
           ssd_dt_bias, ssd_a_log, ssd_d, ssd_norm_g, ssd_w_out, sc_w_in, sc_conv_w, sc_w_out,
           mlp_w1, mlp_w2, final_norm_g):
    batch, seq, d = x.shape
    ctx_len = ctx.shape[1]
    depth = ada_w.shape[0]
    assert depth == 2 and ssd_w_in.shape[0] == 1 and sc_w_in.shape[0] == 1
    d_inner = ssd_w_out.shape[1]
    n_heads = ssd_d.shape[1]
    xbc_dim = ssd_conv_w.shape[2]
    assert n_heads == SSM_GROUPS * HEADS_PER_GROUP and d_inner == SSM_GROUPS * GROUP_WIDTH

    mod_rows = 16
    cvec = jnp.zeros((mod_rows, d), F32).at[:batch].set(c).at[batch].set(c_ctx)
    mod = _modulation(cvec, ada_w, ada_b)
    mod0 = mod[0].reshape(mod_rows, 1, 6 * d)
    mod1 = mod[1].reshape(mod_rows, 1, 6 * d)

    nbc = SSM_GROUPS * SSM_STATE
    x_rng = (d_inner, 2 * d_inner)
    b_rng = (2 * d_inner, 2 * d_inner + nbc)
    c_rng = (2 * d_inner + nbc, 2 * d_inner + 2 * nbc)
    n_proj = ssd_w_in.shape[2]
    w_t = _bf16_padded(jnp.swapaxes(ssd_w_in[0], 0, 1), block_rows=n_proj // 4)
    cw = ssd_conv_w[0]
    cbias = ssd_conv_b[0]
    cw_xc = jnp.concatenate([cw[:, :d_inner], cw[:, d_inner + nbc:]], axis=1)
    cb_xc = jnp.concatenate([cbias[:d_inner], cbias[d_inner + nbc:]]).reshape(1, -1)
    cwb = jnp.concatenate([cw[:, d_inner:d_inner + nbc].T, cbias[d_inner:d_inner + nbc, None],
                           jnp.zeros((nbc, 4), F32)], axis=1)
    gi = jnp.arange(SSM_GROUPS)[:, None, None]
    di_ = jnp.arange(2)[None, :, None]
    ri = jnp.arange(HEADS_PER_GROUP)[None, None, :]
    flat = (di_ * n_heads + gi * HEADS_PER_GROUP + ri).reshape(SSM_GROUPS, 2 * HEADS_PER_GROUP)
    dt_idx = jnp.concatenate([flat, flat], axis=1).reshape(-1)
    wdt_t = w_t[d_inner + xbc_dim:, :d][dt_idx]
    dtb_col = ssd_dt_bias[0].reshape(-1)[dt_idx].reshape(-1, 1).astype(F32)
    alog_col = ssd_a_log[0].reshape(-1)[dt_idx].reshape(-1, 1).astype(F32)
    ng_mix0 = norm_mix_g[0].reshape(1, d)

    tm = 512
    x2d = x.reshape(batch * seq, d)
    ctx2d = ctx.reshape(batch * ctx_len, d)
    lat_row = lambda i: (i * tm) // seq
    tm_in = 256
    z_l, xc_l, bt_l, dt_l = _ssd_in_proj(x2d, mod0, lambda i: (i * tm_in) // seq, ng_mix0, w_t,
                                         wdt_t, cw_xc, cb_xc, cwb, dtb_col, alog_col, tm=tm_in,
                                         period=GRID_W, want_z=True, dz=d_inner, xc_ranges=(x_rng, c_rng),
                                         b_range=b_rng)
    xc_c, bt_c, dt_c = _ssd_in_proj(ctx2d, mod0, lambda i: batch, ng_mix0, w_t, wdt_t,
                                    cw_xc[:, :d_inner], cb_xc[:, :d_inner], cwb, dtb_col, alog_col,
                                    tm=ctx_len, period=ctx_len, want_z=False, dz=d_inner,
                                    xc_ranges=(x_rng,), b_range=b_rng)

    dskip_row = jnp.repeat(ssd_d[0].astype(F32), SSM_HEAD_DIM).reshape(1, d_inner)
    ng_row = ssd_norm_g[0].reshape(1, d_inner)
    y_ssd = _ssd_scan(xc_l, bt_l, dt_l, xc_c, bt_c, dt_c, batch=batch, seq=seq, ctx_len=ctx_len)

    x1 = _ssd_out_mlp(y_ssd, xc_l, z_l, x2d, dskip_row, ng_row, mod0, lat_row, norm_mlp_g[0].reshape(1, d),
                      ssd_w_out[0].astype(BF16), mlp_w1[0].astype(BF16), mlp_w2[0].astype(BF16), tm=tm)

    out = _sc_layer(x1, mod1, lat_row, norm_mix_g[1].reshape(1, d), norm_mlp_g[1].reshape(1, d),
                    final_norm_g.reshape(1, d), sc_w_in[0].astype(BF16), sc_conv_w[0],
                    sc_w_out[0].astype(BF16), mlp_w1[1].astype(BF16), mlp_w2[1].astype(BF16),
                    tm=tm, period=GRID_W)
    return out.reshape(batch, seq, d)
```

```python
import functools

import numpy as np

import jax
import jax.numpy as jnp
from jax import lax
from jax.experimental import pallas as pl
from jax.experimental.pallas import tpu as pltpu

F32 = jnp.float32
BF16 = jnp.bfloat16

EPS = 1e-6
GRID_W = 64
SSM_HEAD_DIM = 64
SSM_GROUPS = 8
HEADS_PER_GROUP = 4
SSM_STATE = 128
SSM_CHUNK = 128
GROUP_WIDTH = HEADS_PER_GROUP * SSM_HEAD_DIM
DT_ROWS_PER_GROUP = 16
NEG_BIG = -1e30

VMEM_LIMIT_BYTES = 56 * 1024 * 1024


def _cparams(n_axes):
    return pltpu.CompilerParams(
        dimension_semantics=("arbitrary",) * n_axes,
        vmem_limit_bytes=VMEM_LIMIT_BYTES,
    )


LANE_PAD = 128


def _cast_pad_kernel(w_ref, o_ref):
    k = w_ref.shape[1]
    o_ref[:, 0:k] = w_ref[...].astype(o_ref.dtype)
    o_ref[:, k:] = jnp.zeros((o_ref.shape[0], o_ref.shape[1] - k), o_ref.dtype)


def _bf16_padded(w, block_rows):
    rows, k = w.shape
    assert rows % block_rows == 0 and block_rows % 16 == 0
    return pl.pallas_call(
        _cast_pad_kernel,
        grid=(rows // block_rows,),
        in_specs=[pl.BlockSpec((block_rows, k), lambda i: (i, 0))],
        out_specs=pl.BlockSpec((block_rows, k + LANE_PAD), lambda i: (i, 0)),
        out_shape=jax.ShapeDtypeStruct((rows, k + LANE_PAD), BF16),
        compiler_params=_cparams(1),
        name="cast_pad_bf16",
    )(w)


def _const_spec(shape):
    nd = len(shape)
    return pl.BlockSpec(shape, lambda *_: (0,) * nd, pipeline_mode=pl.Buffered(1))


def _silu(u):
    return u * (1.0 / (1.0 + jnp.exp(-u)))


def _norm_mod(x, g, shift, scale):
    ms = jnp.mean(x * x, axis=-1, keepdims=True)
    y = x * lax.rsqrt(ms + EPS) * g
    return y * (1.0 + scale) + shift


def _edge_masked_taps(conv_w, period):
    r = jnp.arange(period)[:, None]
    return jnp.concatenate([jnp.where(r != 0, conv_w[0][None, :], 0.0),
                            jnp.where(r != period - 1, conv_w[2][None, :], 0.0)], axis=0)


def _row_conv3(u, w_ref, wm_ref, cols, period):
    rows = u.shape[0]
    reps = rows // period
    w_prev = jnp.concatenate([wm_ref[0:period, cols]] * reps, axis=0)
    w_next = jnp.concatenate([wm_ref[period:2 * period, cols]] * reps, axis=0)
    return pltpu.roll(u, 1, 0) * w_prev + u * w_ref[1:2, cols] + pltpu.roll(u, rows - 1, 0) * w_next


def _mod_kernel(c_ref, w_ref, b_ref, o_ref):
    s = _silu(c_ref[...]).astype(BF16)
    o_ref[...] = jnp.dot(s, w_ref[...].astype(BF16), preferred_element_type=F32) + b_ref[...]


def _modulation(cvec, ada_w, ada_b):
    depth, d, n = ada_w.shape
    rows = cvec.shape[0]
    tn = 1536
    return pl.pallas_call(
        _mod_kernel,
        grid=(depth, n // tn),
        in_specs=[
            pl.BlockSpec((rows, d), lambda i, j: (0, 0)),
            pl.BlockSpec((None, d, tn), lambda i, j: (i, 0, j)),
            pl.BlockSpec((None, 1, tn), lambda i, j: (i, 0, j)),
        ],
        out_specs=pl.BlockSpec((None, rows, tn), lambda i, j: (i, 0, j)),
        out_shape=jax.ShapeDtypeStruct((depth, rows, n), F32),
        compiler_params=_cparams(2),
        name="adaln_mod",
    )(cvec, ada_w, ada_b.reshape(depth, 1, n))


_NT_DIMS = (((1,), (1,)), ((), ()))


def _ssd_in_kernel(x_ref, sh_ref, sc_ref, g_ref, w_ref, wdt_ref, cw_ref, cwm_ref, cb_ref,
                   cwb_ref, dtb_ref, alog_ref, *out_refs, period, ncol, nrow, want_z, xc_starts, b_start):
    if want_z:
        z_ref, xc_ref, bt_ref, dt_ref = out_refs
    else:
        xc_ref, bt_ref, dt_ref = out_refs
    h = _norm_mod(x_ref[...], g_ref[...], sh_ref[...], sc_ref[...]).astype(BF16)
    kd = x_ref.shape[1]
    if want_z:
        z_ref[...] = lax.dot_general(h, w_ref[0:z_ref.shape[1], 0:kd], _NT_DIMS,
                                     preferred_element_type=F32).astype(BF16)
    for j, w0 in enumerate(xc_starts):
        cols = slice(j * ncol, (j + 1) * ncol)
        u = lax.dot_general(h, w_ref[w0:w0 + ncol, 0:kd], _NT_DIMS, preferred_element_type=F32)
        u = _row_conv3(u, cw_ref, cwm_ref, cols, period) + cb_ref[:, cols]
        xc_ref[:, cols] = _silu(u).astype(BF16)
    tm = x_ref.shape[0]
    pos = jnp.bitwise_and(lax.broadcasted_iota(jnp.int32, (1, tm), 1), period - 1)
    for j in range(bt_ref.shape[0] // nrow):
        rows = slice(j * nrow, (j + 1) * nrow)
        u = lax.dot_general(w_ref[b_start + j * nrow:b_start + (j + 1) * nrow, 0:kd], h, _NT_DIMS,
                            preferred_element_type=F32)
        prev = jnp.where(pos != 0, pltpu.roll(u, 1, 1), 0.0)
        nxt = jnp.where(pos != period - 1, pltpu.roll(u, tm - 1, 1), 0.0)
        cwb = cwb_ref[rows, :]
        u = prev * cwb[:, 0:1] + u * cwb[:, 1:2] + nxt * cwb[:, 2:3] + cwb[:, 3:4]
        bt_ref[rows, :] = _silu(u).astype(BF16)
    raw = lax.dot_general(wdt_ref[...], h, _NT_DIMS, preferred_element_type=F32)
    v = raw + dtb_ref[...]
    sp = jnp.maximum(v, 0.0) + jnp.log1p(jnp.exp(-jnp.abs(v)))
    row = lax.broadcasted_iota(jnp.int32, (dt_ref.shape[0], 1), 0)
    is_la = jnp.bitwise_and(row, DT_ROWS_PER_GROUP - 1) >= DT_ROWS_PER_GROUP // 2
    dt_ref[...] = jnp.where(is_la, sp * (-jnp.exp(alog_ref[...])), sp)


def _ssd_in_proj(x2d, mod3, mod_row_of_tile, norm_g, w_t, wdt_t, cw_xc, cb_xc, cwb, dtb_col,
                 alog_col, *, tm, period, want_z, dz, xc_ranges, b_range):
    rows, d = x2d.shape
    dxc = cw_xc.shape[1]
    nb = b_range[1] - b_range[0]
    ndt = wdt_t.shape[0]
    ncol, nrow = 512, 256
    xc_starts = tuple(c for lo, hi in xc_ranges for c in range(lo, hi, ncol))
    assert len(xc_starts) * ncol == dxc
    kern = functools.partial(_ssd_in_kernel, period=period, ncol=ncol, nrow=nrow, want_z=want_z,
                             xc_starts=xc_starts, b_start=b_range[0])
    out_specs = [
        pl.BlockSpec((tm, dxc), lambda i: (i, 0)),
        pl.BlockSpec((nb, tm), lambda i: (0, i)),
        pl.BlockSpec((ndt, tm), lambda i: (0, i)),
    ]
    out_shape = [
        jax.ShapeDtypeStruct((rows, dxc), BF16),
        jax.ShapeDtypeStruct((nb, rows), BF16),
        jax.ShapeDtypeStruct((ndt, rows), F32),
    ]
    if want_z:
        out_specs.insert(0, pl.BlockSpec((tm, dz), lambda i: (i, 0)))
        out_shape.insert(0, jax.ShapeDtypeStruct((rows, dz), BF16))
    return pl.pallas_call(
        kern,
        grid=(rows // tm,),
        in_specs=[
            pl.BlockSpec((tm, d), lambda i: (i, 0)),
            pl.BlockSpec((None, 1, d), lambda i: (mod_row_of_tile(i), 0, 0)),
            pl.BlockSpec((None, 1, d), lambda i: (mod_row_of_tile(i), 0, 1)),
            _const_spec((1, d)),
            _const_spec(w_t.shape),
            _const_spec((ndt, d)),
            _const_spec((3, dxc)),
            _const_spec((2 * period, dxc)),
            _const_spec((1, dxc)),
            _const_spec(cwb.shape),
            _const_spec((ndt, 1)),
            _const_spec((ndt, 1)),
        ],
        out_specs=out_specs,
        out_shape=out_shape,
        compiler_params=_cparams(1),
        name="ssd_in_proj",
    )(x2d, mod3, mod3, norm_g, w_t, wdt_t, cw_xc, _edge_masked_taps(cw_xc, period), cb_xc, cwb,
      dtb_col, alog_col)


COL_CS, COL_ECS, COL_W = 0, 8, 16


def _head_expand(colmat, lane0):
    r = colmat.shape[0]
    first = lax.broadcasted_iota(jnp.int32, (r, SSM_STATE), 1) < SSM_HEAD_DIM
    cols = [colmat[:, lane0 + hd:lane0 + hd + 1] for hd in range(HEADS_PER_GROUP)]
    lo = jnp.where(first, cols[0], cols[1])
    hi = jnp.where(first, cols[2], cols[3])
    return jnp.concatenate([lo, hi], axis=1)


def _expand_select(lane0):
    r = np.arange(2 * SSM_STATE)[:, None] % SSM_STATE
    l = np.arange(2 * GROUP_WIDTH)[None, :]
    src = lane0 + (l // GROUP_WIDTH) * HEADS_PER_GROUP + (l % GROUP_WIDTH) // SSM_HEAD_DIM
    return jnp.asarray(r == src, dtype=BF16)


def _broadcast_select(lane0, count):
    r = np.arange(2 * SSM_STATE)[:, None] % SSM_STATE
    l = np.arange(SSM_STATE * count)[None, :]
    return jnp.asarray(r == lane0 + l // SSM_STATE, dtype=BF16)


def _hi_lo(colmat):
    hi = colmat.astype(BF16)
    lo = (colmat - hi.astype(F32)).astype(BF16)
    return jnp.concatenate([hi, lo], axis=1)


def _decay_rows(dt_ref, n_chunks, upper, lower):
    q = SSM_CHUNK
    nh = HEADS_PER_GROUP
    dt = jnp.concatenate([dt_ref[0:2 * nh, c * q:(c + 1) * q] for c in range(n_chunks)], axis=0)
    la = jnp.concatenate([dt_ref[2 * nh:4 * nh, c * q:(c + 1) * q] for c in range(n_chunks)], axis=0)
    csf = jnp.dot(la, upper, preferred_element_type=F32, precision=lax.Precision.HIGHEST)
    csb = jnp.dot(la, lower, preferred_element_type=F32, precision=lax.Precision.HIGHEST)
    rows = dt.shape[0]
    row = lax.broadcasted_iota(jnp.int32, (rows, 1), 0)
    is_b = jnp.bitwise_and(row, nh) != 0
    cs = jnp.where(is_b, csb, csf)
    tot = jnp.where(is_b, csb[:, 0:1], csf[:, q - 1:q])
    other = jnp.where(is_b, pltpu.roll(dt, nh, 0), pltpu.roll(dt, rows - nh, 0))
    return cs, jnp.exp(cs), jnp.exp(tot - cs) * dt, cs - jnp.log(dt), jnp.log(dt + other)


def _ssd_scan_kernel(xl_ref, btl_ref, cl_ref, dtl_ref, xc_ref, btc_ref, dtc_ref,
                     selw_ref, sele_ref, selc_ref, o_ref,
                     cols_ref, rows_ref, xw_ref, ee_ref, csb_ref, cb_ref, sloc_ref, hin_ref, etot_ref):
    q = SSM_CHUNK
    nh = HEADS_PER_GROUP
    gw = GROUP_WIDTH
    n_lat = xl_ref.shape[0] // q
    n_ctx = xc_ref.shape[0] // q
    ctx_rows = n_ctx * q
    ki = lax.broadcasted_iota(jnp.int32, (q, q), 0)
    kj = lax.broadcasted_iota(jnp.int32, (q, q), 1)

    upper = (ki <= kj).astype(F32)
    lower = (ki >= kj).astype(F32)
    pad = jnp.zeros((q - 6 * nh, q), F32)
    for dref, n_chunks, base in ((dtc_ref, n_ctx, 0), (dtl_ref, n_lat, n_ctx)):
        cs, ecs, w, rk, lds = _decay_rows(dref, n_chunks, upper, lower)
        for c in range(n_chunks):
            r = slice(2 * nh * c, 2 * nh * (c + 1))
            colmat = jnp.concatenate([cs[r], ecs[r], w[r], pad], axis=0).T
            cols_ref[(base + c) * q:(base + c + 1) * q, :] = colmat
            for d in range(2):
                p0 = 0 if d else q - 1
                etot_ref[d, base + c] = _head_expand(colmat[p0:p0 + 1, :], COL_ECS + nh * d)
        if base:
            rows_ref[0:2 * nh * n_lat, :] = rk
            rows_ref[2 * nh * n_lat:4 * nh * n_lat, :] = lds

    def expand(hl, sel_ref):
        return jnp.dot(hl, sel_ref[...], preferred_element_type=F32)

    xf = xc_ref[...].astype(F32)
    hl = _hi_lo(cols_ref[0:ctx_rows, :])
    xw_ref[0:ctx_rows, 0:2 * gw] = (jnp.concatenate([xf, xf], axis=1) * expand(hl, selw_ref)).astype(BF16)
    blk = 4 * q
    for j in range(n_lat * q // blk):
        rows = slice(j * blk, (j + 1) * blk)
        xf = xl_ref[rows, :].astype(F32)
        r0 = ctx_rows + j * blk
        hl = _hi_lo(cols_ref[r0:r0 + blk, :])
        xw_ref[r0:r0 + blk, 0:2 * gw] = (jnp.concatenate([xf, xf], axis=1) * expand(hl, selw_ref)).astype(BF16)
        ee_ref[rows, 0:2 * gw] = expand(hl, sele_ref)
        csb_ref[rows, 0:nh * q] = expand(hl, selc_ref)

    for c in range(n_ctx):
        sloc_ref[c, :, 0:2 * gw] = jnp.dot(btc_ref[:, c * q:(c + 1) * q], xw_ref[c * q:(c + 1) * q, 0:2 * gw],
                              preferred_element_type=F32)

    def s_body(c, carry):
        r0 = pl.multiple_of(c * q, q)
        bt = btl_ref[:, pl.ds(r0, q)]
        sloc_ref[n_ctx + c, :, 0:2 * gw] = jnp.dot(bt, xw_ref[pl.ds(ctx_rows + r0, q), 0:2 * gw],
                                                   preferred_element_type=F32)
        cb_ref[c] = jnp.dot(cl_ref[pl.ds(r0, q), :], bt, preferred_element_type=F32)
        return carry

    lax.fori_loop(0, n_lat, s_body, 0, unroll=True)

    for d in range(2):
        lanes = slice(gw * d, gw * (d + 1))
        h = jnp.zeros((SSM_STATE, gw), F32)
        for c in (range(n_ctx - 1, -1, -1) if d else range(n_ctx)):
            h = h * etot_ref[d, c] + sloc_ref[c, :, lanes]

        def b_body(i, h, d=d, lanes=lanes):
            c = (n_lat - 1 - i) if d else i
            hin_ref[c, :, lanes] = h.astype(BF16)
            return h * etot_ref[d, n_ctx + c] + sloc_ref[n_ctx + c, :, lanes]

        lax.fori_loop(0, n_lat, b_body, h, unroll=4)

    lane_head = lax.broadcasted_iota(jnp.int32, (q, gw), 1) // SSM_HEAD_DIM
    below = ki > kj
    above = ki < kj

    def c_body(c, carry):
        r0 = pl.multiple_of(c * q, q)
        x = xl_ref[pl.ds(r0, q), :]
        r8 = pl.multiple_of(c * 2 * nh, 2 * nh)
        rk8 = rows_ref[pl.ds(r8, 2 * nh), :]
        lds8 = rows_ref[pl.ds(2 * nh * n_lat + r8, 2 * nh), :]
        cb = cb_ref[c]
        colmat = cols_ref[pl.ds(ctx_rows + r0, q), :]
        m_parts = []
        x_parts = []
        for hd in range(nh):
            segf = csb_ref[pl.ds(r0, q), q * hd:q * (hd + 1)] - rk8[hd:hd + 1, :]
            segb = colmat[:, COL_CS + nh + hd:COL_CS + nh + hd + 1] - rk8[nh + hd:nh + hd + 1, :]
            arg = jnp.where(below, segf, jnp.where(above, segb, lds8[hd:hd + 1, :]))
            m_parts.append((jnp.exp(arg) * cb).astype(BF16))
            x_parts.append(jnp.where(lane_head == hd, x, jnp.zeros_like(x)))
        m_all = jnp.concatenate(m_parts, axis=1)
        x_bd = jnp.concatenate(x_parts, axis=0)
        y = jnp.dot(m_all, x_bd, preferred_element_type=F32)
        y_off = ee_ref[pl.ds(r0, q), 0:2 * gw] * jnp.dot(cl_ref[pl.ds(r0, q), :], hin_ref[c, :, 0:2 * gw],
                                                  preferred_element_type=F32)
        o_ref[pl.ds(r0, q), :] = (y + y_off[:, 0:gw] + y_off[:, gw:2 * gw]).astype(BF16)
        return carry

    lax.fori_loop(0, n_lat, c_body, 0, unroll=True)


def _ssd_scan(xc_l, bt_l, dt_l, xc_c, bt_c, dt_c, *, batch, seq, ctx_len):
    g = SSM_GROUPS
    gw = GROUP_WIDTH
    n = SSM_STATE
    c_off = (g * gw) // n
    n_lat = seq // SSM_CHUNK
    n_all = n_lat + ctx_len // SSM_CHUNK
    sel_w = _expand_select(COL_W)
    sel_e = _expand_select(COL_ECS)
    sel_c = _broadcast_select(COL_CS, HEADS_PER_GROUP)
    return pl.pallas_call(
        _ssd_scan_kernel,
        grid=(batch, g),
        in_specs=[
            pl.BlockSpec((seq, gw), lambda b, k: (b, k)),
            pl.BlockSpec((n, seq), lambda b, k: (k, b)),
            pl.BlockSpec((seq, n), lambda b, k: (b, c_off + k)),
            pl.BlockSpec((DT_ROWS_PER_GROUP, seq), lambda b, k: (k, b)),
            pl.BlockSpec((ctx_len, gw), lambda b, k: (b, k)),
            pl.BlockSpec((n, ctx_len), lambda b, k: (k, b)),
            pl.BlockSpec((DT_ROWS_PER_GROUP, ctx_len), lambda b, k: (k, b)),
            _const_spec(sel_w.shape),
            _const_spec(sel_e.shape),
            _const_spec(sel_c.shape),
        ],
        out_specs=pl.BlockSpec((seq, gw), lambda b, k: (b, k)),
        out_shape=jax.ShapeDtypeStruct((batch * seq, g * gw), BF16),
        scratch_shapes=[
            pltpu.VMEM((n_all * SSM_CHUNK, SSM_CHUNK), F32),
            pltpu.VMEM((4 * HEADS_PER_GROUP * n_lat, SSM_CHUNK), F32),
            pltpu.VMEM((n_all * SSM_CHUNK, 2 * gw + LANE_PAD), BF16),
            pltpu.VMEM((seq, 2 * gw + LANE_PAD), F32),
            pltpu.VMEM((seq, HEADS_PER_GROUP * SSM_CHUNK + LANE_PAD), F32),
            pltpu.VMEM((n_lat, SSM_CHUNK, SSM_CHUNK), F32),
            pltpu.VMEM((n_all, n, 2 * gw + LANE_PAD), F32),
            pltpu.VMEM((n_lat, n, 2 * gw + LANE_PAD), BF16),
            pltpu.VMEM((2, n_all, 1, gw), F32),
        ],
        compiler_params=_cparams(2),
        name="ssd_scan",
    )(xc_l, bt_l, xc_l, dt_l, xc_c, bt_c, dt_c, sel_w, sel_e, sel_c)


def _mlp_tail(x1, g_ref, sh_ref, sc_ref, gate_ref, w1_ref, w2_ref, nff):
    h2 = _norm_mod(x1, g_ref[...], sh_ref[...], sc_ref[...]).astype(BF16)
    d = x1.shape[1]
    dff = w2_ref.shape[0]
    acc = None
    for j in range(dff // nff):
        c0 = j * nff
        a = jnp.dot(h2, w1_ref[:, c0:c0 + nff], preferred_element_type=F32)
        a = jnp.square(jnp.maximum(a, 0.0)).astype(BF16)
        p = jnp.dot(a, w2_ref[c0:c0 + nff, 0:d], preferred_element_type=F32)
        acc = p if acc is None else acc + p
    return x1 + gate_ref[...] * acc


def _ssd_out_kernel(y_ref, xs_ref, z_ref, x_ref, dsk_ref, sng_ref, gm_ref, shf_ref, scf_ref, gf_ref,
                    ng_ref, wo_ref, w1_ref, w2_ref, o_ref):
    acc = None
    for g0 in range(0, y_ref.shape[1], GROUP_WIDTH):
        cols = slice(g0, g0 + GROUP_WIDTH)
        y = y_ref[:, cols].astype(F32) + dsk_ref[:, cols] * xs_ref[:, cols].astype(F32)
        y = y * _silu(z_ref[:, cols].astype(F32))
        y = y * lax.rsqrt(jnp.mean(y * y, axis=-1, keepdims=True) + EPS)
        yn = (y * sng_ref[:, cols]).astype(BF16)
        p = jnp.dot(yn, wo_ref[cols, 0:x_ref.shape[1]], preferred_element_type=F32)
        acc = p if acc is None else acc + p
    x1 = x_ref[...] + gm_ref[...] * acc
    o_ref[...] = _mlp_tail(x1, ng_ref, shf_ref, scf_ref, gf_ref, w1_ref, w2_ref, 1024)


def _mod_spec(d, row_of_tile, k):
    return pl.BlockSpec((None, 1, d), lambda i: (row_of_tile(i), 0, k))


def _ssd_out_mlp(y2d, xc2d, z2d, x2d, dskip_row, ssd_ng_row, mod3, row_of_tile, norm_g, wo, w1, w2, *, tm):
    rows, d = x2d.shape
    di = y2d.shape[1]
    return pl.pallas_call(
        _ssd_out_kernel,
        grid=(rows // tm,),
        in_specs=[
            pl.BlockSpec((tm, di), lambda i: (i, 0)),
            pl.BlockSpec((tm, di), lambda i: (i, 0)),
            pl.BlockSpec((tm, di), lambda i: (i, 0)),
            pl.BlockSpec((tm, d), lambda i: (i, 0)),
            _const_spec((1, di)),
            _const_spec((1, di)),
            _mod_spec(d, row_of_tile, 2),
            _mod_spec(d, row_of_tile, 3),
            _mod_spec(d, row_of_tile, 4),
            _mod_spec(d, row_of_tile, 5),
            _const_spec((1, d)),
            _const_spec(wo.shape),
            _const_spec(w1.shape),
            _const_spec(w2.shape),
        ],
        out_specs=pl.BlockSpec((tm, d), lambda i: (i, 0)),
        out_shape=jax.ShapeDtypeStruct((rows, d), F32),
        compiler_params=_cparams(1),
        name="ssd_out_mlp",
    )(y2d, xc2d, z2d, x2d, dskip_row, ssd_ng_row, mod3, mod3, mod3, mod3, norm_g, wo, w1, w2)


def _sc_layer_kernel(x_ref, shm_ref, scm_ref, gm_ref, shf_ref, scf_ref, gf_ref, ngm_ref, ngf_ref,
                     fg_ref, wi_ref, cw_ref, cwm_ref, wo_ref, w1_ref, w2_ref, o_ref, *, period):
    x = x_ref[...]
    h = _norm_mod(x, ngm_ref[...], shm_ref[...], scm_ref[...]).astype(BF16)
    w = wo_ref.shape[0]
    bg = jnp.dot(h, wi_ref[:, 0:w], preferred_element_type=F32)
    cg = jnp.dot(h, wi_ref[:, w:2 * w], preferred_element_type=F32)
    xv = jnp.dot(h, wi_ref[:, 2 * w:3 * w], preferred_element_type=F32)
    u = (bg * _row_conv3(cg * xv, cw_ref, cwm_ref, slice(0, w), period)).astype(BF16)
    y = jnp.dot(u, wo_ref[:, 0:x.shape[1]], preferred_element_type=F32)
    x1 = x + gm_ref[...] * y
    x2 = _mlp_tail(x1, ngf_ref, shf_ref, scf_ref, gf_ref, w1_ref, w2_ref, 1024)
    ms = jnp.mean(x2 * x2, axis=-1, keepdims=True)
    o_ref[...] = x2 * lax.rsqrt(ms + EPS) * fg_ref[...]


def _sc_layer(x2d, mod3, row_of_tile, ng_mix, ng_mlp, final_g, wi, conv_w, wo, w1, w2, *, tm, period):
    rows, d = x2d.shape
    kern = functools.partial(_sc_layer_kernel, period=period)
    return pl.pallas_call(
        kern,
        grid=(rows // tm,),
        in_specs=[
            pl.BlockSpec((tm, d), lambda i: (i, 0)),
            _mod_spec(d, row_of_tile, 0),
            _mod_spec(d, row_of_tile, 1),
            _mod_spec(d, row_of_tile, 2),
            _mod_spec(d, row_of_tile, 3),
            _mod_spec(d, row_of_tile, 4),
            _mod_spec(d, row_of_tile, 5),
            _const_spec((1, d)),
            _const_spec((1, d)),
            _const_spec((1, d)),
            _const_spec(wi.shape),
            _const_spec(conv_w.shape),
            _const_spec((2 * period, conv_w.shape[1])),
            _const_spec(wo.shape),
            _const_spec(w1.shape),
            _const_spec(w2.shape),
        ],
        out_specs=pl.BlockSpec((tm, d), lambda i: (i, 0)),
        out_shape=jax.ShapeDtypeStruct((rows, d), F32),
        compiler_params=_cparams(1),
        name="shortconv_layer",
    )(x2d, mod3, mod3, mod3, mod3, mod3, mod3, ng_mix, ng_mlp, final_g, wi, conv_w,
      _edge_masked_taps(conv_w, period), wo, w1, w2)


def kernel(x, c, ctx, c_ctx, ada_w, ada_b, norm_mix_g, norm_mlp_g, ssd_w_in, ssd_conv_w, ssd_conv_b,
           ssd_dt_bias, ssd_a_log, ssd_d, ssd_norm_g, ssd_w_out, sc_w_in, sc_conv_w, sc_w_out,
           mlp_w1, mlp_w2, final_norm_g):
    batch, seq, d = x.shape
    ctx_len = ctx.shape[1]
    depth = ada_w.shape[0]
    assert depth == 2 and ssd_w_in.shape[0] == 1 and sc_w_in.shape[0] == 1
    d_inner = ssd_w_out.shape[1]
    n_heads = ssd_d.shape[1]
    xbc_dim = ssd_conv_w.shape[2]
    assert n_heads == SSM_GROUPS * HEADS_PER_GROUP and d_inner == SSM_GROUPS * GROUP_WIDTH

    mod_rows = 16
    cvec = jnp.zeros((mod_rows, d), F32).at[:batch].set(c).at[batch].set(c_ctx)
    mod = _modulation(cvec, ada_w, ada_b)
    mod0 = mod[0].reshape(mod_rows, 1, 6 * d)
    mod1 = mod[1].reshape(mod_rows, 1, 6 * d)

    nbc = SSM_GROUPS * SSM_STATE
    x_rng = (d_inner, 2 * d_inner)
    b_rng = (2 * d_inner, 2 * d_inner + nbc)
    c_rng = (2 * d_inner + nbc, 2 * d_inner + 2 * nbc)
    n_proj = ssd_w_in.shape[2]
    w_t = _bf16_padded(jnp.swapaxes(ssd_w_in[0], 0, 1), block_rows=n_proj // 4)
    cw = ssd_conv_w[0]
    cbias = ssd_conv_b[0]
    cw_xc = jnp.concatenate([cw[:, :d_inner], cw[:, d_inner + nbc:]], axis=1)
    cb_xc = jnp.concatenate([cbias[:d_inner], cbias[d_inner + nbc:]]).reshape(1, -1)
    cwb = jnp.concatenate([cw[:, d_inner:d_inner + nbc].T, cbias[d_inner:d_inner + nbc, None],
                           jnp.zeros((nbc, 4), F32)], axis=1)
    gi = jnp.arange(SSM_GROUPS)[:, None, None]
    di_ = jnp.arange(2)[None, :, None]
    ri = jnp.arange(HEADS_PER_GROUP)[None, None, :]
    flat = (di_ * n_heads + gi * HEADS_PER_GROUP + ri).reshape(SSM_GROUPS, 2 * HEADS_PER_GROUP)
    dt_idx = jnp.concatenate([flat, flat], axis=1).reshape(-1)
    wdt_t = w_t[d_inner + xbc_dim:, :d][dt_idx]
    dtb_col = ssd_dt_bias[0].reshape(-1)[dt_idx].reshape(-1, 1).astype(F32)
    alog_col = ssd_a_log[0].reshape(-1)[dt_idx].reshape(-1, 1).astype(F32)
    ng_mix0 = norm_mix_g[0].reshape(1, d)

    tm = 512
    x2d = x.reshape(batch * seq, d)
    ctx2d = ctx.reshape(batch * ctx_len, d)
    lat_row = lambda i: (i * tm) // seq
    tm_in = 256
    z_l, xc_l, bt_l, dt_l = _ssd_in_proj(x2d, mod0, lambda i: (i * tm_in) // seq, ng_mix0, w_t,
                                         wdt_t, cw_xc, cb_xc, cwb, dtb_col, alog_col, tm=tm_in,
                                         period=GRID_W, want_z=True, dz=d_inner, xc_ranges=(x_rng, c_rng),
                                         b_range=b_rng)
    xc_c, bt_c, dt_c = _ssd_in_proj(ctx2d, mod0, lambda i: batch, ng_mix0, w_t, wdt_t,
                                    cw_xc[:, :d_inner], cb_xc[:, :d_inner], cwb, dtb_col, alog_col,
                                    tm=ctx_len, period=ctx_len, want_z=False, dz=d_inner,
                                    xc_ranges=(x_rng,), b_range=b_rng)

    dskip_row = jnp.repeat(ssd_d[0].astype(F32), SSM_HEAD_DIM).reshape(1, d_inner)
    ng_row = ssd_norm_g[0].reshape(1, d_inner)
    y_ssd = _ssd_scan(xc_l, bt_l, dt_l, xc_c, bt_c, dt_c, batch=batch, seq=seq, ctx_len=ctx_len)

    x1 = _ssd_out_mlp(y_ssd, xc_l, z_l, x2d, dskip_row, ng_row, mod0, lat_row, norm_mlp_g[0].reshape(1, d),
                      ssd_w_out[0].astype(BF16), mlp_w1[0].astype(BF16), mlp_w2[0].astype(BF16), tm=tm)

    out = _sc_layer(x1, mod1, lat_row, norm_mix_g[1].reshape(1, d), norm_mlp_g[1].reshape(1, d),
                    final_norm_g.reshape(1, d), sc_w_in[0].astype(BF16), sc_conv_w[0],
                    sc_w_out[0].astype(BF16), mlp_w1[1].astype(BF16), mlp_w2[1].astype(BF16),
                    tm=tm, period=GRID_W)
    return out.reshape(batch, seq, d)
```

```python
import functools

import numpy as np

import jax
import jax.numpy as jnp
from jax import lax
from jax.experimental import pallas as pl
from jax.experimental.pallas import tpu as pltpu

F32 = jnp.float32
BF16 = jnp.bfloat16

EPS = 1e-6
GRID_W = 64
SSM_HEAD_DIM = 64
SSM_GROUPS = 8
HEADS_PER_GROUP = 4
SSM_STATE = 128
SSM_CHUNK = 128
GROUP_WIDTH = HEADS_PER_GROUP * SSM_HEAD_DIM
DT_ROWS_PER_GROUP = 16
NEG_BIG = -1e30

VMEM_LIMIT_BYTES = 56 * 1024 * 1024


def _cparams(n_axes):
    return pltpu.CompilerParams(
        dimension_semantics=("arbitrary",) * n_axes,
        vmem_limit_bytes=VMEM_LIMIT_BYTES,
    )


LANE_PAD = 128


def _cast_pad_kernel(w_ref, o_ref):
    k = w_ref.shape[1]
    o_ref[:, 0:k] = w_ref[...].astype(o_ref.dtype)
    o_ref[:, k:] = jnp.zeros((o_ref.shape[0], o_ref.shape[1] - k), o_ref.dtype)


def _bf16_padded(w, block_rows):
    rows, k = w.shape
    assert rows % block_rows == 0 and block_rows % 16 == 0
    return pl.pallas_call(
        _cast_pad_kernel,
        grid=(rows // block_rows,),
        in_specs=[pl.BlockSpec((block_rows, k), lambda i: (i, 0))],
        out_specs=pl.BlockSpec((block_rows, k + LANE_PAD), lambda i: (i, 0)),
        out_shape=jax.ShapeDtypeStruct((rows, k + LANE_PAD), BF16),
        compiler_params=_cparams(1),
        name="cast_pad_bf16",
    )(w)


def _const_spec(shape):
    nd = len(shape)
    return pl.BlockSpec(shape, lambda *_: (0,) * nd, pipeline_mode=pl.Buffered(1))


def _silu(u):
    return u * (1.0 / (1.0 + jnp.exp(-u)))


def _norm_mod(x, g, shift, scale):
    ms = jnp.mean(x * x, axis=-1, keepdims=True)
    y = x * lax.rsqrt(ms + EPS) * g
    return y * (1.0 + scale) + shift


def _edge_masked_taps(conv_w, period):
    r = jnp.arange(period)[:, None]
    return jnp.concatenate([jnp.where(r != 0, conv_w[0][None, :], 0.0),
                            jnp.where(r != period - 1, conv_w[2][None, :], 0.0)], axis=0)


def _row_conv3(u, w_ref, wm_ref, cols, period):
    rows = u.shape[0]
    reps = rows // period
    w_prev = jnp.concatenate([wm_ref[0:period, cols]] * reps, axis=0)
    w_next = jnp.concatenate([wm_ref[period:2 * period, cols]] * reps, axis=0)
    return pltpu.roll(u, 1, 0) * w_prev + u * w_ref[1:2, cols] + pltpu.roll(u, rows - 1, 0) * w_next


def _mod_kernel(c_ref, w_ref, b_ref, o_ref):
    s = _silu(c_ref[...]).astype(BF16)
    o_ref[...] = jnp.dot(s, w_ref[...].astype(BF16), preferred_element_type=F32) + b_ref[...]


def _modulation(cvec, ada_w, ada_b):
    depth, d, n = ada_w.shape
    rows = cvec.shape[0]
    tn = 1536
    return pl.pallas_call(
        _mod_kernel,
        grid=(depth, n // tn),
        in_specs=[
            pl.BlockSpec((rows, d), lambda i, j: (0, 0)),
            pl.BlockSpec((None, d, tn), lambda i, j: (i, 0, j)),
            pl.BlockSpec((None, 1, tn), lambda i, j: (i, 0, j)),
        ],
        out_specs=pl.BlockSpec((None, rows, tn), lambda i, j: (i, 0, j)),
        out_shape=jax.ShapeDtypeStruct((depth, rows, n), F32),
        compiler_params=_cparams(2),
        name="adaln_mod",
    )(cvec, ada_w, ada_b.reshape(depth, 1, n))


_NT_DIMS = (((1,), (1,)), ((), ()))


def _ssd_in_kernel(x_ref, sh_ref, sc_ref, g_ref, w_ref, wdt_ref, cw_ref, cwm_ref, cb_ref,
                   cwb_ref, dtb_ref, alog_ref, *out_refs, period, ncol, nrow, want_z, xc_starts, b_start):
    if want_z:
        z_ref, xc_ref, bt_ref, dt_ref = out_refs
    else:
        xc_ref, bt_ref, dt_ref = out_refs
    h = _norm_mod(x_ref[...], g_ref[...], sh_ref[...], sc_ref[...]).astype(BF16)
    kd = x_ref.shape[1]
    if want_z:
        z_ref[...] = lax.dot_general(h, w_ref[0:z_ref.shape[1], 0:kd], _NT_DIMS,
                                     preferred_element_type=F32).astype(BF16)
    for j, w0 in enumerate(xc_starts):
        cols = slice(j * ncol, (j + 1) * ncol)
        u = lax.dot_general(h, w_ref[w0:w0 + ncol, 0:kd], _NT_DIMS, preferred_element_type=F32)
        u = _row_conv3(u, cw_ref, cwm_ref, cols, period) + cb_ref[:, cols]
        xc_ref[:, cols] = _silu(u).astype(BF16)
    tm = x_ref.shape[0]
    pos = jnp.bitwise_and(lax.broadcasted_iota(jnp.int32, (1, tm), 1), period - 1)
    for j in range(bt_ref.shape[0] // nrow):
        rows = slice(j * nrow, (j + 1) * nrow)
        u = lax.dot_general(w_ref[b_start + j * nrow:b_start + (j + 1) * nrow, 0:kd], h, _NT_DIMS,
                            preferred_element_type=F32)
        prev = jnp.where(pos != 0, pltpu.roll(u, 1, 1), 0.0)
        nxt = jnp.where(pos != period - 1, pltpu.roll(u, tm - 1, 1), 0.0)
        cwb = cwb_ref[rows, :]
        u = prev * cwb[:, 0:1] + u * cwb[:, 1:2] + nxt * cwb[:, 2:3] + cwb[:, 3:4]
        bt_ref[rows, :] = _silu(u).astype(BF16)
    raw = lax.dot_general(wdt_ref[...], h, _NT_DIMS, preferred_element_type=F32)
    v = raw + dtb_ref[...]
    sp = jnp.maximum(v, 0.0) + jnp.log1p(jnp.exp(-jnp.abs(v)))
    row = lax.broadcasted_iota(jnp.int32, (dt_ref.shape[0], 1), 0)
    is_la = jnp.bitwise_and(row, DT_ROWS_PER_GROUP - 1) >= DT_ROWS_PER_GROUP // 2
    dt_ref[...] = jnp.where(is_la, sp * (-jnp.exp(alog_ref[...])), sp)


def _ssd_in_proj(x2d, mod3, mod_row_of_tile, norm_g, w_t, wdt_t, cw_xc, cb_xc, cwb, dtb_col,
                 alog_col, *, tm, period, want_z, dz, xc_ranges, b_range):
    rows, d = x2d.shape
    dxc = cw_xc.shape[1]
    nb = b_range[1] - b_range[0]
    ndt = wdt_t.shape[0]
    ncol, nrow = 512, 256
    xc_starts = tuple(c for lo, hi in xc_ranges for c in range(lo, hi, ncol))
    assert len(xc_starts) * ncol == dxc
    kern = functools.partial(_ssd_in_kernel, period=period, ncol=ncol, nrow=nrow, want_z=want_z,
                             xc_starts=xc_starts, b_start=b_range[0])
    out_specs = [
        pl.BlockSpec((tm, dxc), lambda i: (i, 0)),
        pl.BlockSpec((nb, tm), lambda i: (0, i)),
        pl.BlockSpec((ndt, tm), lambda i: (0, i)),
    ]
    out_shape = [
        jax.ShapeDtypeStruct((rows, dxc), BF16),
        jax.ShapeDtypeStruct((nb, rows), BF16),
        jax.ShapeDtypeStruct((ndt, rows), F32),
    ]
    if want_z:
        out_specs.insert(0, pl.BlockSpec((tm, dz), lambda i: (i, 0)))
        out_shape.insert(0, jax.ShapeDtypeStruct((rows, dz), BF16))
    return pl.pallas_call(
        kern,
        grid=(rows // tm,),
        in_specs=[
            pl.BlockSpec((tm, d), lambda i: (i, 0)),
            pl.BlockSpec((None, 1, d), lambda i: (mod_row_of_tile(i), 0, 0)),
            pl.BlockSpec((None, 1, d), lambda i: (mod_row_of_tile(i), 0, 1)),
            _const_spec((1, d)),
            _const_spec(w_t.shape),
            _const_spec((ndt, d)),
            _const_spec((3, dxc)),
            _const_spec((2 * period, dxc)),
            _const_spec((1, dxc)),
            _const_spec(cwb.shape),
            _const_spec((ndt, 1)),
            _const_spec((ndt, 1)),
        ],
        out_specs=out_specs,
        out_shape=out_shape,
        compiler_params=_cparams(1),
        name="ssd_in_proj",
    )(x2d, mod3, mod3, norm_g, w_t, wdt_t, cw_xc, _edge_masked_taps(cw_xc, period), cb_xc, cwb,
      dtb_col, alog_col)


COL_CS, COL_ECS, COL_W = 0, 8, 16


def _head_expand(colmat, lane0):
    r = colmat.shape[0]
    first = lax.broadcasted_iota(jnp.int32, (r, SSM_STATE), 1) < SSM_HEAD_DIM
    cols = [colmat[:, lane0 + hd:lane0 + hd + 1] for hd in range(HEADS_PER_GROUP)]
    lo = jnp.where(first, cols[0], cols[1])
    hi = jnp.where(first, cols[2], cols[3])
    return jnp.concatenate([lo, hi], axis=1)


def _expand_select(lane0):
    r = np.arange(2 * SSM_STATE)[:, None] % SSM_STATE
    l = np.arange(2 * GROUP_WIDTH)[None, :]
    src = lane0 + (l // GROUP_WIDTH) * HEADS_PER_GROUP + (l % GROUP_WIDTH) // SSM_HEAD_DIM
    return jnp.asarray(r == src, dtype=BF16)


def _broadcast_select(lane0, count):
    r = np.arange(2 * SSM_STATE)[:, None] % SSM_STATE
    l = np.arange(SSM_STATE * count)[None, :]
    return jnp.asarray(r == lane0 + l // SSM_STATE, dtype=BF16)


def _hi_lo(colmat):
    hi = colmat.astype(BF16)
    lo = (colmat - hi.astype(F32)).astype(BF16)
    return jnp.concatenate([hi, lo], axis=1)


def _decay_rows(dt_ref, n_chunks, upper, lower):
    q = SSM_CHUNK
    nh = HEADS_PER_GROUP
    dt = jnp.concatenate([dt_ref[0:2 * nh, c * q:(c + 1) * q] for c in range(n_chunks)], axis=0)
    la = jnp.concatenate([dt_ref[2 * nh:4 * nh, c * q:(c + 1) * q] for c in range(n_chunks)], axis=0)
    csf = jnp.dot(la, upper, preferred_element_type=F32, precision=lax.Precision.HIGHEST)
    csb = jnp.dot(la, lower, preferred_element_type=F32, precision=lax.Precision.HIGHEST)
    rows = dt.shape[0]
    row = lax.broadcasted_iota(jnp.int32, (rows, 1), 0)
    is_b = jnp.bitwise_and(row, nh) != 0
    cs = jnp.where(is_b, csb, csf)
    tot = jnp.where(is_b, csb[:, 0:1], csf[:, q - 1:q])
    other = jnp.where(is_b, pltpu.roll(dt, nh, 0), pltpu.roll(dt, rows - nh, 0))
    return cs, jnp.exp(cs), jnp.exp(tot - cs) * dt, cs - jnp.log(dt), jnp.log(dt + other)


def _ssd_scan_kernel(xl_ref, btl_ref, cl_ref, dtl_ref, xc_ref, btc_ref, dtc_ref,
                     selw_ref, sele_ref, selc_ref, o_ref,
                     cols_ref, rows_ref, xw_ref, ee_ref, csb_ref, cb_ref, sloc_ref, hin_ref, etot_ref):
    q = SSM_CHUNK
    nh = HEADS_PER_GROUP
    gw = GROUP_WIDTH
    n_lat = xl_ref.shape[0] // q
    n_ctx = xc_ref.shape[0] // q
    ctx_rows = n_ctx * q
    ki = lax.broadcasted_iota(jnp.int32, (q, q), 0)
    kj = lax.broadcasted_iota(jnp.int32, (q, q), 1)

    upper = (ki <= kj).astype(F32)
    lower = (ki >= kj).astype(F32)
    pad = jnp.zeros((q - 6 * nh, q), F32)
    for dref, n_chunks, base in ((dtc_ref, n_ctx, 0), (dtl_ref, n_lat, n_ctx)):
        cs, ecs, w, rk, lds = _decay_rows(dref, n_chunks, upper, lower)
        for c in range(n_chunks):
            r = slice(2 * nh * c, 2 * nh * (c + 1))
            colmat = jnp.concatenate([cs[r], ecs[r], w[r], pad], axis=0).T
            cols_ref[(base + c) * q:(base + c + 1) * q, :] = colmat
            for d in range(2):
                p0 = 0 if d else q - 1
                etot_ref[d, base + c] = _head_expand(colmat[p0:p0 + 1, :], COL_ECS + nh * d)
        if base:
            rows_ref[0:2 * nh * n_lat, :] = rk
            rows_ref[2 * nh * n_lat:4 * nh * n_lat, :] = lds

    def expand(hl, sel_ref):
        return jnp.dot(hl, sel_ref[...], preferred_element_type=F32)

    xf = xc_ref[...].astype(F32)
    hl = _hi_lo(cols_ref[0:ctx_rows, :])
    xw_ref[0:ctx_rows, 0:2 * gw] = (jnp.concatenate([xf, xf], axis=1) * expand(hl, selw_ref)).astype(BF16)
    blk = 4 * q
    for j in range(n_lat * q // blk):
        rows = slice(j * blk, (j + 1) * blk)
        xf = xl_ref[rows, :].astype(F32)
        r0 = ctx_rows + j * blk
        hl = _hi_lo(cols_ref[r0:r0 + blk, :])
        xw_ref[r0:r0 + blk, 0:2 * gw] = (jnp.concatenate([xf, xf], axis=1) * expand(hl, selw_ref)).astype(BF16)
        ee_ref[rows, 0:2 * gw] = expand(hl, sele_ref)
        csb_ref[rows, 0:nh * q] = expand(hl, selc_ref)

    for c in range(n_ctx):
        sloc_ref[c, :, 0:2 * gw] = jnp.dot(btc_ref[:, c * q:(c + 1) * q], xw_ref[c * q:(c + 1) * q, 0:2 * gw],
                              preferred_element_type=F32)

    def s_body(c, carry):
        r0 = pl.multiple_of(c * q, q)
        bt = btl_ref[:, pl.ds(r0, q)]
        sloc_ref[n_ctx + c, :, 0:2 * gw] = jnp.dot(bt, xw_ref[pl.ds(ctx_rows + r0, q), 0:2 * gw],
                                                   preferred_element_type=F32)
        cb_ref[c] = jnp.dot(cl_ref[pl.ds(r0, q), :], bt, preferred_element_type=F32)
        return carry

    lax.fori_loop(0, n_lat, s_body, 0, unroll=True)

    for d in range(2):
        lanes = slice(gw * d, gw * (d + 1))
        h = jnp.zeros((SSM_STATE, gw), F32)
        for c in (range(n_ctx - 1, -1, -1) if d else range(n_ctx)):
            h = h * etot_ref[d, c] + sloc_ref[c, :, lanes]

        def b_body(i, h, d=d, lanes=lanes):
            c = (n_lat - 1 - i) if d else i
            hin_ref[c, :, lanes] = h.astype(BF16)
            return h * etot_ref[d, n_ctx + c] + sloc_ref[n_ctx + c, :, lanes]

        lax.fori_loop(0, n_lat, b_body, h, unroll=True)

    lane_head = lax.broadcasted_iota(jnp.int32, (q, gw), 1) // SSM_HEAD_DIM
    below = ki > kj
    above = ki < kj

    def c_body(c, carry):
        r0 = pl.multiple_of(c * q, q)
        x = xl_ref[pl.ds(r0, q), :]
        r8 = pl.multiple_of(c * 2 * nh, 2 * nh)
        rk8 = rows_ref[pl.ds(r8, 2 * nh), :]
        lds8 = rows_ref[pl.ds(2 * nh * n_lat + r8, 2 * nh), :]
        cb = cb_ref[c]
        colmat = cols_ref[pl.ds(ctx_rows + r0, q), :]
        m_parts = []
        x_parts = []
        for hd in range(nh):
            segf = csb_ref[pl.ds(r0, q), q * hd:q * (hd + 1)] - rk8[hd:hd + 1, :]
            segb = colmat[:, COL_CS + nh + hd:COL_CS + nh + hd + 1] - rk8[nh + hd:nh + hd + 1, :]
            arg = jnp.where(below, segf, jnp.where(above, segb, lds8[hd:hd + 1, :]))
            m_parts.append((jnp.exp(arg) * cb).astype(BF16))
            x_parts.append(jnp.where(lane_head == hd, x, jnp.zeros_like(x)))
        m_all = jnp.concatenate(m_parts, axis=1)
        x_bd = jnp.concatenate(x_parts, axis=0)
        y = jnp.dot(m_all, x_bd, preferred_element_type=F32)
        y_off = ee_ref[pl.ds(r0, q), 0:2 * gw] * jnp.dot(cl_ref[pl.ds(r0, q), :], hin_ref[c, :, 0:2 * gw],
                                                  preferred_element_type=F32)
        o_ref[pl.ds(r0, q), :] = (y + y_off[:, 0:gw] + y_off[:, gw:2 * gw]).astype(BF16)
        return carry

    lax.fori_loop(0, n_lat, c_body, 0, unroll=True)


def _ssd_scan(xc_l, bt_l, dt_l, xc_c, bt_c, dt_c, *, batch, seq, ctx_len):
    g = SSM_GROUPS
    gw = GROUP_WIDTH
    n = SSM_STATE
    c_off = (g * gw) // n
    n_lat = seq // SSM_CHUNK
    n_all = n_lat + ctx_len // SSM_CHUNK
    sel_w = _expand_select(COL_W)
    sel_e = _expand_select(COL_ECS)
    sel_c = _broadcast_select(COL_CS, HEADS_PER_GROUP)
    return pl.pallas_call(
        _ssd_scan_kernel,
        grid=(batch, g),
        in_specs=[
            pl.BlockSpec((seq, gw), lambda b, k: (b, k)),
            pl.BlockSpec((n, seq), lambda b, k: (k, b)),
            pl.BlockSpec((seq, n), lambda b, k: (b, c_off + k)),
            pl.BlockSpec((DT_ROWS_PER_GROUP, seq), lambda b, k: (k, b)),
            pl.BlockSpec((ctx_len, gw), lambda b, k: (b, k)),
            pl.BlockSpec((n, ctx_len), lambda b, k: (k, b)),
            pl.BlockSpec((DT_ROWS_PER_GROUP, ctx_len), lambda b, k: (k, b)),
            _const_spec(sel_w.shape),
            _const_spec(sel_e.shape),
            _const_spec(sel_c.shape),
        ],
        out_specs=pl.BlockSpec((seq, gw), lambda b, k: (b, k)),
        out_shape=jax.ShapeDtypeStruct((batch * seq, g * gw), BF16),
        scratch_shapes=[
            pltpu.VMEM((n_all * SSM_CHUNK, SSM_CHUNK), F32),
            pltpu.VMEM((4 * HEADS_PER_GROUP * n_lat, SSM_CHUNK), F32),
            pltpu.VMEM((n_all * SSM_CHUNK, 2 * gw + LANE_PAD), BF16),
            pltpu.VMEM((seq, 2 * gw + LANE_PAD), F32),
            pltpu.VMEM((seq, HEADS_PER_GROUP * SSM_CHUNK + LANE_PAD), F32),
            pltpu.VMEM((n_lat, SSM_CHUNK, SSM_CHUNK), F32),
            pltpu.VMEM((n_all, n, 2 * gw + LANE_PAD), F32),
            pltpu.VMEM((n_lat, n, 2 * gw + LANE_PAD), BF16),
            pltpu.VMEM((2, n_all, 1, gw), F32),
        ],
        compiler_params=_cparams(2),
        name="ssd_scan",
    )(xc_l, bt_l, xc_l, dt_l, xc_c, bt_c, dt_c, sel_w, sel_e, sel_c)


def _mlp_tail(x1, g_ref, sh_ref, sc_ref, gate_ref, w1_ref, w2_ref, nff):
    h2 = _norm_mod(x1, g_ref[...], sh_ref[...], sc_ref[...]).astype(BF16)
    d = x1.shape[1]
    dff = w2_ref.shape[0]
    acc = None
    for j in range(dff // nff):
        c0 = j * nff
        a = jnp.dot(h2, w1_ref[:, c0:c0 + nff], preferred_element_type=F32)
        a = jnp.square(jnp.maximum(a, 0.0)).astype(BF16)
        p = jnp.dot(a, w2_ref[c0:c0 + nff, 0:d], preferred_element_type=F32)
        acc = p if acc is None else acc + p
    return x1 + gate_ref[...] * acc


def _ssd_out_kernel(y_ref, xs_ref, z_ref, x_ref, dsk_ref, sng_ref, gm_ref, shf_ref, scf_ref, gf_ref,
                    ng_ref, wo_ref, w1_ref, w2_ref, o_ref):
    acc = None
    for g0 in range(0, y_ref.shape[1], GROUP_WIDTH):
        cols = slice(g0, g0 + GROUP_WIDTH)
        y = y_ref[:, cols].astype(F32) + dsk_ref[:, cols] * xs_ref[:, cols].astype(F32)
        y = y * _silu(z_ref[:, cols]).astype(F32)
        y = y * lax.rsqrt(jnp.mean(y * y, axis=-1, keepdims=True) + EPS)
        yn = (y * sng_ref[:, cols]).astype(BF16)
        p = jnp.dot(yn, wo_ref[cols, 0:x_ref.shape[1]], preferred_element_type=F32)
        acc = p if acc is None else acc + p
    x1 = x_ref[...] + gm_ref[...] * acc
    o_ref[...] = _mlp_tail(x1, ng_ref, shf_ref, scf_ref, gf_ref, w1_ref, w2_ref, 1024)


def _mod_spec(d, row_of_tile, k):
    return pl.BlockSpec((None, 1, d), lambda i: (row_of_tile(i), 0, k))


def _ssd_out_mlp(y2d, xc2d, z2d, x2d, dskip_row, ssd_ng_row, mod3, row_of_tile, norm_g, wo, w1, w2, *, tm):
    rows, d = x2d.shape
    di = y2d.shape[1]
    return pl.pallas_call(
        _ssd_out_kernel,
        grid=(rows // tm,),
        in_specs=[
            pl.BlockSpec((tm, di), lambda i: (i, 0)),
            pl.BlockSpec((tm, di), lambda i: (i, 0)),
            pl.BlockSpec((tm, di), lambda i: (i, 0)),
            pl.BlockSpec((tm, d), lambda i: (i, 0)),
            _const_spec((1, di)),
            _const_spec((1, di)),
            _mod_spec(d, row_of_tile, 2),
            _mod_spec(d, row_of_tile, 3),
            _mod_spec(d, row_of_tile, 4),
            _mod_spec(d, row_of_tile, 5),
            _const_spec((1, d)),
            _const_spec(wo.shape),
            _const_spec(w1.shape),
            _const_spec(w2.shape),
        ],
        out_specs=pl.BlockSpec((tm, d), lambda i: (i, 0)),
        out_shape=jax.ShapeDtypeStruct((rows, d), F32),
        compiler_params=_cparams(1),
        name="ssd_out_mlp",
    )(y2d, xc2d, z2d, x2d, dskip_row, ssd_ng_row, mod3, mod3, mod3, mod3, norm_g, wo, w1, w2)


def _sc_layer_kernel(x_ref, shm_ref, scm_ref, gm_ref, shf_ref, scf_ref, gf_ref, ngm_ref, ngf_ref,
                     fg_ref, wi_ref, cw_ref, cwm_ref, wo_ref, w1_ref, w2_ref, o_ref, *, period):
    x = x_ref[...]
    h = _norm_mod(x, ngm_ref[...], shm_ref[...], scm_ref[...]).astype(BF16)
    w = wo_ref.shape[0]
    bg = jnp.dot(h, wi_ref[:, 0:w], preferred_element_type=F32)
    cg = jnp.dot(h, wi_ref[:, w:2 * w], preferred_element_type=F32)
    xv = jnp.dot(h, wi_ref[:, 2 * w:3 * w], preferred_element_type=F32)
    u = (bg * _row_conv3(cg * xv, cw_ref, cwm_ref, slice(0, w), period)).astype(BF16)
    y = jnp.dot(u, wo_ref[:, 0:x.shape[1]], preferred_element_type=F32)
    x1 = x + gm_ref[...] * y
    x2 = _mlp_tail(x1, ngf_ref, shf_ref, scf_ref, gf_ref, w1_ref, w2_ref, 1024)
    ms = jnp.mean(x2 * x2, axis=-1, keepdims=True)
    o_ref[...] = x2 * lax.rsqrt(ms + EPS) * fg_ref[...]


def _sc_layer(x2d, mod3, row_of_tile, ng_mix, ng_mlp, final_g, wi, conv_w, wo, w1, w2, *, tm, period):
    rows, d = x2d.shape
    kern = functools.partial(_sc_layer_kernel, period=period)
    return pl.pallas_call(
        kern,
        grid=(rows // tm,),
        in_specs=[
            pl.BlockSpec((tm, d), lambda i: (i, 0)),
            _mod_spec(d, row_of_tile, 0),
            _mod_spec(d, row_of_tile, 1),
            _mod_spec(d, row_of_tile, 2),
            _mod_spec(d, row_of_tile, 3),
            _mod_spec(d, row_of_tile, 4),
            _mod_spec(d, row_of_tile, 5),
            _const_spec((1, d)),
            _const_spec((1, d)),
            _const_spec((1, d)),
            _const_spec(wi.shape),
            _const_spec(conv_w.shape),
            _const_spec((2 * period, conv_w.shape[1])),
            _const_spec(wo.shape),
            _const_spec(w1.shape),
            _const_spec(w2.shape),
        ],
        out_specs=pl.BlockSpec((tm, d), lambda i: (i, 0)),
        out_shape=jax.ShapeDtypeStruct((rows, d), F32),
        compiler_params=_cparams(1),
        name="shortconv_layer",
    )(x2d, mod3, mod3, mod3, mod3, mod3, mod3, ng_mix, ng_mlp, final_g, wi, conv_w,
      _edge_masked_taps(conv_w, period), wo, w1, w2)


def kernel(x, c, ctx, c_ctx, ada_w, ada_b, norm_mix_g, norm_mlp_g, ssd_w_in, ssd_conv_w, ssd_conv_b,
           ssd_dt_bias, ssd_a_log, ssd_d, ssd_norm_g, ssd_w_out, sc_w_in, sc_conv_w, sc_w_out,
           mlp_w1, mlp_w2, final_norm_g):
    batch, seq, d = x.shape
    ctx_len = ctx.shape[1]
    depth = ada_w.shape[0]
    assert depth == 2 and ssd_w_in.shape[0] == 1 and sc_w_in.shape[0] == 1
    d_inner = ssd_w_out.shape[1]
    n_heads = ssd_d.shape[1]
    xbc_dim = ssd_conv_w.shape[2]
    assert n_heads == SSM_GROUPS * HEADS_PER_GROUP and d_inner == SSM_GROUPS * GROUP_WIDTH

    mod_rows = 16
    cvec = jnp.zeros((mod_rows, d), F32).at[:batch].set(c).at[batch].set(c_ctx)
    mod = _modulation(cvec, ada_w, ada_b)
    mod0 = mod[0].reshape(mod_rows, 1, 6 * d)
    mod1 = mod[1].reshape(mod_rows, 1, 6 * d)

    nbc = SSM_GROUPS * SSM_STATE
    x_rng = (d_inner, 2 * d_inner)
    b_rng = (2 * d_inner, 2 * d_inner + nbc)
    c_rng = (2 * d_inner + nbc, 2 * d_inner + 2 * nbc)
    n_proj = ssd_w_in.shape[2]
    w_t = _bf16_padded(jnp.swapaxes(ssd_w_in[0], 0, 1), block_rows=n_proj // 4)
    cw = ssd_conv_w[0]
    cbias = ssd_conv_b[0]
    cw_xc = jnp.concatenate([cw[:, :d_inner], cw[:, d_inner + nbc:]], axis=1)
    cb_xc = jnp.concatenate([cbias[:d_inner], cbias[d_inner + nbc:]]).reshape(1, -1)
    cwb = jnp.concatenate([cw[:, d_inner:d_inner + nbc].T, cbias[d_inner:d_inner + nbc, None],
                           jnp.zeros((nbc, 4), F32)], axis=1)
    gi = jnp.arange(SSM_GROUPS)[:, None, None]
    di_ = jnp.arange(2)[None, :, None]
    ri = jnp.arange(HEADS_PER_GROUP)[None, None, :]
    flat = (di_ * n_heads + gi * HEADS_PER_GROUP + ri).reshape(SSM_GROUPS, 2 * HEADS_PER_GROUP)
    dt_idx = jnp.concatenate([flat, flat], axis=1).reshape(-1)
    wdt_t = w_t[d_inner + xbc_dim:, :d][dt_idx]
    dtb_col = ssd_dt_bias[0].reshape(-1)[dt_idx].reshape(-1, 1).astype(F32)
    alog_col = ssd_a_log[0].reshape(-1)[dt_idx].reshape(-1, 1).astype(F32)
    ng_mix0 = norm_mix_g[0].reshape(1, d)

    tm = 512
    x2d = x.reshape(batch * seq, d)
    ctx2d = ctx.reshape(batch * ctx_len, d)
    lat_row = lambda i: (i * tm) // seq
    tm_in = 256
    z_l, xc_l, bt_l, dt_l = _ssd_in_proj(x2d, mod0, lambda i: (i * tm_in) // seq, ng_mix0, w_t,
                                         wdt_t, cw_xc, cb_xc, cwb, dtb_col, alog_col, tm=tm_in,
                                         period=GRID_W, want_z=True, dz=d_inner, xc_ranges=(x_rng, c_rng),
                                         b_range=b_rng)
    xc_c, bt_c, dt_c = _ssd_in_proj(ctx2d, mod0, lambda i: batch, ng_mix0, w_t, wdt_t,
                                    cw_xc[:, :d_inner], cb_xc[:, :d_inner], cwb, dtb_col, alog_col,
                                    tm=ctx_len, period=ctx_len, want_z=False, dz=d_inner,
                                    xc_ranges=(x_rng,), b_range=b_rng)

    dskip_row = jnp.repeat(ssd_d[0].astype(F32), SSM_HEAD_DIM).reshape(1, d_inner)
    ng_row = ssd_norm_g[0].reshape(1, d_inner)
    y_ssd = _ssd_scan(xc_l, bt_l, dt_l, xc_c, bt_c, dt_c, batch=batch, seq=seq, ctx_len=ctx_len)

    x1 = _ssd_out_mlp(y_ssd, xc_l, z_l, x2d, dskip_row, ng_row, mod0, lat_row, norm_mlp_g[0].reshape(1, d),
                      ssd_w_out[0].astype(BF16), mlp_w1[0].astype(BF16), mlp_w2[0].astype(BF16), tm=tm)

    out = _sc_layer(x1, mod1, lat_row, norm_mix_g[1].reshape(1, d), norm_mlp_g[1].reshape(1, d),
                    final_norm_g.reshape(1, d), sc_w_in[0].astype(BF16), sc_conv_w[0],
                    sc_w_out[0].astype(BF16), mlp_w1[1].astype(BF16), mlp_w2[1].astype(BF16),
                    tm=tm, period=GRID_W)
    return out.reshape(batch, seq, d)
```

```python
import functools

import numpy as np

import jax
import jax.numpy as jnp
from jax import lax
from jax.experimental import pallas as pl
from jax.experimental.pallas import tpu as pltpu

F32 = jnp.float32
BF16 = jnp.bfloat16

EPS = 1e-6
GRID_W = 64
SSM_HEAD_DIM = 64
SSM_GROUPS = 8
HEADS_PER_GROUP = 4
SSM_STATE = 128
SSM_CHUNK = 128
GROUP_WIDTH = HEADS_PER_GROUP * SSM_HEAD_DIM
DT_ROWS_PER_GROUP = 16

VMEM_LIMIT_BYTES = 56 * 1024 * 1024


def _cparams(n_axes):
    return pltpu.CompilerParams(
        dimension_semantics=("arbitrary",) * n_axes,
        vmem_limit_bytes=VMEM_LIMIT_BYTES,
    )


LANE_PAD = 128


def _cast_pad_kernel(w_ref, o_ref):
    k = w_ref.shape[1]
    o_ref[:, 0:k] = w_ref[...].astype(o_ref.dtype)
    o_ref[:, k:] = jnp.zeros((o_ref.shape[0], o_ref.shape[1] - k), o_ref.dtype)


def _bf16_padded(w, block_rows):
    rows, k = w.shape
    assert rows % block_rows == 0 and block_rows % 16 == 0
    return pl.pallas_call(
        _cast_pad_kernel,
        grid=(rows // block_rows,),
        in_specs=[pl.BlockSpec((block_rows, k), lambda i: (i, 0))],
        out_specs=pl.BlockSpec((block_rows, k + LANE_PAD), lambda i: (i, 0)),
        out_shape=jax.ShapeDtypeStruct((rows, k + LANE_PAD), BF16),
        compiler_params=_cparams(1),
        name="cast_pad_bf16",
    )(w)


def _const_spec(shape):
    nd = len(shape)
    return pl.BlockSpec(shape, lambda *_: (0,) * nd, pipeline_mode=pl.Buffered(1))


def _silu(u):
    return u * (1.0 / (1.0 + jnp.exp(-u)))


def _norm_mod(x, g, shift, scale):
    ms = jnp.mean(x * x, axis=-1, keepdims=True)
    y = x * lax.rsqrt(ms + EPS) * g
    return y * (1.0 + scale) + shift


def _edge_masked_taps(conv_w, period):
    r = jnp.arange(period)[:, None]
    return jnp.concatenate([jnp.where(r != 0, conv_w[0][None, :], 0.0),
                            jnp.where(r != period - 1, conv_w[2][None, :], 0.0)], axis=0)


def _row_conv3(u, w_ref, wm_ref, cols, period):
    rows = u.shape[0]
    reps = rows // period
    w_prev = jnp.concatenate([wm_ref[0:period, cols]] * reps, axis=0)
    w_next = jnp.concatenate([wm_ref[period:2 * period, cols]] * reps, axis=0)
    return pltpu.roll(u, 1, 0) * w_prev + u * w_ref[1:2, cols] + pltpu.roll(u, rows - 1, 0) * w_next


def _mod_kernel(c_ref, w_ref, b_ref, o_ref):
    s = _silu(c_ref[...]).astype(BF16)
    o_ref[...] = jnp.dot(s, w_ref[...].astype(BF16), preferred_element_type=F32) + b_ref[...]


def _modulation(cvec, ada_w, ada_b):
    depth, d, n = ada_w.shape
    rows = cvec.shape[0]
    tn = 1536
    return pl.pallas_call(
        _mod_kernel,
        grid=(depth, n // tn),
        in_specs=[
            pl.BlockSpec((rows, d), lambda i, j: (0, 0)),
            pl.BlockSpec((None, d, tn), lambda i, j: (i, 0, j)),
            pl.BlockSpec((None, 1, tn), lambda i, j: (i, 0, j)),
        ],
        out_specs=pl.BlockSpec((None, rows, tn), lambda i, j: (i, 0, j)),
        out_shape=jax.ShapeDtypeStruct((depth, rows, n), F32),
        compiler_params=_cparams(2),
        name="adaln_mod",
    )(cvec, ada_w, ada_b.reshape(depth, 1, n))


_NT_DIMS = (((1,), (1,)), ((), ()))


def _ssd_in_kernel(x_ref, sh_ref, sc_ref, g_ref, w_ref, wdt_ref, cw_ref, cwm_ref, cb_ref,
                   cwb_ref, dtb_ref, alog_ref, *out_refs, period, ncol, nrow, want_z, xc_starts, b_start):
    if want_z:
        z_ref, xc_ref, bt_ref, dt_ref = out_refs
    else:
        xc_ref, bt_ref, dt_ref = out_refs
    h = _norm_mod(x_ref[...], g_ref[...], sh_ref[...], sc_ref[...]).astype(BF16)
    kd = x_ref.shape[1]
    if want_z:
        z_ref[...] = lax.dot_general(h, w_ref[0:z_ref.shape[1], 0:kd], _NT_DIMS,
                                     preferred_element_type=F32).astype(BF16)
    for j, w0 in enumerate(xc_starts):
        cols = slice(j * ncol, (j + 1) * ncol)
        u = lax.dot_general(h, w_ref[w0:w0 + ncol, 0:kd], _NT_DIMS, preferred_element_type=F32)
        u = _row_conv3(u, cw_ref, cwm_ref, cols, period) + cb_ref[:, cols]
        xc_ref[:, cols] = _silu(u).astype(BF16)
    tm = x_ref.shape[0]
    pos = jnp.bitwise_and(lax.broadcasted_iota(jnp.int32, (1, tm), 1), period - 1)
    for j in range(bt_ref.shape[0] // nrow):
        rows = slice(j * nrow, (j + 1) * nrow)
        u = lax.dot_general(w_ref[b_start + j * nrow:b_start + (j + 1) * nrow, 0:kd], h, _NT_DIMS,
                            preferred_element_type=F32)
        prev = jnp.where(pos != 0, pltpu.roll(u, 1, 1), 0.0)
        nxt = jnp.where(pos != period - 1, pltpu.roll(u, tm - 1, 1), 0.0)
        cwb = cwb_ref[rows, :]
        u = prev * cwb[:, 0:1] + u * cwb[:, 1:2] + nxt * cwb[:, 2:3] + cwb[:, 3:4]
        bt_ref[rows, :] = _silu(u).astype(BF16)
    raw = lax.dot_general(wdt_ref[...], h, _NT_DIMS, preferred_element_type=F32)
    v = raw + dtb_ref[...]
    sp = jnp.maximum(v, 0.0) + jnp.log1p(jnp.exp(-jnp.abs(v)))
    row = lax.broadcasted_iota(jnp.int32, (dt_ref.shape[0], 1), 0)
    is_la = jnp.bitwise_and(row, DT_ROWS_PER_GROUP - 1) >= DT_ROWS_PER_GROUP // 2
    dt_ref[...] = jnp.where(is_la, sp * (-jnp.exp(alog_ref[...])), sp)


def _ssd_in_proj(x2d, mod3, mod_row_of_tile, norm_g, w_t, wdt_t, cw_xc, cb_xc, cwb, dtb_col,
                 alog_col, *, tm, period, want_z, dz, xc_ranges, b_range):
    rows, d = x2d.shape
    dxc = cw_xc.shape[1]
    nb = b_range[1] - b_range[0]
    ndt = wdt_t.shape[0]
    ncol, nrow = 512, 256
    xc_starts = tuple(c for lo, hi in xc_ranges for c in range(lo, hi, ncol))
    assert len(xc_starts) * ncol == dxc
    kern = functools.partial(_ssd_in_kernel, period=period, ncol=ncol, nrow=nrow, want_z=want_z,
                             xc_starts=xc_starts, b_start=b_range[0])
    out_specs = [
        pl.BlockSpec((tm, dxc), lambda i: (i, 0)),
        pl.BlockSpec((nb, tm), lambda i: (0, i)),
        pl.BlockSpec((ndt, tm), lambda i: (0, i)),
    ]
    out_shape = [
        jax.ShapeDtypeStruct((rows, dxc), BF16),
        jax.ShapeDtypeStruct((nb, rows), BF16),
        jax.ShapeDtypeStruct((ndt, rows), F32),
    ]
    if want_z:
        out_specs.insert(0, pl.BlockSpec((tm, dz), lambda i: (i, 0)))
        out_shape.insert(0, jax.ShapeDtypeStruct((rows, dz), BF16))
    return pl.pallas_call(
        kern,
        grid=(rows // tm,),
        in_specs=[
            pl.BlockSpec((tm, d), lambda i: (i, 0)),
            pl.BlockSpec((None, 1, d), lambda i: (mod_row_of_tile(i), 0, 0)),
            pl.BlockSpec((None, 1, d), lambda i: (mod_row_of_tile(i), 0, 1)),
            _const_spec((1, d)),
            _const_spec(w_t.shape),
            _const_spec((ndt, d)),
            _const_spec((3, dxc)),
            _const_spec((2 * period, dxc)),
            _const_spec((1, dxc)),
            _const_spec(cwb.shape),
            _const_spec((ndt, 1)),
            _const_spec((ndt, 1)),
        ],
        out_specs=out_specs,
        out_shape=out_shape,
        compiler_params=_cparams(1),
        name="ssd_in_proj",
    )(x2d, mod3, mod3, norm_g, w_t, wdt_t, cw_xc, _edge_masked_taps(cw_xc, period), cb_xc, cwb,
      dtb_col, alog_col)


COL_CS, COL_ECS, COL_W = 0, 8, 16


def _head_expand(colmat, lane0):
    r = colmat.shape[0]
    first = lax.broadcasted_iota(jnp.int32, (r, SSM_STATE), 1) < SSM_HEAD_DIM
    cols = [colmat[:, lane0 + hd:lane0 + hd + 1] for hd in range(HEADS_PER_GROUP)]
    lo = jnp.where(first, cols[0], cols[1])
    hi = jnp.where(first, cols[2], cols[3])
    return jnp.concatenate([lo, hi], axis=1)


def _expand_select(lane0):
    r = np.arange(2 * SSM_STATE)[:, None] % SSM_STATE
    l = np.arange(2 * GROUP_WIDTH)[None, :]
    src = lane0 + (l // GROUP_WIDTH) * HEADS_PER_GROUP + (l % GROUP_WIDTH) // SSM_HEAD_DIM
    return jnp.asarray(r == src, dtype=BF16)


def _broadcast_select(lane0, count):
    r = np.arange(2 * SSM_STATE)[:, None] % SSM_STATE
    l = np.arange(SSM_STATE * count)[None, :]
    return jnp.asarray(r == lane0 + l // SSM_STATE, dtype=BF16)


def _hi_lo(colmat):
    hi = colmat.astype(BF16)
    lo = (colmat - hi.astype(F32)).astype(BF16)
    return jnp.concatenate([hi, lo], axis=1)


def _decay_rows(dt_ref, n_chunks, upper, lower):
    q = SSM_CHUNK
    nh = HEADS_PER_GROUP
    dt = jnp.concatenate([dt_ref[0:2 * nh, c * q:(c + 1) * q] for c in range(n_chunks)], axis=0)
    la = jnp.concatenate([dt_ref[2 * nh:4 * nh, c * q:(c + 1) * q] for c in range(n_chunks)], axis=0)
    csf = jnp.dot(la, upper, preferred_element_type=F32, precision=lax.Precision.HIGHEST)
    csb = jnp.dot(la, lower, preferred_element_type=F32, precision=lax.Precision.HIGHEST)
    rows = dt.shape[0]
    row = lax.broadcasted_iota(jnp.int32, (rows, 1), 0)
    is_b = jnp.bitwise_and(row, nh) != 0
    cs = jnp.where(is_b, csb, csf)
    tot = jnp.where(is_b, csb[:, 0:1], csf[:, q - 1:q])
    other = jnp.where(is_b, pltpu.roll(dt, nh, 0), pltpu.roll(dt, rows - nh, 0))
    return cs, jnp.exp(cs), jnp.exp(tot - cs) * dt, cs - jnp.log(dt), jnp.log(dt + other)


def _ssd_scan_kernel(xl_ref, btl_ref, cl_ref, dtl_ref, xc_ref, btc_ref, dtc_ref,
                     selw_ref, sele_ref, selc_ref, o_ref,
                     cols_ref, rows_ref, xw_ref, ee_ref, csb_ref, cb_ref, sloc_ref, hin_ref, etot_ref):
    q = SSM_CHUNK
    nh = HEADS_PER_GROUP
    gw = GROUP_WIDTH
    n_lat = xl_ref.shape[0] // q
    n_ctx = xc_ref.shape[0] // q
    ctx_rows = n_ctx * q
    ki = lax.broadcasted_iota(jnp.int32, (q, q), 0)
    kj = lax.broadcasted_iota(jnp.int32, (q, q), 1)

    upper = (ki <= kj).astype(F32)
    lower = (ki >= kj).astype(F32)
    pad = jnp.zeros((q - 6 * nh, q), F32)
    for dref, n_chunks, base in ((dtc_ref, n_ctx, 0), (dtl_ref, n_lat, n_ctx)):
        cs, ecs, w, rk, lds = _decay_rows(dref, n_chunks, upper, lower)
        for c in range(n_chunks):
            r = slice(2 * nh * c, 2 * nh * (c + 1))
            colmat = jnp.concatenate([cs[r], ecs[r], w[r], pad], axis=0).T
            cols_ref[(base + c) * q:(base + c + 1) * q, :] = colmat
            for d in range(2):
                p0 = 0 if d else q - 1
                etot_ref[d, base + c] = _head_expand(colmat[p0:p0 + 1, :], COL_ECS + nh * d)
        if base:
            rows_ref[0:2 * nh * n_lat, :] = rk
            rows_ref[2 * nh * n_lat:4 * nh * n_lat, :] = lds

    def expand(hl, sel_ref):
        return jnp.dot(hl, sel_ref[...], preferred_element_type=F32)

    xf = xc_ref[...].astype(F32)
    hl = _hi_lo(cols_ref[0:ctx_rows, :])
    xw_ref[0:ctx_rows, 0:2 * gw] = (jnp.concatenate([xf, xf], axis=1) * expand(hl, selw_ref)).astype(BF16)
    blk = 4 * q
    for j in range(n_lat * q // blk):
        rows = slice(j * blk, (j + 1) * blk)
        xf = xl_ref[rows, :].astype(F32)
        r0 = ctx_rows + j * blk
        hl = _hi_lo(cols_ref[r0:r0 + blk, :])
        xw_ref[r0:r0 + blk, 0:2 * gw] = (jnp.concatenate([xf, xf], axis=1) * expand(hl, selw_ref)).astype(BF16)
        ee_ref[rows, 0:2 * gw] = expand(hl, sele_ref)
        csb_ref[rows, 0:nh * q] = expand(hl, selc_ref)

    for c in range(n_ctx):
        sloc_ref[c, :, 0:2 * gw] = jnp.dot(btc_ref[:, c * q:(c + 1) * q], xw_ref[c * q:(c + 1) * q, 0:2 * gw],
                              preferred_element_type=F32)

    def s_body(c, carry):
        r0 = pl.multiple_of(c * q, q)
        bt = btl_ref[:, pl.ds(r0, q)]
        sloc_ref[n_ctx + c, :, 0:2 * gw] = jnp.dot(bt, xw_ref[pl.ds(ctx_rows + r0, q), 0:2 * gw],
                                                   preferred_element_type=F32)
        cb_ref[c] = jnp.dot(cl_ref[pl.ds(r0, q), :], bt, preferred_element_type=F32)
        return carry

    lax.fori_loop(0, n_lat, s_body, 0, unroll=True)

    for d in range(2):
        lanes = slice(gw * d, gw * (d + 1))
        h = jnp.zeros((SSM_STATE, gw), F32)
        for c in (range(n_ctx - 1, -1, -1) if d else range(n_ctx)):
            h = h * etot_ref[d, c] + sloc_ref[c, :, lanes]

        def b_body(i, h, d=d, lanes=lanes):
            c = (n_lat - 1 - i) if d else i
            hin_ref[c, :, lanes] = h.astype(BF16)
            return h * etot_ref[d, n_ctx + c] + sloc_ref[n_ctx + c, :, lanes]

        lax.fori_loop(0, n_lat, b_body, h, unroll=True)

    lane_head = lax.broadcasted_iota(jnp.int32, (q, gw), 1) // SSM_HEAD_DIM
    below = ki > kj
    above = ki < kj

    def c_body(c, carry):
        r0 = pl.multiple_of(c * q, q)
        x = xl_ref[pl.ds(r0, q), :]
        r8 = pl.multiple_of(c * 2 * nh, 2 * nh)
        rk8 = rows_ref[pl.ds(r8, 2 * nh), :]
        lds8 = rows_ref[pl.ds(2 * nh * n_lat + r8, 2 * nh), :]
        cb = cb_ref[c]
        colmat = cols_ref[pl.ds(ctx_rows + r0, q), :]
        m_parts = []
        x_parts = []
        for hd in range(nh):
            segf = csb_ref[pl.ds(r0, q), q * hd:q * (hd + 1)] - rk8[hd:hd + 1, :]
            segb = colmat[:, COL_CS + nh + hd:COL_CS + nh + hd + 1] - rk8[nh + hd:nh + hd + 1, :]
            arg = jnp.where(below, segf, jnp.where(above, segb, lds8[hd:hd + 1, :]))
            m_parts.append((jnp.exp(arg) * cb).astype(BF16))
            x_parts.append(jnp.where(lane_head == hd, x, jnp.zeros_like(x)))
        m_all = jnp.concatenate(m_parts, axis=1)
        x_bd = jnp.concatenate(x_parts, axis=0)
        y = jnp.dot(m_all, x_bd, preferred_element_type=F32)
        y_off = ee_ref[pl.ds(r0, q), 0:2 * gw] * jnp.dot(cl_ref[pl.ds(r0, q), :], hin_ref[c, :, 0:2 * gw],
                                                  preferred_element_type=F32)
        o_ref[pl.ds(r0, q), :] = (y + y_off[:, 0:gw] + y_off[:, gw:2 * gw]).astype(BF16)
        return carry

    lax.fori_loop(0, n_lat, c_body, 0, unroll=True)


def _ssd_scan(xc_l, bt_l, dt_l, xc_c, bt_c, dt_c, *, batch, seq, ctx_len):
    g = SSM_GROUPS
    gw = GROUP_WIDTH
    n = SSM_STATE
    c_off = (g * gw) // n
    n_lat = seq // SSM_CHUNK
    n_all = n_lat + ctx_len // SSM_CHUNK
    sel_w = _expand_select(COL_W)
    sel_e = _expand_select(COL_ECS)
    sel_c = _broadcast_select(COL_CS, HEADS_PER_GROUP)
    return pl.pallas_call(
        _ssd_scan_kernel,
        grid=(batch, g),
        in_specs=[
            pl.BlockSpec((seq, gw), lambda b, k: (b, k)),
            pl.BlockSpec((n, seq), lambda b, k: (k, b)),
            pl.BlockSpec((seq, n), lambda b, k: (b, c_off + k)),
            pl.BlockSpec((DT_ROWS_PER_GROUP, seq), lambda b, k: (k, b)),
            pl.BlockSpec((ctx_len, gw), lambda b, k: (b, k)),
            pl.BlockSpec((n, ctx_len), lambda b, k: (k, b)),
            pl.BlockSpec((DT_ROWS_PER_GROUP, ctx_len), lambda b, k: (k, b)),
            _const_spec(sel_w.shape),
            _const_spec(sel_e.shape),
            _const_spec(sel_c.shape),
        ],
        out_specs=pl.BlockSpec((seq, gw), lambda b, k: (b, k)),
        out_shape=jax.ShapeDtypeStruct((batch * seq, g * gw), BF16),
        scratch_shapes=[
            pltpu.VMEM((n_all * SSM_CHUNK, SSM_CHUNK), F32),
            pltpu.VMEM((4 * HEADS_PER_GROUP * n_lat, SSM_CHUNK), F32),
            pltpu.VMEM((n_all * SSM_CHUNK, 2 * gw + LANE_PAD), BF16),
            pltpu.VMEM((seq, 2 * gw + LANE_PAD), F32),
            pltpu.VMEM((seq, HEADS_PER_GROUP * SSM_CHUNK + LANE_PAD), F32),
            pltpu.VMEM((n_lat, SSM_CHUNK, SSM_CHUNK), F32),
            pltpu.VMEM((n_all, n, 2 * gw + LANE_PAD), F32),
            pltpu.VMEM((n_lat, n, 2 * gw + LANE_PAD), BF16),
            pltpu.VMEM((2, n_all, 1, gw), F32),
        ],
        compiler_params=_cparams(2),
        name="ssd_scan",
    )(xc_l, bt_l, xc_l, dt_l, xc_c, bt_c, dt_c, sel_w, sel_e, sel_c)


def _mlp_tail(x1, g_ref, sh_ref, sc_ref, gate_ref, w1_ref, w2_ref, nff):
    h2 = _norm_mod(x1, g_ref[...], sh_ref[...], sc_ref[...]).astype(BF16)
    d = x1.shape[1]
    dff = w2_ref.shape[0]
    acc = None
    for j in range(dff // nff):
        c0 = j * nff
        a = jnp.dot(h2, w1_ref[:, c0:c0 + nff], preferred_element_type=F32)
        a = jnp.square(jnp.maximum(a, 0.0)).astype(BF16)
        p = jnp.dot(a, w2_ref[c0:c0 + nff, 0:d], preferred_element_type=F32)
        acc = p if acc is None else acc + p
    return x1 + gate_ref[...] * acc


def _ssd_out_kernel(y_ref, xs_ref, z_ref, x_ref, dsk_ref, sng_ref, gm_ref, shf_ref, scf_ref, gf_ref,
                    ng_ref, wo_ref, w1_ref, w2_ref, o_ref):
    acc = None
    for g0 in range(0, y_ref.shape[1], GROUP_WIDTH):
        cols = slice(g0, g0 + GROUP_WIDTH)
        y = y_ref[:, cols].astype(F32) + dsk_ref[:, cols] * xs_ref[:, cols].astype(F32)
        y = y * _silu(z_ref[:, cols]).astype(F32)
        y = y * lax.rsqrt(jnp.mean(y * y, axis=-1, keepdims=True) + EPS)
        yn = (y * sng_ref[:, cols]).astype(BF16)
        p = jnp.dot(yn, wo_ref[cols, 0:x_ref.shape[1]], preferred_element_type=F32)
        acc = p if acc is None else acc + p
    x1 = x_ref[...] + gm_ref[...] * acc
    o_ref[...] = _mlp_tail(x1, ng_ref, shf_ref, scf_ref, gf_ref, w1_ref, w2_ref, 1024)


def _mod_spec(d, row_of_tile, k):
    return pl.BlockSpec((None, 1, d), lambda i: (row_of_tile(i), 0, k))


def _ssd_out_mlp(y2d, xc2d, z2d, x2d, dskip_row, ssd_ng_row, mod3, row_of_tile, norm_g, wo, w1, w2, *, tm):
    rows, d = x2d.shape
    di = y2d.shape[1]
    return pl.pallas_call(
        _ssd_out_kernel,
        grid=(rows // tm,),
        in_specs=[
            pl.BlockSpec((tm, di), lambda i: (i, 0)),
            pl.BlockSpec((tm, di), lambda i: (i, 0)),
            pl.BlockSpec((tm, di), lambda i: (i, 0)),
            pl.BlockSpec((tm, d), lambda i: (i, 0)),
            _const_spec((1, di)),
            _const_spec((1, di)),
            _mod_spec(d, row_of_tile, 2),
            _mod_spec(d, row_of_tile, 3),
            _mod_spec(d, row_of_tile, 4),
            _mod_spec(d, row_of_tile, 5),
            _const_spec((1, d)),
            _const_spec(wo.shape),
            _const_spec(w1.shape),
            _const_spec(w2.shape),
        ],
        out_specs=pl.BlockSpec((tm, d), lambda i: (i, 0)),
        out_shape=jax.ShapeDtypeStruct((rows, d), F32),
        compiler_params=_cparams(1),
        name="ssd_out_mlp",
    )(y2d, xc2d, z2d, x2d, dskip_row, ssd_ng_row, mod3, mod3, mod3, mod3, norm_g, wo, w1, w2)


def _sc_layer_kernel(x_ref, shm_ref, scm_ref, gm_ref, shf_ref, scf_ref, gf_ref, ngm_ref, ngf_ref,
                     fg_ref, wi_ref, cw_ref, cwm_ref, wo_ref, w1_ref, w2_ref, o_ref, *, period):
    x = x_ref[...]
    h = _norm_mod(x, ngm_ref[...], shm_ref[...], scm_ref[...]).astype(BF16)
    w = wo_ref.shape[0]
    bg = jnp.dot(h, wi_ref[:, 0:w], preferred_element_type=F32)
    cg = jnp.dot(h, wi_ref[:, w:2 * w], preferred_element_type=F32)
    xv = jnp.dot(h, wi_ref[:, 2 * w:3 * w], preferred_element_type=F32)
    u = (bg * _row_conv3(cg * xv, cw_ref, cwm_ref, slice(0, w), period)).astype(BF16)
    y = jnp.dot(u, wo_ref[:, 0:x.shape[1]], preferred_element_type=F32)
    x1 = x + gm_ref[...] * y
    x2 = _mlp_tail(x1, ngf_ref, shf_ref, scf_ref, gf_ref, w1_ref, w2_ref, 1024)
    ms = jnp.mean(x2 * x2, axis=-1, keepdims=True)
    o_ref[...] = x2 * lax.rsqrt(ms + EPS) * fg_ref[...]


def _sc_layer(x2d, mod3, row_of_tile, ng_mix, ng_mlp, final_g, wi, conv_w, wo, w1, w2, *, tm, period):
    rows, d = x2d.shape
    kern = functools.partial(_sc_layer_kernel, period=period)
    return pl.pallas_call(
        kern,
        grid=(rows // tm,),
        in_specs=[
            pl.BlockSpec((tm, d), lambda i: (i, 0)),
            _mod_spec(d, row_of_tile, 0),
            _mod_spec(d, row_of_tile, 1),
            _mod_spec(d, row_of_tile, 2),
            _mod_spec(d, row_of_tile, 3),
            _mod_spec(d, row_of_tile, 4),
            _mod_spec(d, row_of_tile, 5),
            _const_spec((1, d)),
            _const_spec((1, d)),
            _const_spec((1, d)),
            _const_spec(wi.shape),
            _const_spec(conv_w.shape),
            _const_spec((2 * period, conv_w.shape[1])),
            _const_spec(wo.shape),
            _const_spec(w1.shape),
            _const_spec(w2.shape),
        ],
        out_specs=pl.BlockSpec((tm, d), lambda i: (i, 0)),
        out_shape=jax.ShapeDtypeStruct((rows, d), F32),
        compiler_params=_cparams(1),
        name="shortconv_layer",
    )(x2d, mod3, mod3, mod3, mod3, mod3, mod3, ng_mix, ng_mlp, final_g, wi, conv_w,
      _edge_masked_taps(conv_w, period), wo, w1, w2)


def kernel(x, c, ctx, c_ctx, ada_w, ada_b, norm_mix_g, norm_mlp_g, ssd_w_in, ssd_conv_w, ssd_conv_b,
           ssd_dt_bias, ssd_a_log, ssd_d, ssd_norm_g, ssd_w_out, sc_w_in, sc_conv_w, sc_w_out,
           mlp_w1, mlp_w2, final_norm_g):
    batch, seq, d = x.shape
    ctx_len = ctx.shape[1]
    depth = ada_w.shape[0]
    assert depth == 2 and ssd_w_in.shape[0] == 1 and sc_w_in.shape[0] == 1
    d_inner = ssd_w_out.shape[1]
    n_heads = ssd_d.shape[1]
    xbc_dim = ssd_conv_w.shape[2]
    assert n_heads == SSM_GROUPS * HEADS_PER_GROUP and d_inner == SSM_GROUPS * GROUP_WIDTH

    mod_rows = 16
    cvec = jnp.zeros((mod_rows, d), F32).at[:batch].set(c).at[batch].set(c_ctx)
    mod = _modulation(cvec, ada_w, ada_b)
    mod0 = mod[0].reshape(mod_rows, 1, 6 * d)
    mod1 = mod[1].reshape(mod_rows, 1, 6 * d)

    nbc = SSM_GROUPS * SSM_STATE
    x_rng = (d_inner, 2 * d_inner)
    b_rng = (2 * d_inner, 2 * d_inner + nbc)
    c_rng = (2 * d_inner + nbc, 2 * d_inner + 2 * nbc)
    n_proj = ssd_w_in.shape[2]
    w_t = _bf16_padded(jnp.swapaxes(ssd_w_in[0], 0, 1), block_rows=n_proj // 4)
    cw = ssd_conv_w[0]
    cbias = ssd_conv_b[0]
    cw_xc = jnp.concatenate([cw[:, :d_inner], cw[:, d_inner + nbc:]], axis=1)
    cb_xc = jnp.concatenate([cbias[:d_inner], cbias[d_inner + nbc:]]).reshape(1, -1)
    cwb = jnp.concatenate([cw[:, d_inner:d_inner + nbc].T, cbias[d_inner:d_inner + nbc, None],
                           jnp.zeros((nbc, 4), F32)], axis=1)
    gi = jnp.arange(SSM_GROUPS)[:, None, None]
    di_ = jnp.arange(2)[None, :, None]
    ri = jnp.arange(HEADS_PER_GROUP)[None, None, :]
    flat = (di_ * n_heads + gi * HEADS_PER_GROUP + ri).reshape(SSM_GROUPS, 2 * HEADS_PER_GROUP)
    dt_idx = jnp.concatenate([flat, flat], axis=1).reshape(-1)
    wdt_t = w_t[d_inner + xbc_dim:, :d][dt_idx]
    dtb_col = ssd_dt_bias[0].reshape(-1)[dt_idx].reshape(-1, 1).astype(F32)
    alog_col = ssd_a_log[0].reshape(-1)[dt_idx].reshape(-1, 1).astype(F32)
    ng_mix0 = norm_mix_g[0].reshape(1, d)

    tm = 512
    x2d = x.reshape(batch * seq, d)
    ctx2d = ctx.reshape(batch * ctx_len, d)
    lat_row = lambda i: (i * tm) // seq
    tm_in = 256
    z_l, xc_l, bt_l, dt_l = _ssd_in_proj(x2d, mod0, lambda i: (i * tm_in) // seq, ng_mix0, w_t,
                                         wdt_t, cw_xc, cb_xc, cwb, dtb_col, alog_col, tm=tm_in,
                                         period=GRID_W, want_z=True, dz=d_inner, xc_ranges=(x_rng, c_rng),
                                         b_range=b_rng)
    xc_c, bt_c, dt_c = _ssd_in_proj(ctx2d, mod0, lambda i: batch, ng_mix0, w_t, wdt_t,
                                    cw_xc[:, :d_inner], cb_xc[:, :d_inner], cwb, dtb_col, alog_col,
                                    tm=ctx_len, period=ctx_len, want_z=False, dz=d_inner,
                                    xc_ranges=(x_rng,), b_range=b_rng)

    dskip_row = jnp.repeat(ssd_d[0].astype(F32), SSM_HEAD_DIM).reshape(1, d_inner)
    ng_row = ssd_norm_g[0].reshape(1, d_inner)
    y_ssd = _ssd_scan(xc_l, bt_l, dt_l, xc_c, bt_c, dt_c, batch=batch, seq=seq, ctx_len=ctx_len)

    x1 = _ssd_out_mlp(y_ssd, xc_l, z_l, x2d, dskip_row, ng_row, mod0, lat_row, norm_mlp_g[0].reshape(1, d),
                      ssd_w_out[0].astype(BF16), mlp_w1[0].astype(BF16), mlp_w2[0].astype(BF16), tm=tm)

    out = _sc_layer(x1, mod1, lat_row, norm_mix_g[1].reshape(1, d), norm_mlp_g[1].reshape(1, d),
                    final_norm_g.reshape(1, d), sc_w_in[0].astype(BF16), sc_conv_w[0],
                    sc_w_out[0].astype(BF16), mlp_w1[1].astype(BF16), mlp_w2[1].astype(BF16),
                    tm=tm, period=GRID_W)
    return out.reshape(batch, seq, d)
```

```python
import functools

import numpy as np

import jax
import jax.numpy as jnp
from jax import lax
from jax.experimental import pallas as pl
from jax.experimental.pallas import tpu as pltpu

F32 = jnp.float32
BF16 = jnp.bfloat16

EPS = 1e-6
GRID_W = 64
SSM_HEAD_DIM = 64
SSM_GROUPS = 8
HEADS_PER_GROUP = 4
SSM_STATE = 128
SSM_CHUNK = 128
GROUP_WIDTH = HEADS_PER_GROUP * SSM_HEAD_DIM
DT_ROWS_PER_GROUP = 16

VMEM_LIMIT_BYTES = 56 * 1024 * 1024


def _cparams(n_axes):
    return pltpu.CompilerParams(
        dimension_semantics=("arbitrary",) * n_axes,
        vmem_limit_bytes=VMEM_LIMIT_BYTES,
    )


LANE_PAD = 128


def _cast_pad_kernel(w_ref, o_ref):
    k = w_ref.shape[1]
    o_ref[:, 0:k] = w_ref[...].astype(o_ref.dtype)
    o_ref[:, k:] = jnp.zeros((o_ref.shape[0], o_ref.shape[1] - k), o_ref.dtype)


def _bf16_padded(w, block_rows):
    rows, k = w.shape
    assert rows % block_rows == 0 and block_rows % 16 == 0
    return pl.pallas_call(
        _cast_pad_kernel,
        grid=(rows // block_rows,),
        in_specs=[pl.BlockSpec((block_rows, k), lambda i: (i, 0))],
        out_specs=pl.BlockSpec((block_rows, k + LANE_PAD), lambda i: (i, 0)),
        out_shape=jax.ShapeDtypeStruct((rows, k + LANE_PAD), BF16),
        compiler_params=_cparams(1),
        name="cast_pad_bf16",
    )(w)


def _const_spec(shape):
    nd = len(shape)
    return pl.BlockSpec(shape, lambda *_: (0,) * nd, pipeline_mode=pl.Buffered(1))


def _silu(u):
    return u * (1.0 / (1.0 + jnp.exp(-u)))


def _norm_mod(x, g, shift, scale):
    ms = jnp.mean(x * x, axis=-1, keepdims=True)
    y = x * lax.rsqrt(ms + EPS) * g
    return y * (1.0 + scale) + shift


def _edge_masked_taps(conv_w, period):
    r = jnp.arange(period)[:, None]
    return jnp.concatenate([jnp.where(r != 0, conv_w[0][None, :], 0.0),
                            jnp.where(r != period - 1, conv_w[2][None, :], 0.0)], axis=0)


def _row_conv3(u, w_ref, wm_ref, cols, period):
    rows = u.shape[0]
    reps = rows // period
    w_prev = jnp.concatenate([wm_ref[0:period, cols]] * reps, axis=0)
    w_next = jnp.concatenate([wm_ref[period:2 * period, cols]] * reps, axis=0)
    return pltpu.roll(u, 1, 0) * w_prev + u * w_ref[1:2, cols] + pltpu.roll(u, rows - 1, 0) * w_next


def _mod_kernel(c_ref, w_ref, b_ref, o_ref):
    s = _silu(c_ref[...]).astype(BF16)
    o_ref[...] = jnp.dot(s, w_ref[...].astype(BF16), preferred_element_type=F32) + b_ref[...]


def _modulation(cvec, ada_w, ada_b):
    depth, d, n = ada_w.shape
    rows = cvec.shape[0]
    tn = 1536
    return pl.pallas_call(
        _mod_kernel,
        grid=(depth, n // tn),
        in_specs=[
            pl.BlockSpec((rows, d), lambda i, j: (0, 0)),
            pl.BlockSpec((None, d, tn), lambda i, j: (i, 0, j)),
            pl.BlockSpec((None, 1, tn), lambda i, j: (i, 0, j)),
        ],
        out_specs=pl.BlockSpec((None, rows, tn), lambda i, j: (i, 0, j)),
        out_shape=jax.ShapeDtypeStruct((depth, rows, n), F32),
        compiler_params=_cparams(2),
        name="adaln_mod",
    )(cvec, ada_w, ada_b.reshape(depth, 1, n))


_NT_DIMS = (((1,), (1,)), ((), ()))


def _ssd_in_kernel(x_ref, sh_ref, sc_ref, g_ref, w_ref, wdt_ref, cw_ref, cwm_ref, cb_ref,
                   cwb_ref, dtb_ref, alog_ref, *out_refs, period, ncol, nrow, want_z, xc_starts, b_start):
    if want_z:
        z_ref, xc_ref, bt_ref, dt_ref = out_refs
    else:
        xc_ref, bt_ref, dt_ref = out_refs
    h = _norm_mod(x_ref[...], g_ref[...], sh_ref[...], sc_ref[...]).astype(BF16)
    kd = x_ref.shape[1]
    if want_z:
        z_ref[...] = lax.dot_general(h, w_ref[0:z_ref.shape[1], 0:kd], _NT_DIMS,
                                     preferred_element_type=F32).astype(BF16)
    for j, w0 in enumerate(xc_starts):
        cols = slice(j * ncol, (j + 1) * ncol)
        u = lax.dot_general(h, w_ref[w0:w0 + ncol, 0:kd], _NT_DIMS, preferred_element_type=F32)
        u = _row_conv3(u, cw_ref, cwm_ref, cols, period) + cb_ref[:, cols]
        xc_ref[:, cols] = _silu(u).astype(BF16)
    tm = x_ref.shape[0]
    pos = jnp.bitwise_and(lax.broadcasted_iota(jnp.int32, (1, tm), 1), period - 1)
    for j in range(bt_ref.shape[0] // nrow):
        rows = slice(j * nrow, (j + 1) * nrow)
        u = lax.dot_general(w_ref[b_start + j * nrow:b_start + (j + 1) * nrow, 0:kd], h, _NT_DIMS,
                            preferred_element_type=F32)
        prev = jnp.where(pos != 0, pltpu.roll(u, 1, 1), 0.0)
        nxt = jnp.where(pos != period - 1, pltpu.roll(u, tm - 1, 1), 0.0)
        cwb = cwb_ref[rows, :]
        u = prev * cwb[:, 0:1] + u * cwb[:, 1:2] + nxt * cwb[:, 2:3] + cwb[:, 3:4]
        bt_ref[rows, :] = _silu(u).astype(BF16)
    raw = lax.dot_general(wdt_ref[...], h, _NT_DIMS, preferred_element_type=F32)
    v = raw + dtb_ref[...]
    sp = jnp.maximum(v, 0.0) + jnp.log1p(jnp.exp(-jnp.abs(v)))
    row = lax.broadcasted_iota(jnp.int32, (dt_ref.shape[0], 1), 0)
    is_la = jnp.bitwise_and(row, DT_ROWS_PER_GROUP - 1) >= DT_ROWS_PER_GROUP // 2
    dt_ref[...] = jnp.where(is_la, sp * (-jnp.exp(alog_ref[...])), sp)


def _ssd_in_proj(x2d, mod3, mod_row_of_tile, norm_g, w_t, wdt_t, cw_xc, cb_xc, cwb, dtb_col,
                 alog_col, *, tm, period, want_z, dz, xc_ranges, b_range):
    rows, d = x2d.shape
    dxc = cw_xc.shape[1]
    nb = b_range[1] - b_range[0]
    ndt = wdt_t.shape[0]
    ncol, nrow = 512, 256
    xc_starts = tuple(c for lo, hi in xc_ranges for c in range(lo, hi, ncol))
    assert len(xc_starts) * ncol == dxc
    kern = functools.partial(_ssd_in_kernel, period=period, ncol=ncol, nrow=nrow, want_z=want_z,
                             xc_starts=xc_starts, b_start=b_range[0])
    out_specs = [
        pl.BlockSpec((tm, dxc), lambda i: (i, 0)),
        pl.BlockSpec((nb, tm), lambda i: (0, i)),
        pl.BlockSpec((ndt, tm), lambda i: (0, i)),
    ]
    out_shape = [
        jax.ShapeDtypeStruct((rows, dxc), BF16),
        jax.ShapeDtypeStruct((nb, rows), BF16),
        jax.ShapeDtypeStruct((ndt, rows), F32),
    ]
    if want_z:
        out_specs.insert(0, pl.BlockSpec((tm, dz), lambda i: (i, 0)))
        out_shape.insert(0, jax.ShapeDtypeStruct((rows, dz), BF16))
    return pl.pallas_call(
        kern,
        grid=(rows // tm,),
        in_specs=[
            pl.BlockSpec((tm, d), lambda i: (i, 0)),
            pl.BlockSpec((None, 1, d), lambda i: (mod_row_of_tile(i), 0, 0)),
            pl.BlockSpec((None, 1, d), lambda i: (mod_row_of_tile(i), 0, 1)),
            _const_spec((1, d)),
            _const_spec(w_t.shape),
            _const_spec((ndt, d)),
            _const_spec((3, dxc)),
            _const_spec((2 * period, dxc)),
            _const_spec((1, dxc)),
            _const_spec(cwb.shape),
            _const_spec((ndt, 1)),
            _const_spec((ndt, 1)),
        ],
        out_specs=out_specs,
        out_shape=out_shape,
        compiler_params=_cparams(1),
        name="ssd_in_proj",
    )(x2d, mod3, mod3, norm_g, w_t, wdt_t, cw_xc, _edge_masked_taps(cw_xc, period), cb_xc, cwb,
      dtb_col, alog_col)


COL_CS, COL_ECS, COL_W = 0, 8, 16


def _head_expand(colmat, lane0):
    r = colmat.shape[0]
    first = lax.broadcasted_iota(jnp.int32, (r, SSM_STATE), 1) < SSM_HEAD_DIM
    cols = [colmat[:, lane0 + hd:lane0 + hd + 1] for hd in range(HEADS_PER_GROUP)]
    lo = jnp.where(first, cols[0], cols[1])
    hi = jnp.where(first, cols[2], cols[3])
    return jnp.concatenate([lo, hi], axis=1)


def _expand_select(lane0):
    r = np.arange(2 * SSM_STATE)[:, None] % SSM_STATE
    l = np.arange(2 * GROUP_WIDTH)[None, :]
    src = lane0 + (l // GROUP_WIDTH) * HEADS_PER_GROUP + (l % GROUP_WIDTH) // SSM_HEAD_DIM
    return jnp.asarray(r == src, dtype=BF16)


def _broadcast_select(lane0, count):
    r = np.arange(2 * SSM_STATE)[:, None] % SSM_STATE
    l = np.arange(SSM_STATE * count)[None, :]
    return jnp.asarray(r == lane0 + l // SSM_STATE, dtype=BF16)


def _hi_lo(colmat):
    hi = colmat.astype(BF16)
    lo = (colmat - hi.astype(F32)).astype(BF16)
    return jnp.concatenate([hi, lo], axis=1)


def _decay_rows(dt_ref, n_chunks, upper, lower):
    q = SSM_CHUNK
    nh = HEADS_PER_GROUP
    dt = jnp.concatenate([dt_ref[0:2 * nh, c * q:(c + 1) * q] for c in range(n_chunks)], axis=0)
    la = jnp.concatenate([dt_ref[2 * nh:4 * nh, c * q:(c + 1) * q] for c in range(n_chunks)], axis=0)
    csf = jnp.dot(la, upper, preferred_element_type=F32, precision=lax.Precision.HIGHEST)
    csb = jnp.dot(la, lower, preferred_element_type=F32, precision=lax.Precision.HIGHEST)
    rows = dt.shape[0]
    row = lax.broadcasted_iota(jnp.int32, (rows, 1), 0)
    is_b = jnp.bitwise_and(row, nh) != 0
    cs = jnp.where(is_b, csb, csf)
    tot = jnp.where(is_b, csb[:, 0:1], csf[:, q - 1:q])
    other = jnp.where(is_b, pltpu.roll(dt, nh, 0), pltpu.roll(dt, rows - nh, 0))
    return cs, jnp.exp(cs), jnp.exp(tot - cs) * dt, cs - jnp.log(dt), jnp.log(dt + other)


def _ssd_scan_kernel(xl_ref, btl_ref, cl_ref, dtl_ref, xc_ref, btc_ref, dtc_ref,
                     selw_ref, sele_ref, selc_ref, o_ref,
                     cols_ref, rows_ref, xw_ref, ee_ref, csb_ref, cb_ref, sloc_ref, hin_ref, etot_ref):
    q = SSM_CHUNK
    nh = HEADS_PER_GROUP
    gw = GROUP_WIDTH
    n_lat = xl_ref.shape[0] // q
    n_ctx = xc_ref.shape[0] // q
    ctx_rows = n_ctx * q
    ki = lax.broadcasted_iota(jnp.int32, (q, q), 0)
    kj = lax.broadcasted_iota(jnp.int32, (q, q), 1)

    upper = (ki <= kj).astype(F32)
    lower = (ki >= kj).astype(F32)
    pad = jnp.zeros((q - 6 * nh, q), F32)
    for dref, n_chunks, base in ((dtc_ref, n_ctx, 0), (dtl_ref, n_lat, n_ctx)):
        cs, ecs, w, rk, lds = _decay_rows(dref, n_chunks, upper, lower)
        for c in range(n_chunks):
            r = slice(2 * nh * c, 2 * nh * (c + 1))
            colmat = jnp.concatenate([cs[r], ecs[r], w[r], pad], axis=0).T
            cols_ref[(base + c) * q:(base + c + 1) * q, :] = colmat
            for d in range(2):
                p0 = 0 if d else q - 1
                etot_ref[d, base + c] = _head_expand(colmat[p0:p0 + 1, :], COL_ECS + nh * d)
        if base:
            rows_ref[0:2 * nh * n_lat, :] = rk
            rows_ref[2 * nh * n_lat:4 * nh * n_lat, :] = lds

    def expand(hl, sel_ref):
        return jnp.dot(hl, sel_ref[...], preferred_element_type=F32)

    xf = xc_ref[...].astype(F32)
    hl = _hi_lo(cols_ref[0:ctx_rows, :])
    xw_ref[0:ctx_rows, 0:2 * gw] = (jnp.concatenate([xf, xf], axis=1) * expand(hl, selw_ref)).astype(BF16)
    blk = 4 * q
    for j in range(n_lat * q // blk):
        rows = slice(j * blk, (j + 1) * blk)
        xf = xl_ref[rows, :].astype(F32)
        r0 = ctx_rows + j * blk
        hl = _hi_lo(cols_ref[r0:r0 + blk, :])
        xw_ref[r0:r0 + blk, 0:2 * gw] = (jnp.concatenate([xf, xf], axis=1) * expand(hl, selw_ref)).astype(BF16)
        ee_ref[rows, 0:gw] = expand(hl, sele_ref)
        csb_ref[rows, 0:nh * q] = expand(hl, selc_ref)

    for c in range(n_ctx):
        sloc_ref[c, :, 0:2 * gw] = jnp.dot(btc_ref[:, c * q:(c + 1) * q], xw_ref[c * q:(c + 1) * q, 0:2 * gw],
                              preferred_element_type=F32)

    def s_body(c, carry):
        r0 = pl.multiple_of(c * q, q)
        bt = btl_ref[:, pl.ds(r0, q)]
        sloc_ref[n_ctx + c, :, 0:2 * gw] = jnp.dot(bt, xw_ref[pl.ds(ctx_rows + r0, q), 0:2 * gw],
                                                   preferred_element_type=F32)
        cb_ref[c] = jnp.dot(cl_ref[pl.ds(r0, q), :], bt, preferred_element_type=F32)
        return carry

    lax.fori_loop(0, n_lat, s_body, 0, unroll=True)

    for d in range(2):
        lanes = slice(gw * d, gw * (d + 1))
        h = jnp.zeros((SSM_STATE, gw), F32)
        for c in (range(n_ctx - 1, -1, -1) if d else range(n_ctx)):
            h = h * etot_ref[d, c] + sloc_ref[c, :, lanes]

        def b_body(i, h, d=d, lanes=lanes):
            c = (n_lat - 1 - i) if d else i
            hin_ref[c, :, lanes] = h.astype(BF16)
            return h * etot_ref[d, n_ctx + c] + sloc_ref[n_ctx + c, :, lanes]

        lax.fori_loop(0, n_lat, b_body, h, unroll=True)

    lane_head = lax.broadcasted_iota(jnp.int32, (q, gw), 1) // SSM_HEAD_DIM
    below = ki > kj
    above = ki < kj

    def c_body(c, carry):
        r0 = pl.multiple_of(c * q, q)
        x = xl_ref[pl.ds(r0, q), :]
        r8 = pl.multiple_of(c * 2 * nh, 2 * nh)
        rk8 = rows_ref[pl.ds(r8, 2 * nh), :]
        lds8 = rows_ref[pl.ds(2 * nh * n_lat + r8, 2 * nh), :]
        cb = cb_ref[c]
        colmat = cols_ref[pl.ds(ctx_rows + r0, q), :]
        m_parts = []
        x_parts = []
        for hd in range(nh):
            segf = csb_ref[pl.ds(r0, q), q * hd:q * (hd + 1)] - rk8[hd:hd + 1, :]
            segb = colmat[:, COL_CS + nh + hd:COL_CS + nh + hd + 1] - rk8[nh + hd:nh + hd + 1, :]
            arg = jnp.where(below, segf, jnp.where(above, segb, lds8[hd:hd + 1, :]))
            m_parts.append((jnp.exp(arg) * cb).astype(BF16))
            x_parts.append(jnp.where(lane_head == hd, x, jnp.zeros_like(x)))
        m_all = jnp.concatenate(m_parts, axis=1)
        x_bd = jnp.concatenate(x_parts, axis=0)
        y = jnp.dot(m_all, x_bd, preferred_element_type=F32)
        ch = jnp.dot(cl_ref[pl.ds(r0, q), :], hin_ref[c, :, 0:2 * gw], preferred_element_type=F32)
        y = y + ee_ref[pl.ds(r0, q), 0:gw] * ch[:, 0:gw]
        y = y + _head_expand(colmat, COL_ECS + nh) * ch[:, gw:2 * gw]
        o_ref[pl.ds(r0, q), :] = y.astype(BF16)
        return carry

    lax.fori_loop(0, n_lat, c_body, 0, unroll=True)


def _ssd_scan(xc_l, bt_l, dt_l, xc_c, bt_c, dt_c, *, batch, seq, ctx_len):
    g = SSM_GROUPS
    gw = GROUP_WIDTH
    n = SSM_STATE
    c_off = (g * gw) // n
    n_lat = seq // SSM_CHUNK
    n_all = n_lat + ctx_len // SSM_CHUNK
    sel_w = _expand_select(COL_W)
    sel_e = _expand_select(COL_ECS)[:, :gw]
    sel_c = _broadcast_select(COL_CS, HEADS_PER_GROUP)
    return pl.pallas_call(
        _ssd_scan_kernel,
        grid=(batch, g),
        in_specs=[
            pl.BlockSpec((seq, gw), lambda b, k: (b, k)),
            pl.BlockSpec((n, seq), lambda b, k: (k, b)),
            pl.BlockSpec((seq, n), lambda b, k: (b, c_off + k)),
            pl.BlockSpec((DT_ROWS_PER_GROUP, seq), lambda b, k: (k, b)),
            pl.BlockSpec((ctx_len, gw), lambda b, k: (b, k)),
            pl.BlockSpec((n, ctx_len), lambda b, k: (k, b)),
            pl.BlockSpec((DT_ROWS_PER_GROUP, ctx_len), lambda b, k: (k, b)),
            _const_spec(sel_w.shape),
            _const_spec(sel_e.shape),
            _const_spec(sel_c.shape),
        ],
        out_specs=pl.BlockSpec((seq, gw), lambda b, k: (b, k)),
        out_shape=jax.ShapeDtypeStruct((batch * seq, g * gw), BF16),
        scratch_shapes=[
            pltpu.VMEM((n_all * SSM_CHUNK, SSM_CHUNK), F32),
            pltpu.VMEM((4 * HEADS_PER_GROUP * n_lat, SSM_CHUNK), F32),
            pltpu.VMEM((n_all * SSM_CHUNK, 2 * gw + LANE_PAD), BF16),
            pltpu.VMEM((seq, 2 * gw + LANE_PAD), F32),
            pltpu.VMEM((seq, HEADS_PER_GROUP * SSM_CHUNK + LANE_PAD), F32),
            pltpu.VMEM((n_lat, SSM_CHUNK, SSM_CHUNK), F32),
            pltpu.VMEM((n_all, n, 2 * gw + LANE_PAD), F32),
            pltpu.VMEM((n_lat, n, 2 * gw + LANE_PAD), BF16),
            pltpu.VMEM((2, n_all, 1, gw), F32),
        ],
        compiler_params=_cparams(2),
        name="ssd_scan",
    )(xc_l, bt_l, xc_l, dt_l, xc_c, bt_c, dt_c, sel_w, sel_e, sel_c)


def _mlp_tail(x1, g_ref, sh_ref, sc_ref, gate_ref, w1_ref, w2_ref, nff):
    h2 = _norm_mod(x1, g_ref[...], sh_ref[...], sc_ref[...]).astype(BF16)
    d = x1.shape[1]
    dff = w2_ref.shape[0]
    acc = None
    for j in range(dff // nff):
        c0 = j * nff
        a = jnp.dot(h2, w1_ref[:, c0:c0 + nff], preferred_element_type=F32)
        a = jnp.square(jnp.maximum(a, 0.0)).astype(BF16)
        p = jnp.dot(a, w2_ref[c0:c0 + nff, 0:d], preferred_element_type=F32)
        acc = p if acc is None else acc + p
    return x1 + gate_ref[...] * acc


def _ssd_out_kernel(y_ref, xs_ref, z_ref, x_ref, dsk_ref, sng_ref, gm_ref, shf_ref, scf_ref, gf_ref,
                    ng_ref, wo_ref, w1_ref, w2_ref, o_ref):
    acc = None
    for g0 in range(0, y_ref.shape[1], GROUP_WIDTH):
        cols = slice(g0, g0 + GROUP_WIDTH)
        y = y_ref[:, cols].astype(F32) + dsk_ref[:, cols] * xs_ref[:, cols].astype(F32)
        y = y * _silu(z_ref[:, cols]).astype(F32)
        y = y * lax.rsqrt(jnp.mean(y * y, axis=-1, keepdims=True) + EPS)
        yn = (y * sng_ref[:, cols]).astype(BF16)
        p = jnp.dot(yn, wo_ref[cols, 0:x_ref.shape[1]], preferred_element_type=F32)
        acc = p if acc is None else acc + p
    x1 = x_ref[...] + gm_ref[...] * acc
    o_ref[...] = _mlp_tail(x1, ng_ref, shf_ref, scf_ref, gf_ref, w1_ref, w2_ref, 1024)


def _mod_spec(d, row_of_tile, k):
    return pl.BlockSpec((None, 1, d), lambda i: (row_of_tile(i), 0, k))


def _ssd_out_mlp(y2d, xc2d, z2d, x2d, dskip_row, ssd_ng_row, mod3, row_of_tile, norm_g, wo, w1, w2, *, tm):
    rows, d = x2d.shape
    di = y2d.shape[1]
    return pl.pallas_call(
        _ssd_out_kernel,
        grid=(rows // tm,),
        in_specs=[
            pl.BlockSpec((tm, di), lambda i: (i, 0)),
            pl.BlockSpec((tm, di), lambda i: (i, 0)),
            pl.BlockSpec((tm, di), lambda i: (i, 0)),
            pl.BlockSpec((tm, d), lambda i: (i, 0)),
            _const_spec((1, di)),
            _const_spec((1, di)),
            _mod_spec(d, row_of_tile, 2),
            _mod_spec(d, row_of_tile, 3),
            _mod_spec(d, row_of_tile, 4),
            _mod_spec(d, row_of_tile, 5),
            _const_spec((1, d)),
            _const_spec(wo.shape),
            _const_spec(w1.shape),
            _const_spec(w2.shape),
        ],
        out_specs=pl.BlockSpec((tm, d), lambda i: (i, 0)),
        out_shape=jax.ShapeDtypeStruct((rows, d), F32),
        compiler_params=_cparams(1),
        name="ssd_out_mlp",
    )(y2d, xc2d, z2d, x2d, dskip_row, ssd_ng_row, mod3, mod3, mod3, mod3, norm_g, wo, w1, w2)


def _sc_layer_kernel(x_ref, shm_ref, scm_ref, gm_ref, shf_ref, scf_ref, gf_ref, ngm_ref, ngf_ref,
                     fg_ref, wi_ref, cw_ref, cwm_ref, wo_ref, w1_ref, w2_ref, o_ref, *, period):
    x = x_ref[...]
    h = _norm_mod(x, ngm_ref[...], shm_ref[...], scm_ref[...]).astype(BF16)
    w = wo_ref.shape[0]
    bg = jnp.dot(h, wi_ref[:, 0:w], preferred_element_type=F32)
    cg = jnp.dot(h, wi_ref[:, w:2 * w], preferred_element_type=F32)
    xv = jnp.dot(h, wi_ref[:, 2 * w:3 * w], preferred_element_type=F32)
    u = (bg * _row_conv3(cg * xv, cw_ref, cwm_ref, slice(0, w), period)).astype(BF16)
    y = jnp.dot(u, wo_ref[:, 0:x.shape[1]], preferred_element_type=F32)
    x1 = x + gm_ref[...] * y
    x2 = _mlp_tail(x1, ngf_ref, shf_ref, scf_ref, gf_ref, w1_ref, w2_ref, 1024)
    ms = jnp.mean(x2 * x2, axis=-1, keepdims=True)
    o_ref[...] = x2 * lax.rsqrt(ms + EPS) * fg_ref[...]


def _sc_layer(x2d, mod3, row_of_tile, ng_mix, ng_mlp, final_g, wi, conv_w, wo, w1, w2, *, tm, period):
    rows, d = x2d.shape
    kern = functools.partial(_sc_layer_kernel, period=period)
    return pl.pallas_call(
        kern,
        grid=(rows // tm,),
        in_specs=[
            pl.BlockSpec((tm, d), lambda i: (i, 0)),
            _mod_spec(d, row_of_tile, 0),
            _mod_spec(d, row_of_tile, 1),
            _mod_spec(d, row_of_tile, 2),
            _mod_spec(d, row_of_tile, 3),
            _mod_spec(d, row_of_tile, 4),
            _mod_spec(d, row_of_tile, 5),
            _const_spec((1, d)),
            _const_spec((1, d)),
            _const_spec((1, d)),
            _const_spec(wi.shape),
            _const_spec(conv_w.shape),
            _const_spec((2 * period, conv_w.shape[1])),
            _const_spec(wo.shape),
            _const_spec(w1.shape),
            _const_spec(w2.shape),
        ],
        out_specs=pl.BlockSpec((tm, d), lambda i: (i, 0)),
        out_shape=jax.ShapeDtypeStruct((rows, d), F32),
        compiler_params=_cparams(1),
        name="shortconv_layer",
    )(x2d, mod3, mod3, mod3, mod3, mod3, mod3, ng_mix, ng_mlp, final_g, wi, conv_w,
      _edge_masked_taps(conv_w, period), wo, w1, w2)


def kernel(x, c, ctx, c_ctx, ada_w, ada_b, norm_mix_g, norm_mlp_g, ssd_w_in, ssd_conv_w, ssd_conv_b,
           ssd_dt_bias, ssd_a_log, ssd_d, ssd_norm_g, ssd_w_out, sc_w_in, sc_conv_w, sc_w_out,
           mlp_w1, mlp_w2, final_norm_g):
    batch, seq, d = x.shape
    ctx_len = ctx.shape[1]
    depth = ada_w.shape[0]
    assert depth == 2 and ssd_w_in.shape[0] == 1 and sc_w_in.shape[0] == 1
    d_inner = ssd_w_out.shape[1]
    n_heads = ssd_d.shape[1]
    xbc_dim = ssd_conv_w.shape[2]
    assert n_heads == SSM_GROUPS * HEADS_PER_GROUP and d_inner == SSM_GROUPS * GROUP_WIDTH

    mod_rows = 16
    cvec = jnp.zeros((mod_rows, d), F32).at[:batch].set(c).at[batch].set(c_ctx)
    mod = _modulation(cvec, ada_w, ada_b)
    mod0 = mod[0].reshape(mod_rows, 1, 6 * d)
    mod1 = mod[1].reshape(mod_rows, 1, 6 * d)

    nbc = SSM_GROUPS * SSM_STATE
    x_rng = (d_inner, 2 * d_inner)
    b_rng = (2 * d_inner, 2 * d_inner + nbc)
    c_rng = (2 * d_inner + nbc, 2 * d_inner + 2 * nbc)
    n_proj = ssd_w_in.shape[2]
    w_t = _bf16_padded(jnp.swapaxes(ssd_w_in[0], 0, 1), block_rows=n_proj // 4)
    cw = ssd_conv_w[0]
    cbias = ssd_conv_b[0]
    cw_xc = jnp.concatenate([cw[:, :d_inner], cw[:, d_inner + nbc:]], axis=1)
    cb_xc = jnp.concatenate([cbias[:d_inner], cbias[d_inner + nbc:]]).reshape(1, -1)
    cwb = jnp.concatenate([cw[:, d_inner:d_inner + nbc].T, cbias[d_inner:d_inner + nbc, None],
                           jnp.zeros((nbc, 4), F32)], axis=1)
    gi = jnp.arange(SSM_GROUPS)[:, None, None]
    di_ = jnp.arange(2)[None, :, None]
    ri = jnp.arange(HEADS_PER_GROUP)[None, None, :]
    flat = (di_ * n_heads + gi * HEADS_PER_GROUP + ri).reshape(SSM_GROUPS, 2 * HEADS_PER_GROUP)
    dt_idx = jnp.concatenate([flat, flat], axis=1).reshape(-1)
    wdt_t = w_t[d_inner + xbc_dim:, :d][dt_idx]
    dtb_col = ssd_dt_bias[0].reshape(-1)[dt_idx].reshape(-1, 1).astype(F32)
    alog_col = ssd_a_log[0].reshape(-1)[dt_idx].reshape(-1, 1).astype(F32)
    ng_mix0 = norm_mix_g[0].reshape(1, d)

    tm = 512
    x2d = x.reshape(batch * seq, d)
    ctx2d = ctx.reshape(batch * ctx_len, d)
    lat_row = lambda i: (i * tm) // seq
    tm_in = 256
    z_l, xc_l, bt_l, dt_l = _ssd_in_proj(x2d, mod0, lambda i: (i * tm_in) // seq, ng_mix0, w_t,
                                         wdt_t, cw_xc, cb_xc, cwb, dtb_col, alog_col, tm=tm_in,
                                         period=GRID_W, want_z=True, dz=d_inner, xc_ranges=(x_rng, c_rng),
                                         b_range=b_rng)
    xc_c, bt_c, dt_c = _ssd_in_proj(ctx2d, mod0, lambda i: batch, ng_mix0, w_t, wdt_t,
                                    cw_xc[:, :d_inner], cb_xc[:, :d_inner], cwb, dtb_col, alog_col,
                                    tm=ctx_len, period=ctx_len, want_z=False, dz=d_inner,
                                    xc_ranges=(x_rng,), b_range=b_rng)

    dskip_row = jnp.repeat(ssd_d[0].astype(F32), SSM_HEAD_DIM).reshape(1, d_inner)
    ng_row = ssd_norm_g[0].reshape(1, d_inner)
    y_ssd = _ssd_scan(xc_l, bt_l, dt_l, xc_c, bt_c, dt_c, batch=batch, seq=seq, ctx_len=ctx_len)

    x1 = _ssd_out_mlp(y_ssd, xc_l, z_l, x2d, dskip_row, ng_row, mod0, lat_row, norm_mlp_g[0].reshape(1, d),
                      ssd_w_out[0].astype(BF16), mlp_w1[0].astype(BF16), mlp_w2[0].astype(BF16), tm=tm)

    out = _sc_layer(x1, mod1, lat_row, norm_mix_g[1].reshape(1, d), norm_mlp_g[1].reshape(1, d),
                    final_norm_g.reshape(1, d), sc_w_in[0].astype(BF16), sc_conv_w[0],
                    sc_w_out[0].astype(BF16), mlp_w1[1].astype(BF16), mlp_w2[1].astype(BF16),
                    tm=tm, period=GRID_W)
    return out.reshape(batch, seq, d)
```
